```python
import math
import jax, jax.numpy as jnp
from jax import lax
import numpy as np

D_MODEL = 1024
BATCH = 4
SEQ = 8192
DEPTH = 2

N_A_LAYERS = DEPTH // 2
N_B_LAYERS = DEPTH - N_A_LAYERS
D_FF = 2816
CONV_WIDTH = 3
N_HEADS = 16
N_KV_GROUPS = 4
HEADS_PER_GROUP = N_HEADS // N_KV_GROUPS
HEAD_DIM = D_MODEL // N_HEADS
N_BRANCH = 3
CMP_BLOCK = 32
CMP_STRIDE = 16
CMP_HIDDEN = 2 * HEAD_DIM
SLC_BLOCK = 64
N_SELECT = 16
WINDOW = 512
Q_BLOCK = 128
N_BUCKETS = 32
MAX_EXACT = N_BUCKETS // 2
MAX_DISTANCE = 4096
EPS = 1e-6
NEG = -1e30
BIG = 1e30

kernel_name = "yoco_shortconv_nsa_macaron"


def rmsnorm(x, g):
    xf = x.astype(jnp.float32)
    y = xf * lax.rsqrt(jnp.mean(xf * xf, axis=-1, keepdims=True) + EPS)
    return (y * g.astype(jnp.float32)).astype(x.dtype)


def swiglu_ffn(x, w_in, w_out):
    gate, up = jnp.split(x @ w_in, 2, axis=-1)
    return (jax.nn.silu(gate) * up) @ w_out


def rel_bucket(rel):
    n = jnp.maximum(rel, 0)
    nf = jnp.maximum(n, 1).astype(jnp.float32)
    large = MAX_EXACT + (jnp.log(nf / MAX_EXACT) / math.log(MAX_DISTANCE / MAX_EXACT)
                         * (N_BUCKETS - MAX_EXACT)).astype(jnp.int32)
    large = jnp.minimum(large, N_BUCKETS - 1)
    return jnp.where(n < MAX_EXACT, n, large)


def short_conv_mixer(x, w_in, conv_w, w_out):
    b_gate, c_gate, v = jnp.split(x @ w_in, 3, axis=-1)
    u = c_gate * v
    y = lax.conv_general_dilated(u, conv_w[:, None, :], window_strides=(1,),
                                 padding=[(CONV_WIDTH - 1, 0)],
                                 dimension_numbers=('NWC', 'WIO', 'NWC'),
                                 feature_group_count=D_MODEL)
    return (b_gate * y) @ w_out


def compress(raw, pe, w1, w2):
    bsz, s = raw.shape[0], raw.shape[1]
    nc = (s - CMP_BLOCK) // CMP_STRIDE + 1
    idx = jnp.arange(nc)[:, None] * CMP_STRIDE + jnp.arange(CMP_BLOCK)[None, :]
    blk = raw[:, idx] + pe[None, None, :, None, :]
    blk = blk.transpose(0, 3, 1, 2, 4).reshape(bsz, N_KV_GROUPS, nc, CMP_BLOCK * HEAD_DIM)
    return jax.nn.gelu(blk @ w1) @ w2


def shared_kv(h, kv_norm, w_kv, k_norm, pe_k, pe_v, w1_k, w2_k, w1_v, w2_v):
    bsz, s, _ = h.shape
    kv = (rmsnorm(h, kv_norm) @ w_kv).reshape(bsz, s, N_BRANCH, 2, N_KV_GROUPS, HEAD_DIM)
    k_c = rmsnorm(compress(kv[:, :, 0, 0], pe_k, w1_k, w2_k), k_norm[0])
    v_c = compress(kv[:, :, 0, 1], pe_v, w1_v, w2_v)
    ns = s // SLC_BLOCK
    k_s = rmsnorm(kv[:, :, 1, 0], k_norm[1]).transpose(0, 2, 1, 3).reshape(bsz, N_KV_GROUPS, ns, SLC_BLOCK, HEAD_DIM)
    v_s = kv[:, :, 1, 1].transpose(0, 2, 1, 3).reshape(bsz, N_KV_GROUPS, ns, SLC_BLOCK, HEAD_DIM)
    pad = ((0, 0), (0, 0), (WINDOW, 0), (0, 0))
    k_w = jnp.pad(rmsnorm(kv[:, :, 2, 0], k_norm[2]).transpose(0, 2, 1, 3), pad)
    v_w = jnp.pad(kv[:, :, 2, 1].transpose(0, 2, 1, 3), pad)
    return k_c, v_c, k_s, v_s, k_w, v_w


def masked_softmax(s, mask):
    return jax.nn.softmax(jnp.where(mask, s.astype(jnp.float32), NEG), axis=-1)


def nsa_mixer(x, w_q, q_norm, w_o, rel_bias, kv):
    k_c, v_c, k_s, v_s, k_w, v_w = kv
    bsz, s, _ = x.shape
    G, HG, dh = N_KV_GROUPS, HEADS_PER_GROUP, HEAD_DIM
    proj = x @ w_q
    q = rmsnorm(proj[..., :N_HEADS * dh].reshape(bsz, s, G, HG, dh), q_norm)
    q = q.transpose(0, 2, 3, 1, 4)
    gates = jax.nn.sigmoid(proj[..., N_HEADS * dh:].astype(jnp.float32))
    gates = gates.reshape(bsz, s, G, HG, N_BRANCH).transpose(0, 2, 3, 1, 4).astype(x.dtype)
    nc, ns = k_c.shape[2], k_s.shape[2]
    n_sel = min(N_SELECT, ns)
    scale = HEAD_DIM ** -0.5
    c_start = jnp.arange(nc) * CMP_STRIDE
    c_end = c_start + CMP_BLOCK - 1
    s_start = jnp.arange(ns) * SLC_BLOCK
    overlap = (jnp.clip(jnp.minimum(c_start[:, None] + CMP_BLOCK, s_start[None, :] + SLC_BLOCK)
                        - jnp.maximum(c_start[:, None], s_start[None, :]), 0) / CMP_STRIDE).astype(jnp.float32)
    bias_grp = rel_bias.reshape(N_BUCKETS, G, HG).transpose(1, 0, 2)
    bi = jnp.arange(bsz)[:, None, None, None]
    gi = jnp.arange(G)[None, :, None, None]
    blk_ids = jnp.arange(ns)

    def head_bias(rel):
        return rel_bias[rel_bucket(rel)].reshape(rel.shape + (G, HG)).transpose(2, 3, 0, 1)

    def block(qb):
        q0 = qb * Q_BLOCK
        qblk = lax.dynamic_slice_in_dim(q, q0, Q_BLOCK, axis=3)
        gblk = lax.dynamic_slice_in_dim(gates, q0, Q_BLOCK, axis=3)
        t = q0 + jnp.arange(Q_BLOCK)
        rel_c = t[:, None] - c_end[None, :]
        s_c = jnp.einsum('bghqd,bgnd->bghqn', qblk, k_c) * scale + head_bias(rel_c)
        p_c = masked_softmax(s_c, rel_c >= 0)
        p_c = jnp.where((t >= CMP_BLOCK - 1)[:, None], p_c, 0.0)
        o_c = jnp.einsum('bghqn,bgnd->bghqd', p_c.astype(v_c.dtype), v_c)
        imp = jnp.einsum('bghqn,nj->bgqj', p_c, overlap)
        blk_t = t // SLC_BLOCK
        valid = blk_ids[None, :] <= blk_t[:, None]
        forced = ((blk_ids[None, :] == 0) | (blk_ids[None, :] == blk_t[:, None])
                  | (blk_ids[None, :] == blk_t[:, None] - 1))
        score = jnp.where(valid, jnp.where(forced, BIG, imp), NEG)
        _, idx = lax.top_k(score, n_sel)
        k_sel = k_s[bi, gi, idx].reshape(bsz, G, Q_BLOCK, n_sel * SLC_BLOCK, dh)
        v_sel = v_s[bi, gi, idx].reshape(bsz, G, Q_BLOCK, n_sel * SLC_BLOCK, dh)
        pos = (idx[..., None] * SLC_BLOCK + jnp.arange(SLC_BLOCK)).reshape(bsz, G, Q_BLOCK, n_sel * SLC_BLOCK)
        rel_s = t[None, None, :, None] - pos
        bias_s = bias_grp[gi, rel_bucket(rel_s)].transpose(0, 1, 4, 2, 3)
        s_s = jnp.einsum('bghqd,bgqkd->bghqk', qblk, k_sel) * scale + bias_s
        p_s = masked_softmax(s_s, (rel_s >= 0)[:, :, None])
        o_s = jnp.einsum('bghqk,bgqkd->bghqd', p_s.astype(v_sel.dtype), v_sel)
        k_wb = lax.dynamic_slice_in_dim(k_w, q0, Q_BLOCK + WINDOW, axis=2)
        v_wb = lax.dynamic_slice_in_dim(v_w, q0, Q_BLOCK + WINDOW, axis=2)
        s_pos = q0 - WINDOW + jnp.arange(Q_BLOCK + WINDOW)
        rel_w = t[:, None] - s_pos[None, :]
        mask_w = (rel_w >= 0) & (rel_w < WINDOW) & (s_pos >= 0)[None, :]
        s_w = jnp.einsum('bghqd,bgkd->bghqk', qblk, k_wb) * scale + head_bias(rel_w)
        p_w = masked_softmax(s_w, mask_w)
        o_w = jnp.einsum('bghqk,bgkd->bghqd', p_w.astype(v_wb.dtype), v_wb)
        return gblk[..., 0:1] * o_c + gblk[..., 1:2] * o_s + gblk[..., 2:3] * o_w

    out = lax.map(block, jnp.arange(s // Q_BLOCK))
    out = out.transpose(1, 0, 4, 2, 3, 5).reshape(bsz, s, N_HEADS * dh)
    return out @ w_o


def setup_inputs(seed: int = 0) -> dict:
    key = jax.random.key(seed)
    ks = jax.random.split(key, 32)

    def w(k, shape, fan_in):
        return jax.random.normal(k, shape, jnp.float32) * (fan_in ** -0.5)

    def gain(k, shape):
        return 1.0 + 0.02 * jax.random.normal(k, shape, jnp.float32)

    q_cols = N_HEADS * HEAD_DIM + N_BRANCH * N_HEADS
    kv_cols = N_BRANCH * 2 * N_KV_GROUPS * HEAD_DIM
    return {
        'x': jax.random.normal(ks[0], (BATCH, SEQ, D_MODEL), jnp.float32),
        'ffn1_norm': gain(ks[1], (DEPTH, D_MODEL)),
        'ffn1_w_in': w(ks[2], (DEPTH, D_MODEL, 2 * D_FF), D_MODEL),
        'ffn1_w_out': w(ks[3], (DEPTH, D_FF, D_MODEL), D_FF),
        'mix_norm': gain(ks[4], (DEPTH, D_MODEL)),
        'ffn2_norm': gain(ks[5], (DEPTH, D_MODEL)),
        'ffn2_w_in': w(ks[6], (DEPTH, D_MODEL, 2 * D_FF), D_MODEL),
        'ffn2_w_out': w(ks[7], (DEPTH, D_FF, D_MODEL), D_FF),
        'conv_w_in': w(ks[8], (N_A_LAYERS, D_MODEL, 3 * D_MODEL), D_MODEL),
        'conv_w': w(ks[9], (N_A_LAYERS, CONV_WIDTH, D_MODEL), CONV_WIDTH),
        'conv_w_out': w(ks[10], (N_A_LAYERS, D_MODEL, D_MODEL), D_MODEL),
        'attn_w_q': w(ks[11], (N_B_LAYERS, D_MODEL, q_cols), D_MODEL),
        'attn_q_norm': gain(ks[12], (N_B_LAYERS, HEAD_DIM)),
        'attn_w_o': w(ks[13], (N_B_LAYERS, N_HEADS * HEAD_DIM, D_MODEL), N_HEADS * HEAD_DIM),
        'kv_norm': gain(ks[14], (D_MODEL,)),
        'w_kv': w(ks[15], (D_MODEL, kv_cols), D_MODEL),
        'k_norm': gain(ks[16], (N_BRANCH, HEAD_DIM)),
        'cmp_pe_k': 0.1 * jax.random.normal(ks[17], (CMP_BLOCK, HEAD_DIM), jnp.float32),
        'cmp_pe_v': 0.1 * jax.random.normal(ks[18], (CMP_BLOCK, HEAD_DIM), jnp.float32),
        'cmp_w1_k': w(ks[19], (CMP_BLOCK * HEAD_DIM, CMP_HIDDEN), CMP_BLOCK * HEAD_DIM),
        'cmp_w2_k': w(ks[20], (CMP_HIDDEN, HEAD_DIM), CMP_HIDDEN),
        'cmp_w1_v': w(ks[21], (CMP_BLOCK * HEAD_DIM, CMP_HIDDEN), CMP_BLOCK * HEAD_DIM),
        'cmp_w2_v': w(ks[22], (CMP_HIDDEN, HEAD_DIM), CMP_HIDDEN),
        'rel_bias': 0.5 * jax.random.normal(ks[23], (N_BUCKETS, N_HEADS), jnp.float32),
    }


def reference(x, ffn1_norm, ffn1_w_in, ffn1_w_out, mix_norm, ffn2_norm, ffn2_w_in, ffn2_w_out,
              conv_w_in, conv_w, conv_w_out, attn_w_q, attn_q_norm, attn_w_o,
              kv_norm, w_kv, k_norm, cmp_pe_k, cmp_pe_v, cmp_w1_k, cmp_w2_k, cmp_w1_v, cmp_w2_v,
              rel_bias):
    h = x
    kv = None
    for i in range(DEPTH):
        if i == N_A_LAYERS:
            kv = shared_kv(h, kv_norm, w_kv, k_norm, cmp_pe_k, cmp_pe_v,
                           cmp_w1_k, cmp_w2_k, cmp_w1_v, cmp_w2_v)
        h = h + 0.5 * swiglu_ffn(rmsnorm(h, ffn1_norm[i]), ffn1_w_in[i], ffn1_w_out[i])
        hn = rmsnorm(h, mix_norm[i])
        if i < N_A_LAYERS:
            h = h + short_conv_mixer(hn, conv_w_in[i], conv_w[i], conv_w_out[i])
        else:
            j = i - N_A_LAYERS
            h = h + nsa_mixer(hn, attn_w_q[j], attn_q_norm[j], attn_w_o[j], rel_bias, kv)
        h = h + 0.5 * swiglu_ffn(rmsnorm(h, ffn2_norm[i]), ffn2_w_in[i], ffn2_w_out[i])
    return h
```

```python
import functools
import math

import jax
import jax.numpy as jnp
from jax import lax
from jax.experimental import pallas as pl
from jax.experimental.pallas import tpu as pltpu

F32 = jnp.float32
BF16 = jnp.bfloat16

N_HEADS = 16
N_KV_GROUPS = 4
HEADS_PER_GROUP = N_HEADS // N_KV_GROUPS
HEAD_DIM = 64
N_BRANCH = 3
CMP_BLOCK = 32
CMP_STRIDE = 16
SLC_BLOCK = 64
N_SELECT = 16
WINDOW = 512
N_BUCKETS = 32
MAX_EXACT = N_BUCKETS // 2
MAX_DISTANCE = 4096
EPS = 1e-6
NEG = -1e30
BIG = 1e30
REMOVED = -3e38

LANES = 128
SUBLANES = 8
VMEM_LIMIT = 56 * 1024 * 1024

QT = 128
KCHUNK = 512
N_SUB = KCHUNK // QT
WIN_KEYS = WINDOW + QT
N_WSUB = WIN_KEYS // QT
FAR_TILE = int(math.ceil((MAX_EXACT * (MAX_DISTANCE / MAX_EXACT) ** ((N_BUCKETS - MAX_EXACT - 1) / (N_BUCKETS - MAX_EXACT))
                          + QT) / QT))
MASK_TILE = FAR_TILE + 1
WEDGE_TILE = FAR_TILE + 2
N_TILES = FAR_TILE + 3
FLAG_LANE = HEAD_DIM


def _cparams(sem):
    return pltpu.CompilerParams(dimension_semantics=sem, vmem_limit_bytes=VMEM_LIMIT)


def _const_spec(shape):
    nd = len(shape)
    return pl.BlockSpec(shape, lambda *_: (0,) * nd, pipeline_mode=pl.Buffered(1))


def _rms(x, g, n):
    ms = jnp.sum(x * x, axis=-1, keepdims=True) * (1.0 / n)
    return x * lax.rsqrt(ms + EPS) * g


def _dot(a, b):
    return jnp.dot(a, b, preferred_element_type=F32)


def _dot_t(a, b):
    return lax.dot_general(a, b, (((1,), (1,)), ((), ())), preferred_element_type=F32)


def _ffn_body(x_ref, g_ref, wg_ref, wu_ref, wo_ref, o_ref, acc_ref, *, n_chunks, d_model):
    x = x_ref[...]
    xn = _rms(x, g_ref[...], d_model).astype(BF16)
    acc_ref[...] = jnp.zeros_like(acc_ref)

    def chunk(c, carry):
        gate = _dot(xn, wg_ref[c])
        up = _dot(xn, wu_ref[c])
        act = (gate * (1.0 / (1.0 + jnp.exp(-gate))) * up).astype(BF16)
        acc_ref[...] += _dot(act, wo_ref[c])
        return carry

    lax.fori_loop(0, n_chunks, chunk, 0)
    o_ref[...] = x + 0.5 * acc_ref[...]


def _ffn(h, norm_g, w_in, w_out, *, tm=512, ck=256):
    rows, d = h.shape
    dff = w_out.shape[0]
    nch = dff // ck
    wg = w_in[:, :dff].reshape(d, nch, ck).transpose(1, 0, 2).astype(BF16)
    wu = w_in[:, dff:].reshape(d, nch, ck).transpose(1, 0, 2).astype(BF16)
    wo = w_out.reshape(nch, ck, d).astype(BF16)
    return pl.pallas_call(
        functools.partial(_ffn_body, n_chunks=nch, d_model=d),
        grid=(rows // tm,),
        in_specs=[
            pl.BlockSpec((tm, d), lambda i: (i, 0)),
            _const_spec((1, d)),
            _const_spec((nch, d, ck)),
            _const_spec((nch, d, ck)),
            _const_spec((nch, ck, d)),
        ],
        out_specs=pl.BlockSpec((tm, d), lambda i: (i, 0)),
        out_shape=jax.ShapeDtypeStruct((rows, d), F32),
        scratch_shapes=[pltpu.VMEM((tm, d), F32)],
        compiler_params=_cparams(("parallel",)),
        name="ffn",
    )(h, norm_g.reshape(1, d), wg, wu, wo)


def _conv_body(x_ref, g_ref, win_ref, cw_ref, wout_ref, o_ref, ubuf_ref, *, tm, d_model):
    i = pl.program_id(1)
    x = x_ref[0]
    xn = _rms(x, g_ref[...], d_model).astype(BF16)
    b_gate = _dot(xn, win_ref[0])
    c_gate = _dot(xn, win_ref[1])
    v = _dot(xn, win_ref[2])
    u = c_gate * v

    @pl.when(i == 0)
    def _():
        ubuf_ref[0:SUBLANES, :] = jnp.zeros((SUBLANES, d_model), F32)

    ubuf_ref[SUBLANES:SUBLANES + tm, :] = u
    u1 = ubuf_ref[SUBLANES - 1:SUBLANES - 1 + tm, :]
    u2 = ubuf_ref[SUBLANES - 2:SUBLANES - 2 + tm, :]
    y = cw_ref[2:3, :] * u + cw_ref[1:2, :] * u1 + cw_ref[0:1, :] * u2
    ubuf_ref[0:SUBLANES, :] = u[tm - SUBLANES:, :]
    o_ref[0] = x + _dot((b_gate * y).astype(BF16), wout_ref[...])


def _conv_mixer(h, norm_g, w_in, conv_w, w_out, *, batch, tm=256):
    rows, d = h.shape
    seq = rows // batch
    win = w_in.reshape(d, 3, d).transpose(1, 0, 2).astype(BF16)
    out = pl.pallas_call(
        functools.partial(_conv_body, tm=tm, d_model=d),
        grid=(batch, seq // tm),
        in_specs=[
            pl.BlockSpec((1, tm, d), lambda b, i: (b, i, 0)),
            _const_spec((1, d)),
            _const_spec((3, d, d)),
            _const_spec((conv_w.shape[0], d)),
            _const_spec((d, d)),
        ],
        out_specs=pl.BlockSpec((1, tm, d), lambda b, i: (b, i, 0)),
        out_shape=jax.ShapeDtypeStruct((batch, seq, d), F32),
        scratch_shapes=[pltpu.VMEM((tm + SUBLANES, d), F32)],
        compiler_params=_cparams(("arbitrary", "arbitrary")),
        name="conv_mixer",
    )(h.reshape(batch, seq, d), norm_g.reshape(1, d), win, conv_w, w_out.astype(BF16))
    return out.reshape(rows, d)


def _pad_head_cols(w, n_slices):
    d = w.shape[0]
    w = w.reshape(d, n_slices, HEAD_DIM)
    return jnp.pad(w, ((0, 0), (0, 0), (0, LANES - HEAD_DIM))).reshape(d, n_slices * LANES)


def _pad_lanes(v):
    return jnp.pad(v, [(0, 0)] * (v.ndim - 1) + [(0, LANES - HEAD_DIM)])


def _kvproj_body(x_ref, g_ref, w_ref, kn_ref, kc_ref, vc_ref, ks_ref, vs_ref, kw_ref, vw_ref, *, tm, d_model):
    i = pl.program_id(1)
    xn = _rms(x_ref[0], g_ref[...], d_model).astype(BF16)
    lane = lax.broadcasted_iota(jnp.int32, (tm, LANES), 1)
    ones_col = jnp.where(lane == FLAG_LANE, 1.0, 0.0)
    blk = (i * tm + lax.broadcasted_iota(jnp.int32, (tm, LANES), 0)) // SLC_BLOCK
    onehot = jnp.where(lane == blk, 1.0, 0.0).astype(BF16)
    G = N_KV_GROUPS
    for g in range(G):
        def col(br, kv):
            return _dot(xn, w_ref[(br * 2 + kv) * G + g])
        kc_ref[0, g] = col(0, 0)
        vc_ref[0, g] = col(0, 1)
        ks = _rms(col(1, 0), kn_ref[1:2, :], HEAD_DIM).astype(BF16)
        ks_ref[0, g] = jnp.concatenate([ks, onehot], axis=1)
        vs_ref[0, g] = (col(1, 1) + ones_col).astype(BF16)
        kw_ref[0, g] = _rms(col(2, 0), kn_ref[2:3, :], HEAD_DIM).astype(BF16)
        vw_ref[0, g] = (col(2, 1) + ones_col).astype(BF16)


def _kv_proj(h, kv_norm, w_kv, k_norm, *, batch, tm=256):
    rows, d = h.shape
    seq = rows // batch
    G = N_KV_GROUPS
    n_sl = N_BRANCH * 2 * G
    w = _pad_head_cols(w_kv, n_sl).reshape(d, n_sl, LANES).transpose(1, 0, 2).astype(BF16)
    kn = _pad_lanes(k_norm)
    sds = lambda lanes, dt: jax.ShapeDtypeStruct((batch, G, seq, lanes), dt)
    ospec = lambda lanes: pl.BlockSpec((1, G, tm, lanes), lambda b, i: (b, 0, i, 0))
    return pl.pallas_call(
        functools.partial(_kvproj_body, tm=tm, d_model=d),
        grid=(batch, seq // tm),
        in_specs=[
            pl.BlockSpec((1, tm, d), lambda b, i: (b, i, 0)),
            _const_spec((1, d)),
            _const_spec((n_sl, d, LANES)),
            _const_spec((N_BRANCH, LANES)),
        ],
        out_specs=[ospec(LANES), ospec(LANES), ospec(2 * LANES), ospec(LANES), ospec(LANES), ospec(LANES)],
        out_shape=[sds(LANES, F32), sds(LANES, F32), sds(2 * LANES, BF16), sds(LANES, BF16),
                   sds(LANES, BF16), sds(LANES, BF16)],
        compiler_params=_cparams(("parallel", "parallel")),
        name="kv_proj",
    )(h.reshape(batch, seq, d), kv_norm.reshape(1, d), w, kn)


def _gelu_tanh(x):
    return 0.5 * x * (1.0 + jnp.tanh(math.sqrt(2.0 / math.pi) * (x + 0.044715 * (x * x * x))))


def _compress_body(kr_ref, vr_ref, pek_ref, pev_ref, w1k_ref, w2k_ref, w1v_ref, w2v_ref, kn_ref,
                   kc_ref, vc_ref, *, ncw, front):
    row = lax.broadcasted_iota(jnp.int32, (ncw, LANES), 0)
    lane = lax.broadcasted_iota(jnp.int32, (front, LANES), 1)

    def mlp(r_ref, pe_ref, w1_ref, w2_ref):
        r = r_ref[0, 0]
        a = _dot((r + pe_ref[0:1, :]).astype(BF16), w1_ref[0])
        b = _dot((r + pe_ref[1:2, :]).astype(BF16), w1_ref[1])
        hid = a + pltpu.roll(b, ncw - 1, axis=0)
        out = _dot(_gelu_tanh(hid).astype(BF16), w2_ref[...])
        return jnp.where(row < ncw - 1, out, 0.0)

    kc = _rms(mlp(kr_ref, pek_ref, w1k_ref, w2k_ref), kn_ref[0:1, :], HEAD_DIM)
    vc = mlp(vr_ref, pev_ref, w1v_ref, w2v_ref)
    kc_ref[0, 0, 0:front, :] = jnp.where(lane == FLAG_LANE, 1.0, 0.0)
    vc_ref[0, 0, 0:front, :] = jnp.zeros((front, LANES), F32)
    kc_ref[0, 0, front:front + ncw, :] = kc
    vc_ref[0, 0, front:front + ncw, :] = vc


def _compress(kc_raw, vc_raw, pe_k, pe_v, w1_k, w2_k, w1_v, w2_v, k_norm):
    batch, G, seq, _ = kc_raw.shape
    ncw = seq // CMP_STRIDE
    front = ncw - QT // CMP_STRIDE
    tok = CMP_STRIDE * LANES
    hid = w1_k.shape[1]

    def prep_w1(w1):
        w = w1.reshape(CMP_BLOCK, HEAD_DIM, hid)
        w = jnp.pad(w, ((0, 0), (0, LANES - HEAD_DIM), (0, 0)))
        return w.reshape(2, tok, hid).astype(BF16)

    def prep_pe(pe):
        return _pad_lanes(pe).reshape(2, tok)

    def prep_w2(w2):
        return _pad_lanes(w2).astype(BF16)

    rspec = pl.BlockSpec((1, 1, ncw, tok), lambda b, g: (b, g, 0, 0))
    ospec = pl.BlockSpec((1, 1, front + ncw, LANES), lambda b, g: (b, g, 0, 0))
    osds = jax.ShapeDtypeStruct((batch, G, front + ncw, LANES), F32)
    return pl.pallas_call(
        functools.partial(_compress_body, ncw=ncw, front=front),
        grid=(batch, G),
        in_specs=[rspec, rspec, _const_spec((2, tok)), _const_spec((2, tok)),
                  _const_spec((2, tok, hid)), _const_spec((hid, LANES)),
                  _const_spec((2, tok, hid)), _const_spec((hid, LANES)),
                  _const_spec((N_BRANCH, LANES))],
        out_specs=[ospec, ospec],
        out_shape=[osds, osds],
        compiler_params=_cparams(("parallel", "parallel")),
        name="compress",
    )(kc_raw.reshape(batch, G, ncw, tok), vc_raw.reshape(batch, G, ncw, tok),
      prep_pe(pe_k), prep_pe(pe_v), prep_w1(w1_k), prep_w2(w2_k), prep_w1(w1_v), prep_w2(w2_v),
      _pad_lanes(k_norm))


def _qproj_body(x_ref, g_ref, wq_ref, wg_ref, qn_ref, q_ref, gt_ref, *, tm, d_model):
    xn = _rms(x_ref[0], g_ref[...], d_model).astype(BF16)
    lane = lax.broadcasted_iota(jnp.int32, (tm, LANES), 1)
    scale = HEAD_DIM ** -0.5
    for h in range(N_HEADS):
        q = _rms(_dot(xn, wq_ref[h]), qn_ref[...], HEAD_DIM) * scale
        q_ref[0, h] = jnp.where(lane == FLAG_LANE, NEG, q).astype(BF16)
    for g in range(N_KV_GROUPS):
        z = _dot(xn, wg_ref[g])
        gt_ref[0, :, g * LANES:(g + 1) * LANES] = 1.0 / (1.0 + jnp.exp(-z))


def _q_proj(h, norm_g, w_q, q_norm, *, batch, tm=256):
    rows, d = h.shape
    seq = rows // batch
    G, HG = N_KV_GROUPS, HEADS_PER_GROUP
    nq = N_HEADS * HEAD_DIM
    wq = _pad_head_cols(w_q[:, :nq], N_HEADS).reshape(d, N_HEADS, LANES).transpose(1, 0, 2).astype(BF16)
    ngc = HG * N_BRANCH
    wg = w_q[:, nq:].reshape(d, G, ngc)
    wg = jnp.pad(wg, ((0, 0), (0, 0), (0, LANES - ngc))).transpose(1, 0, 2).astype(BF16)
    return pl.pallas_call(
        functools.partial(_qproj_body, tm=tm, d_model=d),
        grid=(batch, seq // tm),
        in_specs=[
            pl.BlockSpec((1, tm, d), lambda b, i: (b, i, 0)),
            _const_spec((1, d)),
            _const_spec((N_HEADS, d, LANES)),
            _const_spec((G, d, LANES)),
            _const_spec((1, LANES)),
        ],
        out_specs=[pl.BlockSpec((1, N_HEADS, tm, LANES), lambda b, i: (b, 0, i, 0)),
                   pl.BlockSpec((1, tm, G * LANES), lambda b, i: (b, i, 0))],
        out_shape=[jax.ShapeDtypeStruct((batch, N_HEADS, seq, LANES), BF16),
                   jax.ShapeDtypeStruct((batch, seq, G * LANES), F32)],
        compiler_params=_cparams(("parallel", "parallel")),
        name="q_proj",
    )(h.reshape(batch, seq, d), norm_g.reshape(1, d), wq, wg, _pad_lanes(q_norm.reshape(1, HEAD_DIM)))


def _tables_body(thr_ref, rb_ref, tbl_ref, cb_ref, *, ncw, c_off):
    h = pl.program_id(0)

    def bias_of(rel):
        v = jnp.full(rel.shape, rb_ref[0, h], F32)
        for k in range(1, N_BUCKETS):
            v = jnp.where(rel >= thr_ref[k], rb_ref[k, h], v)
        return v

    base = (lax.broadcasted_iota(jnp.int32, (QT, QT), 0) - lax.broadcasted_iota(jnp.int32, (QT, QT), 1))

    def tile(d, carry):
        rel = base + d * QT
        tbl_ref[0, d] = jnp.where(rel < 0, NEG, bias_of(rel))
        return carry

    lax.fori_loop(0, FAR_TILE + 1, tile, 0)
    tbl_ref[0, MASK_TILE] = jnp.full((QT, QT), NEG, F32)
    rel = base + WINDOW
    tbl_ref[0, WEDGE_TILE] = jnp.where(rel >= WINDOW, NEG, bias_of(rel))
    relc = (lax.broadcasted_iota(jnp.int32, (QT, ncw), 0)
            - CMP_STRIDE * lax.broadcasted_iota(jnp.int32, (QT, ncw), 1) + c_off)
    cb_ref[0] = jnp.where(relc < 0, NEG, bias_of(relc))


def _bucket_thresholds(seq):
    n = jnp.arange(seq)
    nf = jnp.maximum(n, 1).astype(jnp.float32)
    large = MAX_EXACT + (jnp.log(nf / MAX_EXACT) / math.log(MAX_DISTANCE / MAX_EXACT)
                         * (N_BUCKETS - MAX_EXACT)).astype(jnp.int32)
    large = jnp.minimum(large, N_BUCKETS - 1)
    bucket = jnp.where(n < MAX_EXACT, n, large)
    return jnp.sum(bucket[None, :] < jnp.arange(N_BUCKETS)[:, None], axis=1).astype(jnp.int32)


def _bias_tables(rel_bias, seq):
    ncw = seq // CMP_STRIDE
    front = ncw - QT // CMP_STRIDE
    c_off = CMP_STRIDE * front - (CMP_BLOCK - 1)
    smem = pl.BlockSpec(memory_space=pltpu.SMEM)
    return pl.pallas_call(
        functools.partial(_tables_body, ncw=ncw, c_off=c_off),
        grid=(N_HEADS,),
        in_specs=[smem, smem],
        out_specs=[pl.BlockSpec((1, N_TILES, QT, QT), lambda h: (h, 0, 0, 0)),
                   pl.BlockSpec((1, QT, ncw), lambda h: (h, 0, 0))],
        out_shape=[jax.ShapeDtypeStruct((N_HEADS, N_TILES, QT, QT), F32),
                   jax.ShapeDtypeStruct((N_HEADS, QT, ncw), F32)],
        compiler_params=_cparams(("parallel",)),
        name="bias_tables",
    )(_bucket_thresholds(seq), rel_bias)


def _nsa_body(q_ref, gt_ref, ks_ref, vs_ref, kw_ref, vw_ref, kc_ref, vc_ref, ov_ref, tbl_ref, cb_ref,
              o_ref, *, ncw):
    HG = HEADS_PER_GROUP
    R = HG * QT
    qi = pl.program_id(2)
    q2 = q_ref[0].reshape(R, LANES)

    def bias_rows(ids):
        return jnp.concatenate(
            [jnp.concatenate([tbl_ref[h, t] for t in ids], axis=1) for h in range(HG)], axis=0)

    c0 = pl.multiple_of(qi * (QT // CMP_STRIDE), SUBLANES)
    kc = kc_ref[0, 0, pl.ds(c0, ncw), :].astype(BF16)
    vc = vc_ref[0, 0, pl.ds(c0, ncw), :].astype(BF16)
    s = _dot_t(q2, kc) + cb_ref[...].reshape(R, ncw)
    e = jnp.exp(s - jnp.max(s, axis=-1, keepdims=True))
    p = e / jnp.sum(e, axis=-1, keepdims=True)
    t_row = qi * QT + (lax.broadcasted_iota(jnp.int32, (R, 1), 0) & (QT - 1))
    p = jnp.where(t_row >= CMP_BLOCK - 1, p, 0.0)
    o_c = _dot(p.astype(BF16), vc)

    ov = ov_ref[pl.ds(c0, ncw), :].astype(BF16)
    p_hi = p.astype(BF16)
    p_lo = (p - p_hi.astype(F32)).astype(BF16)
    imp_h = _dot(p_hi, ov) + _dot(p_lo, ov)
    imp = imp_h[0:QT]
    for h in range(1, HG):
        imp = imp + imp_h[h * QT:(h + 1) * QT]
    blk = lax.broadcasted_iota(jnp.int32, (LANES, QT), 0)
    blk_t = (qi * QT + lax.broadcasted_iota(jnp.int32, (LANES, QT), 1)) // SLC_BLOCK
    forced = (blk == 0) | (blk == blk_t) | (blk == blk_t - 1)
    score = jnp.where(blk <= blk_t, jnp.where(forced, BIG, imp.T), NEG)
    sel = jnp.zeros((LANES, QT), F32)
    for _ in range(N_SELECT):
        top = jnp.max(score, axis=0, keepdims=True)
        first = jnp.min(jnp.where(score == top, blk, LANES), axis=0, keepdims=True)
        hit = blk == first
        sel = jnp.where(hit, 1.0, sel)
        score = jnp.where(hit, REMOVED, score)
    sel_neg = ((sel - 1.0) * BIG).T.astype(BF16)
    lhs = jnp.concatenate([q2, jnp.concatenate([sel_neg] * HG, axis=0)], axis=1)

    def sel_chunk(kj, carry):
        m, acc = carry
        k0 = pl.multiple_of(kj * KCHUNK, KCHUNK)
        ids = []
        for u in range(N_SUB):
            d = qi - kj * N_SUB - u
            ids.append(jnp.where(d < 0, MASK_TILE, jnp.minimum(d, FAR_TILE)))
        s = _dot_t(lhs, ks_ref[0, 0, pl.ds(k0, KCHUNK), :]) + bias_rows(ids)
        m_new = jnp.maximum(m, jnp.max(s, axis=-1, keepdims=True))
        pr = jnp.exp(s - m_new).astype(BF16)
        acc = jnp.exp(m - m_new) * acc + _dot(pr, vs_ref[0, 0, pl.ds(k0, KCHUNK), :])
        return m_new, acc

    _, acc = lax.fori_loop(0, qi // N_SUB + 1, sel_chunk,
                           (jnp.full((R, 1), REMOVED, F32), jnp.zeros((R, LANES), F32)))
    o_s = acc / acc[:, FLAG_LANE:FLAG_LANE + 1]

    kst = jnp.maximum(qi - WINDOW // QT, 0)
    w0 = pl.multiple_of(kst * QT, QT)
    ids = []
    for u in range(N_WSUB):
        d = qi - kst - u
        ids.append(jnp.where(d < 0, MASK_TILE, jnp.where(d == WINDOW // QT, WEDGE_TILE, d)))
    s = _dot_t(q2, kw_ref[0, 0, pl.ds(w0, WIN_KEYS), :]) + bias_rows(ids)
    pr = jnp.exp(s - jnp.max(s, axis=-1, keepdims=True)).astype(BF16)
    acc_w = _dot(pr, vw_ref[0, 0, pl.ds(w0, WIN_KEYS), :])
    o_w = acc_w / acc_w[:, FLAG_LANE:FLAG_LANE + 1]

    gt = gt_ref[0]
    outs = []
    for h in range(HG):
        rows = slice(h * QT, (h + 1) * QT)
        c = h * N_BRANCH
        outs.append(gt[:, c:c + 1] * o_c[rows] + gt[:, c + 1:c + 2] * o_s[rows]
                    + gt[:, c + 2:c + 3] * o_w[rows])
    o_ref[0] = jnp.concatenate(outs, axis=1).astype(BF16)


def _overlap_table(seq):
    ncw = seq // CMP_STRIDE
    front = ncw - QT // CMP_STRIDE
    nc = (seq - CMP_BLOCK) // CMP_STRIDE + 1
    ns = seq // SLC_BLOCK
    c_start = jnp.arange(nc) * CMP_STRIDE
    s_start = jnp.arange(ns) * SLC_BLOCK
    ov = (jnp.clip(jnp.minimum(c_start[:, None] + CMP_BLOCK, s_start[None, :] + SLC_BLOCK)
                   - jnp.maximum(c_start[:, None], s_start[None, :]), 0) / CMP_STRIDE).astype(F32)
    return jnp.pad(ov, ((front, ncw - nc), (0, LANES - ns)))


def _nsa(q, gates, ks, vs, kw, vw, kc, vc, tbl, cb):
    batch, _, seq, _ = q.shape
    G, HG = N_KV_GROUPS, HEADS_PER_GROUP
    ncw = seq // CMP_STRIDE
    crow = kc.shape[2]
    ov = _overlap_table(seq)
    kvspec = lambda lanes: pl.BlockSpec((1, 1, seq, lanes), lambda g, b, i: (b, g, 0, 0))
    cspec = pl.BlockSpec((1, 1, crow, LANES), lambda g, b, i: (b, g, 0, 0))
    return pl.pallas_call(
        functools.partial(_nsa_body, ncw=ncw),
        grid=(G, batch, seq // QT),
        in_specs=[
            pl.BlockSpec((1, HG, QT, LANES), lambda g, b, i: (b, g, i, 0)),
            pl.BlockSpec((1, QT, LANES), lambda g, b, i: (b, i, g)),
            kvspec(2 * LANES), kvspec(LANES), kvspec(LANES), kvspec(LANES),
            cspec, cspec,
            _const_spec(ov.shape),
            pl.BlockSpec((HG, N_TILES, QT, QT), lambda g, b, i: (g, 0, 0, 0)),
            pl.BlockSpec((HG, QT, ncw), lambda g, b, i: (g, 0, 0)),
        ],
        out_specs=pl.BlockSpec((1, QT, HG * LANES), lambda g, b, i: (b, i, g)),
        out_shape=jax.ShapeDtypeStruct((batch, seq, N_HEADS * LANES), BF16),
        compiler_params=_cparams(("arbitrary", "arbitrary", "arbitrary")),
        name="nsa",
    )(q, gates, ks, vs, kw, vw, kc, vc, ov, tbl, cb)


def _oproj_body(x_ref, a_ref, w_ref, o_ref):
    o_ref[...] = x_ref[...] + _dot(a_ref[...], w_ref[...])


def _o_proj(h, attn, w_o, *, tm=512):
    rows, d = h.shape
    k = attn.shape[1]
    w = jnp.pad(w_o.reshape(N_HEADS, HEAD_DIM, d), ((0, 0), (0, LANES - HEAD_DIM), (0, 0)))
    w = w.reshape(k, d).astype(BF16)
    return pl.pallas_call(
        _oproj_body,
        grid=(rows // tm,),
        in_specs=[pl.BlockSpec((tm, d), lambda i: (i, 0)),
                  pl.BlockSpec((tm, k), lambda i: (i, 0)),
                  _const_spec((k, d))],
        out_specs=pl.BlockSpec((tm, d), lambda i: (i, 0)),
        out_shape=jax.ShapeDtypeStruct((rows, d), F32),
        compiler_params=_cparams(("parallel",)),
        name="o_proj",
    )(h, attn, w)


def kernel(x, ffn1_norm, ffn1_w_in, ffn1_w_out, mix_norm, ffn2_norm, ffn2_w_in, ffn2_w_out, conv_w_in, conv_w, conv_w_out, attn_w_q, attn_q_norm, attn_w_o, kv_norm, w_kv, k_norm, cmp_pe_k, cmp_pe_v, cmp_w1_k, cmp_w2_k, cmp_w1_v, cmp_w2_v, rel_bias):
    batch, seq, d = x.shape
    depth = ffn1_norm.shape[0]
    n_a = conv_w_in.shape[0]
    assert seq % KCHUNK == 0 and seq // SLC_BLOCK <= LANES and seq >= WIN_KEYS
    h = x.reshape(batch * seq, d)
    kv = None
    tbl = cb = None
    for i in range(depth):
        if i == n_a:
            kc_raw, vc_raw, ks, vs, kw, vw = _kv_proj(h, kv_norm, w_kv, k_norm, batch=batch)
            kc, vc = _compress(kc_raw, vc_raw, cmp_pe_k, cmp_pe_v, cmp_w1_k, cmp_w2_k,
                               cmp_w1_v, cmp_w2_v, k_norm)
            kv = (ks, vs, kw, vw, kc, vc)
            tbl, cb = _bias_tables(rel_bias, seq)
        h = _ffn(h, ffn1_norm[i], ffn1_w_in[i], ffn1_w_out[i])
        if i < n_a:
            h = _conv_mixer(h, mix_norm[i], conv_w_in[i], conv_w[i], conv_w_out[i], batch=batch)
        else:
            j = i - n_a
            q, gates = _q_proj(h, mix_norm[i], attn_w_q[j], attn_q_norm[j], batch=batch)
            attn = _nsa(q, gates, *kv, tbl, cb)
            h = _o_proj(h, attn.reshape(batch * seq, -1), attn_w_o[j])
        h = _ffn(h, ffn2_norm[i], ffn2_w_in[i], ffn2_w_out[i])
    return h.reshape(batch, seq, d)
```

```python
import functools
import math

import jax
import jax.numpy as jnp
from jax import lax
from jax.experimental import pallas as pl
from jax.experimental.pallas import tpu as pltpu

F32 = jnp.float32
BF16 = jnp.bfloat16

N_HEADS = 16
N_KV_GROUPS = 4
HEADS_PER_GROUP = N_HEADS // N_KV_GROUPS
HEAD_DIM = 64
N_BRANCH = 3
CMP_BLOCK = 32
CMP_STRIDE = 16
SLC_BLOCK = 64
N_SELECT = 16
WINDOW = 512
N_BUCKETS = 32
MAX_EXACT = N_BUCKETS // 2
MAX_DISTANCE = 4096
EPS = 1e-6
NEG = -1e30
BIG = 1e30
REMOVED = -3e38
LOG2E = math.log2(math.e)

LANES = 128
SUBLANES = 8
VMEM_LIMIT = 56 * 1024 * 1024

QT = 128
KCHUNK = 512
N_SUB = KCHUNK // QT
STRIP = 32
WIN_KEYS = WINDOW + QT
N_WSUB = WIN_KEYS // QT
FAR_TILE = int(math.ceil((MAX_EXACT * (MAX_DISTANCE / MAX_EXACT) ** ((N_BUCKETS - MAX_EXACT - 1) / (N_BUCKETS - MAX_EXACT))
                          + QT) / QT))
MASK_TILE = FAR_TILE + 1
N_TILES = FAR_TILE + 2
WIN_TILES = WINDOW // QT + 1
WIN_MASK_TILE = WIN_TILES
FLAG_LANE = HEAD_DIM


def _cparams(sem):
    return pltpu.CompilerParams(dimension_semantics=sem, vmem_limit_bytes=VMEM_LIMIT)


def _const_spec(shape):
    nd = len(shape)
    return pl.BlockSpec(shape, lambda *_: (0,) * nd, pipeline_mode=pl.Buffered(1))


def _rms(x, g, n):
    ms = jnp.sum(x * x, axis=-1, keepdims=True) * (1.0 / n)
    return x * lax.rsqrt(ms + EPS) * g


def _dot(a, b):
    return jnp.dot(a, b, preferred_element_type=F32)


def _dot_t(a, b):
    return lax.dot_general(a, b, (((1,), (1,)), ((), ())), preferred_element_type=F32)


def _ffn_body(x_ref, g_ref, wg_ref, wu_ref, wo_ref, o_ref, acc_ref, *, n_chunks, d_model):
    x = x_ref[...]
    xn = _rms(x, g_ref[...], d_model).astype(BF16)
    acc_ref[...] = jnp.zeros_like(acc_ref)

    def chunk(c, carry):
        gate = _dot(xn, wg_ref[c])
        up = _dot(xn, wu_ref[c])
        act = (gate * (1.0 / (1.0 + jnp.exp(-gate))) * up).astype(BF16)
        acc_ref[...] += _dot(act, wo_ref[c])
        return carry

    lax.fori_loop(0, n_chunks, chunk, 0)
    o_ref[...] = x + 0.5 * acc_ref[...]


def _ffn(h, norm_g, w_in, w_out, *, tm=512, ck=256):
    rows, d = h.shape
    dff = w_out.shape[0]
    nch = dff // ck
    wg = w_in[:, :dff].reshape(d, nch, ck).transpose(1, 0, 2).astype(BF16)
    wu = w_in[:, dff:].reshape(d, nch, ck).transpose(1, 0, 2).astype(BF16)
    wo = w_out.reshape(nch, ck, d).astype(BF16)
    return pl.pallas_call(
        functools.partial(_ffn_body, n_chunks=nch, d_model=d),
        grid=(rows // tm,),
        in_specs=[
            pl.BlockSpec((tm, d), lambda i: (i, 0)),
            _const_spec((1, d)),
            _const_spec((nch, d, ck)),
            _const_spec((nch, d, ck)),
            _const_spec((nch, ck, d)),
        ],
        out_specs=pl.BlockSpec((tm, d), lambda i: (i, 0)),
        out_shape=jax.ShapeDtypeStruct((rows, d), F32),
        scratch_shapes=[pltpu.VMEM((tm, d), F32)],
        compiler_params=_cparams(("parallel",)),
        name="ffn",
    )(h, norm_g.reshape(1, d), wg, wu, wo)


def _conv_body(x_ref, g_ref, win_ref, cw_ref, wout_ref, o_ref, ubuf_ref, *, tm, d_model):
    i = pl.program_id(1)
    x = x_ref[0]
    xn = _rms(x, g_ref[...], d_model).astype(BF16)
    b_gate = _dot(xn, win_ref[0])
    c_gate = _dot(xn, win_ref[1])
    v = _dot(xn, win_ref[2])
    u = c_gate * v

    @pl.when(i == 0)
    def _():
        ubuf_ref[0:SUBLANES, :] = jnp.zeros((SUBLANES, d_model), F32)

    ubuf_ref[SUBLANES:SUBLANES + tm, :] = u
    u1 = ubuf_ref[SUBLANES - 1:SUBLANES - 1 + tm, :]
    u2 = ubuf_ref[SUBLANES - 2:SUBLANES - 2 + tm, :]
    y = cw_ref[2:3, :] * u + cw_ref[1:2, :] * u1 + cw_ref[0:1, :] * u2
    ubuf_ref[0:SUBLANES, :] = u[tm - SUBLANES:, :]
    o_ref[0] = x + _dot((b_gate * y).astype(BF16), wout_ref[...])


def _conv_mixer(h, norm_g, w_in, conv_w, w_out, *, batch, tm=256):
    rows, d = h.shape
    seq = rows // batch
    win = w_in.reshape(d, 3, d).transpose(1, 0, 2).astype(BF16)
    out = pl.pallas_call(
        functools.partial(_conv_body, tm=tm, d_model=d),
        grid=(batch, seq // tm),
        in_specs=[
            pl.BlockSpec((1, tm, d), lambda b, i: (b, i, 0)),
            _const_spec((1, d)),
            _const_spec((3, d, d)),
            _const_spec((conv_w.shape[0], d)),
            _const_spec((d, d)),
        ],
        out_specs=pl.BlockSpec((1, tm, d), lambda b, i: (b, i, 0)),
        out_shape=jax.ShapeDtypeStruct((batch, seq, d), F32),
        scratch_shapes=[pltpu.VMEM((tm + SUBLANES, d), F32)],
        compiler_params=_cparams(("arbitrary", "arbitrary")),
        name="conv_mixer",
    )(h.reshape(batch, seq, d), norm_g.reshape(1, d), win, conv_w, w_out.astype(BF16))
    return out.reshape(rows, d)


def _pad_head_cols(w, n_slices):
    d = w.shape[0]
    w = w.reshape(d, n_slices, HEAD_DIM)
    return jnp.pad(w, ((0, 0), (0, 0), (0, LANES - HEAD_DIM))).reshape(d, n_slices * LANES)


def _pad_lanes(v):
    return jnp.pad(v, [(0, 0)] * (v.ndim - 1) + [(0, LANES - HEAD_DIM)])


def _kvproj_body(x_ref, g_ref, w_ref, wvt_ref, kn_ref, kc_ref, vc_ref, ks_ref, vst_ref, kw_ref, vw_ref, *,
                 tm, d_model):
    i = pl.program_id(1)
    xn = _rms(x_ref[0], g_ref[...], d_model).astype(BF16)
    lane = lax.broadcasted_iota(jnp.int32, (tm, LANES), 1)
    ones_col = jnp.where(lane == FLAG_LANE, 1.0, 0.0)
    ones_row = jnp.where(lax.broadcasted_iota(jnp.int32, (LANES, tm), 0) == FLAG_LANE, 1.0, 0.0)
    blk = (i * tm + lax.broadcasted_iota(jnp.int32, (tm, LANES), 0)) // SLC_BLOCK
    onehot = jnp.where(lane == blk, 1.0, 0.0).astype(BF16)
    G = N_KV_GROUPS
    for g in range(G):
        def col(br, kv):
            return _dot(xn, w_ref[(br * 2 + kv) * G + g])
        kc_ref[0, g] = col(0, 0)
        vc_ref[0, g] = col(0, 1)
        ks = _rms(col(1, 0), kn_ref[1:2, :], HEAD_DIM).astype(BF16)
        ks_ref[0, g] = jnp.concatenate([ks, onehot], axis=1)
        vst_ref[0, g] = (_dot_t(wvt_ref[g], xn) + ones_row).astype(BF16)
        kw_ref[0, g] = _rms(col(2, 0), kn_ref[2:3, :], HEAD_DIM).astype(BF16)
        vw_ref[0, g] = (col(2, 1) + ones_col).astype(BF16)


def _kv_proj(h, kv_norm, w_kv, k_norm, *, batch, tm=256):
    rows, d = h.shape
    seq = rows // batch
    G = N_KV_GROUPS
    n_sl = N_BRANCH * 2 * G
    w = _pad_head_cols(w_kv, n_sl).reshape(d, n_sl, LANES).transpose(1, 0, 2).astype(BF16)
    v_sel = (1 * 2 + 1) * G
    wvt = w[v_sel:v_sel + G].transpose(0, 2, 1)
    kn = _pad_lanes(k_norm)
    sds = lambda lanes, dt: jax.ShapeDtypeStruct((batch, G, seq, lanes), dt)
    ospec = lambda lanes: pl.BlockSpec((1, G, tm, lanes), lambda b, i: (b, 0, i, 0))
    kc_raw, vc_raw, ks, vst, kw, vw = pl.pallas_call(
        functools.partial(_kvproj_body, tm=tm, d_model=d),
        grid=(batch, seq // tm),
        in_specs=[
            pl.BlockSpec((1, tm, d), lambda b, i: (b, i, 0)),
            _const_spec((1, d)),
            _const_spec((n_sl, d, LANES)),
            _const_spec((G, LANES, d)),
            _const_spec((N_BRANCH, LANES)),
        ],
        out_specs=[ospec(LANES), ospec(LANES), ospec(2 * LANES),
                   pl.BlockSpec((1, G, LANES, tm), lambda b, i: (b, 0, 0, i)), ospec(LANES), ospec(LANES)],
        out_shape=[sds(LANES, F32), sds(LANES, F32), sds(2 * LANES, BF16),
                   jax.ShapeDtypeStruct((batch, G, LANES, seq), BF16), sds(LANES, BF16), sds(LANES, BF16)],
        compiler_params=_cparams(("parallel", "parallel")),
        name="kv_proj",
    )(h.reshape(batch, seq, d), kv_norm.reshape(1, d), w, wvt, kn)
    vst = vst.reshape(batch, G, LANES, seq // KCHUNK, KCHUNK).transpose(0, 1, 3, 2, 4)
    return kc_raw, vc_raw, ks, vst, kw, vw


def _gelu_tanh(x):
    return 0.5 * x * (1.0 + jnp.tanh(math.sqrt(2.0 / math.pi) * (x + 0.044715 * (x * x * x))))


def _compress_body(kr_ref, vr_ref, pek_ref, pev_ref, w1k_ref, w2k_ref, w1v_ref, w2v_ref, kn_ref,
                   kc_ref, vc_ref, *, ncw, front):
    row = lax.broadcasted_iota(jnp.int32, (ncw, LANES), 0)
    row_lane = lax.broadcasted_iota(jnp.int32, (ncw, LANES), 1)
    lane = lax.broadcasted_iota(jnp.int32, (front, LANES), 1)

    def mlp(r_ref, pe_ref, w1_ref, w2_ref):
        r = r_ref[0, 0]
        a = _dot((r + pe_ref[0:1, :]).astype(BF16), w1_ref[0])
        b = _dot((r + pe_ref[1:2, :]).astype(BF16), w1_ref[1])
        hid = a + pltpu.roll(b, ncw - 1, axis=0)
        out = _dot(_gelu_tanh(hid).astype(BF16), w2_ref[...])
        return jnp.where(row < ncw - 1, out, 0.0)

    kc = _rms(mlp(kr_ref, pek_ref, w1k_ref, w2k_ref), kn_ref[0:1, :], HEAD_DIM)
    vc = mlp(vr_ref, pev_ref, w1v_ref, w2v_ref)
    kc_ref[0, 0, 0:front, :] = jnp.where(lane == FLAG_LANE, 1.0, 0.0)
    vc_ref[0, 0, 0:front, :] = jnp.zeros((front, LANES), F32)
    kc_ref[0, 0, front:front + ncw, :] = kc
    vc_ref[0, 0, front:front + ncw, :] = vc + jnp.where(row_lane == FLAG_LANE, 1.0, 0.0)


def _compress(kc_raw, vc_raw, pe_k, pe_v, w1_k, w2_k, w1_v, w2_v, k_norm):
    batch, G, seq, _ = kc_raw.shape
    ncw = seq // CMP_STRIDE
    front = ncw - QT // CMP_STRIDE
    tok = CMP_STRIDE * LANES
    hid = w1_k.shape[1]

    def prep_w1(w1):
        w = w1.reshape(CMP_BLOCK, HEAD_DIM, hid)
        w = jnp.pad(w, ((0, 0), (0, LANES - HEAD_DIM), (0, 0)))
        return w.reshape(2, tok, hid).astype(BF16)

    def prep_pe(pe):
        return _pad_lanes(pe).reshape(2, tok)

    def prep_w2(w2):
        return _pad_lanes(w2).astype(BF16)

    rspec = pl.BlockSpec((1, 1, ncw, tok), lambda b, g: (b, g, 0, 0))
    ospec = pl.BlockSpec((1, 1, front + ncw, LANES), lambda b, g: (b, g, 0, 0))
    osds = jax.ShapeDtypeStruct((batch, G, front + ncw, LANES), F32)
    return pl.pallas_call(
        functools.partial(_compress_body, ncw=ncw, front=front),
        grid=(batch, G),
        in_specs=[rspec, rspec, _const_spec((2, tok)), _const_spec((2, tok)),
                  _const_spec((2, tok, hid)), _const_spec((hid, LANES)),
                  _const_spec((2, tok, hid)), _const_spec((hid, LANES)),
                  _const_spec((N_BRANCH, LANES))],
        out_specs=[ospec, ospec],
        out_shape=[osds, osds],
        compiler_params=_cparams(("parallel", "parallel")),
        name="compress",
    )(kc_raw.reshape(batch, G, ncw, tok), vc_raw.reshape(batch, G, ncw, tok),
      prep_pe(pe_k), prep_pe(pe_v), prep_w1(w1_k), prep_w2(w2_k), prep_w1(w1_v), prep_w2(w2_v),
      _pad_lanes(k_norm))


def _qproj_body(x_ref, g_ref, wq_ref, wg_ref, qn_ref, q_ref, gt_ref, *, tm, d_model):
    xn = _rms(x_ref[0], g_ref[...], d_model).astype(BF16)
    lane = lax.broadcasted_iota(jnp.int32, (tm, LANES), 1)
    scale = HEAD_DIM ** -0.5 * LOG2E
    for h in range(N_HEADS):
        q = _rms(_dot(xn, wq_ref[h]), qn_ref[...], HEAD_DIM) * scale
        q_ref[0, h] = jnp.where(lane == FLAG_LANE, NEG, q).astype(BF16)
    for g in range(N_KV_GROUPS):
        z = _dot(xn, wg_ref[g])
        gt_ref[0, :, g * LANES:(g + 1) * LANES] = 1.0 / (1.0 + jnp.exp(-z))


def _q_proj(h, norm_g, w_q, q_norm, *, batch, tm=256):
    rows, d = h.shape
    seq = rows // batch
    G, HG = N_KV_GROUPS, HEADS_PER_GROUP
    nq = N_HEADS * HEAD_DIM
    wq = _pad_head_cols(w_q[:, :nq], N_HEADS).reshape(d, N_HEADS, LANES).transpose(1, 0, 2).astype(BF16)
    ngc = HG * N_BRANCH
    wg = w_q[:, nq:].reshape(d, G, ngc)
    wg = jnp.pad(wg, ((0, 0), (0, 0), (0, LANES - ngc))).transpose(1, 0, 2).astype(BF16)
    return pl.pallas_call(
        functools.partial(_qproj_body, tm=tm, d_model=d),
        grid=(batch, seq // tm),
        in_specs=[
            pl.BlockSpec((1, tm, d), lambda b, i: (b, i, 0)),
            _const_spec((1, d)),
            _const_spec((N_HEADS, d, LANES)),
            _const_spec((G, d, LANES)),
            _const_spec((1, LANES)),
        ],
        out_specs=[pl.BlockSpec((1, N_HEADS, tm, LANES), lambda b, i: (b, 0, i, 0)),
                   pl.BlockSpec((1, tm, G * LANES), lambda b, i: (b, i, 0))],
        out_shape=[jax.ShapeDtypeStruct((batch, N_HEADS, seq, LANES), BF16),
                   jax.ShapeDtypeStruct((batch, seq, G * LANES), F32)],
        compiler_params=_cparams(("parallel", "parallel")),
        name="q_proj",
    )(h.reshape(batch, seq, d), norm_g.reshape(1, d), wq, wg, _pad_lanes(q_norm.reshape(1, HEAD_DIM)))


def _tables_body(thr_ref, rb_ref, tblt_ref, tblw_ref, cb_ref, *, ncw, c_off):
    h = pl.program_id(0)

    def bias_of(rel):
        v = jnp.full(rel.shape, rb_ref[0, h], F32)
        for k in range(1, N_BUCKETS):
            v = jnp.where(rel >= thr_ref[k], rb_ref[k, h], v)
        return v * LOG2E

    row = lax.broadcasted_iota(jnp.int32, (QT, QT), 0)
    col = lax.broadcasted_iota(jnp.int32, (QT, QT), 1)

    def tile_t(d, carry):
        rel = (col - row) + d * QT
        tblt_ref[0, d] = jnp.where(rel < 0, NEG, bias_of(rel))
        return carry

    lax.fori_loop(0, FAR_TILE + 1, tile_t, 0)
    tblt_ref[0, MASK_TILE] = jnp.full((QT, QT), NEG, F32)

    def tile_w(d, carry):
        rel = (row - col) + d * QT
        tblw_ref[0, d] = jnp.where((rel < 0) | (rel >= WINDOW), NEG, bias_of(rel))
        return carry

    lax.fori_loop(0, WIN_TILES, tile_w, 0)
    tblw_ref[0, WIN_MASK_TILE] = jnp.full((QT, QT), NEG, F32)
    relc = (lax.broadcasted_iota(jnp.int32, (QT, ncw), 0)
            - CMP_STRIDE * lax.broadcasted_iota(jnp.int32, (QT, ncw), 1) + c_off)
    cb_ref[0] = jnp.where(relc < 0, NEG, bias_of(relc))


def _bucket_thresholds(seq):
    n = jnp.arange(seq)
    nf = jnp.maximum(n, 1).astype(jnp.float32)
    large = MAX_EXACT + (jnp.log(nf / MAX_EXACT) / math.log(MAX_DISTANCE / MAX_EXACT)
                         * (N_BUCKETS - MAX_EXACT)).astype(jnp.int32)
    large = jnp.minimum(large, N_BUCKETS - 1)
    bucket = jnp.where(n < MAX_EXACT, n, large)
    return jnp.sum(bucket[None, :] < jnp.arange(N_BUCKETS)[:, None], axis=1).astype(jnp.int32)


def _bias_tables(rel_bias, seq):
    ncw = seq // CMP_STRIDE
    front = ncw - QT // CMP_STRIDE
    c_off = CMP_STRIDE * front - (CMP_BLOCK - 1)
    smem = pl.BlockSpec(memory_space=pltpu.SMEM)
    return pl.pallas_call(
        functools.partial(_tables_body, ncw=ncw, c_off=c_off),
        grid=(N_HEADS,),
        in_specs=[smem, smem],
        out_specs=[pl.BlockSpec((1, N_TILES, QT, QT), lambda h: (h, 0, 0, 0)),
                   pl.BlockSpec((1, WIN_TILES + 1, QT, QT), lambda h: (h, 0, 0, 0)),
                   pl.BlockSpec((1, QT, ncw), lambda h: (h, 0, 0))],
        out_shape=[jax.ShapeDtypeStruct((N_HEADS, N_TILES, QT, QT), F32),
                   jax.ShapeDtypeStruct((N_HEADS, WIN_TILES + 1, QT, QT), F32),
                   jax.ShapeDtypeStruct((N_HEADS, QT, ncw), F32)],
        compiler_params=_cparams(("parallel",)),
        name="bias_tables",
    )(_bucket_thresholds(seq), rel_bias)


def _nsa_body(q_ref, gt_ref, ks_ref, vst_ref, kw_ref, vw_ref, kc_ref, vc_ref, ov_ref, tblt_ref, tblw_ref, cb_ref,
              o_ref, sa_ref, sb_ref, pa_ref, pb_ref, lhs_ref, m_ref, ala_ref, alb_ref, acct_ref, *, ncw):
    HG = HEADS_PER_GROUP
    R = HG * QT
    qi = pl.program_id(2)
    q2 = q_ref[0].reshape(R, LANES)

    def softmax_strips(s_ref, p_ref, width, bias_fn):
        for st in range(R // STRIP):
            rows = slice(st * STRIP, (st + 1) * STRIP)
            s = s_ref[rows, :width] + bias_fn((st * STRIP) // QT, (st * STRIP) % QT)
            p_ref[rows, :width] = jnp.exp2(s - jnp.max(s, axis=-1, keepdims=True)).astype(BF16)

    c0 = pl.multiple_of(qi * (QT // CMP_STRIDE), SUBLANES)
    kc = kc_ref[0, 0, pl.ds(c0, ncw), :].astype(BF16)
    vo = jnp.concatenate([vc_ref[0, 0, pl.ds(c0, ncw), :], ov_ref[pl.ds(c0, ncw), :]], axis=1).astype(BF16)
    sa_ref[:, :ncw] = _dot_t(q2, kc)
    softmax_strips(sa_ref, pa_ref, ncw, lambda h, r: cb_ref[h, r:r + STRIP, :])
    r = _dot(pa_ref[:, :ncw], vo)
    t_row = qi * QT + (lax.broadcasted_iota(jnp.int32, (R, 1), 0) & (QT - 1))
    r = jnp.where(t_row >= CMP_BLOCK - 1, r / r[:, FLAG_LANE:FLAG_LANE + 1], 0.0)
    o_c = r[:, :LANES]
    imp = r[0:QT, LANES:]
    for h in range(1, HG):
        imp = imp + r[h * QT:(h + 1) * QT, LANES:]

    kst = jnp.maximum(qi - WINDOW // QT, 0)
    w0 = pl.multiple_of(kst * QT, QT)
    ids = []
    for u in range(N_WSUB):
        d = qi - kst - u
        ids.append(jnp.where(d < 0, WIN_MASK_TILE, d))
    sb_ref[:, :WIN_KEYS] = _dot_t(q2, kw_ref[0, 0, pl.ds(w0, WIN_KEYS), :])
    softmax_strips(sb_ref, pb_ref, WIN_KEYS,
                   lambda h, r: jnp.concatenate([tblw_ref[h, t, r:r + STRIP, :] for t in ids], axis=1))
    acc_w = _dot(pb_ref[:, :WIN_KEYS], vw_ref[0, 0, pl.ds(w0, WIN_KEYS), :])
    o_w = acc_w / acc_w[:, FLAG_LANE:FLAG_LANE + 1]

    blk = lax.broadcasted_iota(jnp.int32, (LANES, QT), 0)
    blk_t = (qi * QT + lax.broadcasted_iota(jnp.int32, (LANES, QT), 1)) // SLC_BLOCK
    forced = (blk == 0) | (blk == blk_t) | (blk == blk_t - 1)
    score = jnp.where(blk <= blk_t, jnp.where(forced, BIG, imp.T), NEG)
    sel = jnp.zeros((LANES, QT), F32)
    for _ in range(N_SELECT):
        top = jnp.max(score, axis=0, keepdims=True)
        first = jnp.min(jnp.where(score == top, blk, LANES), axis=0, keepdims=True)
        hit = blk == first
        sel = jnp.where(hit, 1.0, sel)
        score = jnp.where(hit, REMOVED, score)
    sel_neg = ((sel - 1.0) * BIG).T.astype(BF16)
    lhs_ref[...] = jnp.concatenate([q2, jnp.concatenate([sel_neg] * HG, axis=0)], axis=1)

    m_ref[...] = jnp.full((1, R), REMOVED, F32)
    acct_ref[...] = jnp.zeros((LANES, R), F32)
    n_chunks = qi // N_SUB + 1

    def scores(s_ref, kj):
        k0 = pl.multiple_of(kj * KCHUNK, KCHUNK)
        s_ref[:, :R] = _dot_t(ks_ref[0, 0, pl.ds(k0, KCHUNK), :], lhs_ref[...])

    def accumulate(s_ref, p_ref, al_ref, kj):
        for h in range(HG):
            cols = slice(h * QT, (h + 1) * QT)
            mx = m_ref[:, cols]
            for u in range(N_SUB):
                keys = slice(u * QT, (u + 1) * QT)
                d = qi - kj * N_SUB - u
                tile = jnp.where(d < 0, MASK_TILE, jnp.minimum(d, FAR_TILE))
                sb = s_ref[keys, cols] + tblt_ref[h, tile]
                s_ref[keys, cols] = sb
                mx = jnp.maximum(mx, jnp.max(sb, axis=0, keepdims=True))
            al_ref[:, cols] = jnp.exp2(m_ref[:, cols] - mx)
            m_ref[:, cols] = mx
            p_ref[:, cols] = jnp.exp2(s_ref[:, cols] - mx).astype(BF16)
        acct_ref[...] = al_ref[...] * acct_ref[...] + _dot(vst_ref[0, 0, kj], p_ref[:, :R])

    scores(sa_ref, 0)

    def chunk_pair(j, carry):
        a = 2 * j
        scores(sb_ref, a + 1)
        accumulate(sa_ref, pa_ref, ala_ref, a)
        scores(sa_ref, jnp.minimum(a + 2, n_chunks - 1))
        accumulate(sb_ref, pb_ref, alb_ref, a + 1)
        return carry

    lax.fori_loop(0, n_chunks // 2, chunk_pair, 0)

    @pl.when(n_chunks % 2 == 1)
    def _():
        accumulate(sa_ref, pa_ref, ala_ref, n_chunks - 1)

    acct = acct_ref[...]
    o_st = acct / acct[FLAG_LANE:FLAG_LANE + 1, :]

    gt = gt_ref[0]
    outs = []
    for h in range(HG):
        rows = slice(h * QT, (h + 1) * QT)
        c = h * N_BRANCH
        outs.append(gt[:, c:c + 1] * o_c[rows] + gt[:, c + 1:c + 2] * o_st[:, rows].T
                    + gt[:, c + 2:c + 3] * o_w[rows])
    o_ref[0] = jnp.concatenate(outs, axis=1).astype(BF16)


def _overlap_table(seq):
    ncw = seq // CMP_STRIDE
    front = ncw - QT // CMP_STRIDE
    nc = (seq - CMP_BLOCK) // CMP_STRIDE + 1
    ns = seq // SLC_BLOCK
    c_start = jnp.arange(nc) * CMP_STRIDE
    s_start = jnp.arange(ns) * SLC_BLOCK
    ov = (jnp.clip(jnp.minimum(c_start[:, None] + CMP_BLOCK, s_start[None, :] + SLC_BLOCK)
                   - jnp.maximum(c_start[:, None], s_start[None, :]), 0) / CMP_STRIDE).astype(F32)
    return jnp.pad(ov, ((front, ncw - nc), (0, LANES - ns)))


def _nsa(q, gates, ks, vst, kw, vw, kc, vc, tblt, tblw, cb):
    batch, _, seq, _ = q.shape
    G, HG = N_KV_GROUPS, HEADS_PER_GROUP
    ncw = seq // CMP_STRIDE
    crow = kc.shape[2]
    ov = _overlap_table(seq)
    rows = HG * QT
    assert rows == KCHUNK
    width = max(ncw, WIN_KEYS, KCHUNK)
    kvspec = lambda lanes: pl.BlockSpec((1, 1, seq, lanes), lambda g, b, i: (b, g, 0, 0))
    cspec = pl.BlockSpec((1, 1, crow, LANES), lambda g, b, i: (b, g, 0, 0))
    per_group = lambda n, w: pl.BlockSpec((HG, n, QT, w), lambda g, b, i: (g, 0, 0, 0),
                                          pipeline_mode=pl.Buffered(1))
    return pl.pallas_call(
        functools.partial(_nsa_body, ncw=ncw),
        grid=(G, batch, seq // QT),
        in_specs=[
            pl.BlockSpec((1, HG, QT, LANES), lambda g, b, i: (b, g, i, 0)),
            pl.BlockSpec((1, QT, LANES), lambda g, b, i: (b, i, g)),
            kvspec(2 * LANES),
            pl.BlockSpec((1, 1, seq // KCHUNK, LANES, KCHUNK), lambda g, b, i: (b, g, 0, 0, 0)),
            kvspec(LANES), kvspec(LANES),
            cspec, cspec,
            _const_spec(ov.shape),
            per_group(N_TILES, QT), per_group(WIN_TILES + 1, QT),
            pl.BlockSpec((HG, QT, ncw), lambda g, b, i: (g, 0, 0), pipeline_mode=pl.Buffered(1)),
        ],
        out_specs=pl.BlockSpec((1, QT, HG * LANES), lambda g, b, i: (b, i, g)),
        out_shape=jax.ShapeDtypeStruct((batch, seq, N_HEADS * LANES), BF16),
        scratch_shapes=[pltpu.VMEM((rows, width), F32), pltpu.VMEM((rows, width), F32),
                        pltpu.VMEM((rows, width), BF16), pltpu.VMEM((rows, width), BF16),
                        pltpu.VMEM((rows, 2 * LANES), BF16),
                        pltpu.VMEM((1, rows), F32), pltpu.VMEM((1, rows), F32), pltpu.VMEM((1, rows), F32),
                        pltpu.VMEM((LANES, rows), F32)],
        compiler_params=_cparams(("arbitrary", "arbitrary", "arbitrary")),
        name="nsa",
    )(q, gates, ks, vst, kw, vw, kc, vc, ov, tblt, tblw, cb)


def _oproj_body(x_ref, a_ref, w_ref, o_ref):
    o_ref[...] = x_ref[...] + _dot(a_ref[...], w_ref[...])


def _o_proj(h, attn, w_o, *, tm=512):
    rows, d = h.shape
    k = attn.shape[1]
    w = jnp.pad(w_o.reshape(N_HEADS, HEAD_DIM, d), ((0, 0), (0, LANES - HEAD_DIM), (0, 0)))
    w = w.reshape(k, d).astype(BF16)
    return pl.pallas_call(
        _oproj_body,
        grid=(rows // tm,),
        in_specs=[pl.BlockSpec((tm, d), lambda i: (i, 0)),
                  pl.BlockSpec((tm, k), lambda i: (i, 0)),
                  _const_spec((k, d))],
        out_specs=pl.BlockSpec((tm, d), lambda i: (i, 0)),
        out_shape=jax.ShapeDtypeStruct((rows, d), F32),
        compiler_params=_cparams(("parallel",)),
        name="o_proj",
    )(h, attn, w)


def kernel(x, ffn1_norm, ffn1_w_in, ffn1_w_out, mix_norm, ffn2_norm, ffn2_w_in, ffn2_w_out, conv_w_in, conv_w, conv_w_out, attn_w_q, attn_q_norm, attn_w_o, kv_norm, w_kv, k_norm, cmp_pe_k, cmp_pe_v, cmp_w1_k, cmp_w2_k, cmp_w1_v, cmp_w2_v, rel_bias):
    batch, seq, d = x.shape
    depth = ffn1_norm.shape[0]
    n_a = conv_w_in.shape[0]
    assert seq % KCHUNK == 0 and seq // SLC_BLOCK <= LANES and seq >= WIN_KEYS
    h = x.reshape(batch * seq, d)
    kv = tables = None
    for i in range(depth):
        if i == n_a:
            kc_raw, vc_raw, ks, vst, kw, vw = _kv_proj(h, kv_norm, w_kv, k_norm, batch=batch)
            kc, vc = _compress(kc_raw, vc_raw, cmp_pe_k, cmp_pe_v, cmp_w1_k, cmp_w2_k,
                               cmp_w1_v, cmp_w2_v, k_norm)
            kv = (ks, vst, kw, vw, kc, vc)
            tables = _bias_tables(rel_bias, seq)
        h = _ffn(h, ffn1_norm[i], ffn1_w_in[i], ffn1_w_out[i])
        if i < n_a:
            h = _conv_mixer(h, mix_norm[i], conv_w_in[i], conv_w[i], conv_w_out[i], batch=batch)
        else:
            j = i - n_a
            q, gates = _q_proj(h, mix_norm[i], attn_w_q[j], attn_q_norm[j], batch=batch)
            attn = _nsa(q, gates, *kv, *tables)
            h = _o_proj(h, attn.reshape(batch * seq, -1), attn_w_o[j])
        h = _ffn(h, ffn2_norm[i], ffn2_w_in[i], ffn2_w_out[i])
    return h.reshape(batch, seq, d)
```

```python
import functools
import math

import jax
import jax.numpy as jnp
from jax import lax
from jax.experimental import pallas as pl
from jax.experimental.pallas import tpu as pltpu

F32 = jnp.float32
BF16 = jnp.bfloat16

N_HEADS = 16
N_KV_GROUPS = 4
HEADS_PER_GROUP = N_HEADS // N_KV_GROUPS
HEAD_DIM = 64
N_BRANCH = 3
CMP_BLOCK = 32
CMP_STRIDE = 16
SLC_BLOCK = 64
N_SELECT = 16
N_FORCED = 3
WINDOW = 512
N_BUCKETS = 32
MAX_EXACT = N_BUCKETS // 2
MAX_DISTANCE = 4096
EPS = 1e-6
NEG = -1e30
BIG = 1e30
REMOVED = -3e38
LOG2E = math.log2(math.e)

LANES = 128
SUBLANES = 8
VMEM_LIMIT = 56 * 1024 * 1024

QT = 128
KCHUNK = 512
N_SUB = KCHUNK // QT
STRIP = 32
WIN_KEYS = WINDOW + QT
N_WSUB = WIN_KEYS // QT
FAR_TILE = int(math.ceil((MAX_EXACT * (MAX_DISTANCE / MAX_EXACT) ** ((N_BUCKETS - MAX_EXACT - 1) / (N_BUCKETS - MAX_EXACT))
                          + QT) / QT))
MASK_TILE = FAR_TILE + 1
N_TILES = FAR_TILE + 2
WIN_TILES = WINDOW // QT + 1
WIN_MASK_TILE = WIN_TILES
FLAG_LANE = HEAD_DIM


def _cparams(sem):
    return pltpu.CompilerParams(dimension_semantics=sem, vmem_limit_bytes=VMEM_LIMIT)


def _const_spec(shape):
    nd = len(shape)
    return pl.BlockSpec(shape, lambda *_: (0,) * nd, pipeline_mode=pl.Buffered(1))


def _rms(x, g, n):
    ms = jnp.sum(x * x, axis=-1, keepdims=True) * (1.0 / n)
    return x * lax.rsqrt(ms + EPS) * g


def _dot(a, b):
    return jnp.dot(a, b, preferred_element_type=F32)


def _dot_t(a, b):
    return lax.dot_general(a, b, (((1,), (1,)), ((), ())), preferred_element_type=F32)


def _ffn_body(x_ref, g_ref, wg_ref, wu_ref, wo_ref, o_ref, acc_ref, *, n_chunks, d_model):
    x = x_ref[...]
    xn = _rms(x, g_ref[...], d_model).astype(BF16)
    acc_ref[...] = jnp.zeros_like(acc_ref)

    def chunk(c, carry):
        gate = _dot(xn, wg_ref[c])
        up = _dot(xn, wu_ref[c])
        act = (gate * (1.0 / (1.0 + jnp.exp(-gate))) * up).astype(BF16)
        acc_ref[...] += _dot(act, wo_ref[c])
        return carry

    lax.fori_loop(0, n_chunks, chunk, 0)
    o_ref[...] = x + 0.5 * acc_ref[...]


def _ffn(h, norm_g, w_in, w_out, *, tm=1024, ck=256):
    rows, d = h.shape
    dff = w_out.shape[0]
    nch = dff // ck
    wg = w_in[:, :dff].reshape(d, nch, ck).transpose(1, 0, 2).astype(BF16)
    wu = w_in[:, dff:].reshape(d, nch, ck).transpose(1, 0, 2).astype(BF16)
    wo = w_out.reshape(nch, ck, d).astype(BF16)
    return pl.pallas_call(
        functools.partial(_ffn_body, n_chunks=nch, d_model=d),
        grid=(rows // tm,),
        in_specs=[
            pl.BlockSpec((tm, d), lambda i: (i, 0)),
            _const_spec((1, d)),
            _const_spec((nch, d, ck)),
            _const_spec((nch, d, ck)),
            _const_spec((nch, ck, d)),
        ],
        out_specs=pl.BlockSpec((tm, d), lambda i: (i, 0)),
        out_shape=jax.ShapeDtypeStruct((rows, d), F32),
        scratch_shapes=[pltpu.VMEM((tm, d), F32)],
        compiler_params=_cparams(("parallel",)),
        name="ffn",
    )(h, norm_g.reshape(1, d), wg, wu, wo)


def _conv_body(x_ref, g_ref, win_ref, cw_ref, wout_ref, o_ref, ubuf_ref, *, tm, d_model):
    i = pl.program_id(1)
    x = x_ref[0]
    xn = _rms(x, g_ref[...], d_model).astype(BF16)
    b_gate = _dot(xn, win_ref[0])
    c_gate = _dot(xn, win_ref[1])
    v = _dot(xn, win_ref[2])
    u = c_gate * v

    @pl.when(i == 0)
    def _():
        ubuf_ref[0:SUBLANES, :] = jnp.zeros((SUBLANES, d_model), F32)

    ubuf_ref[SUBLANES:SUBLANES + tm, :] = u
    u1 = ubuf_ref[SUBLANES - 1:SUBLANES - 1 + tm, :]
    u2 = ubuf_ref[SUBLANES - 2:SUBLANES - 2 + tm, :]
    y = cw_ref[2:3, :] * u + cw_ref[1:2, :] * u1 + cw_ref[0:1, :] * u2
    ubuf_ref[0:SUBLANES, :] = u[tm - SUBLANES:, :]
    o_ref[0] = x + _dot((b_gate * y).astype(BF16), wout_ref[...])


def _conv_mixer(h, norm_g, w_in, conv_w, w_out, *, batch, tm=256):
    rows, d = h.shape
    seq = rows // batch
    win = w_in.reshape(d, 3, d).transpose(1, 0, 2).astype(BF16)
    out = pl.pallas_call(
        functools.partial(_conv_body, tm=tm, d_model=d),
        grid=(batch, seq // tm),
        in_specs=[
            pl.BlockSpec((1, tm, d), lambda b, i: (b, i, 0)),
            _const_spec((1, d)),
            _const_spec((3, d, d)),
            _const_spec((conv_w.shape[0], d)),
            _const_spec((d, d)),
        ],
        out_specs=pl.BlockSpec((1, tm, d), lambda b, i: (b, i, 0)),
        out_shape=jax.ShapeDtypeStruct((batch, seq, d), F32),
        scratch_shapes=[pltpu.VMEM((tm + SUBLANES, d), F32)],
        compiler_params=_cparams(("arbitrary", "arbitrary")),
        name="conv_mixer",
    )(h.reshape(batch, seq, d), norm_g.reshape(1, d), win, conv_w, w_out.astype(BF16))
    return out.reshape(rows, d)


def _pad_head_cols(w, n_slices):
    d = w.shape[0]
    w = w.reshape(d, n_slices, HEAD_DIM)
    return jnp.pad(w, ((0, 0), (0, 0), (0, LANES - HEAD_DIM))).reshape(d, n_slices * LANES)


def _pad_lanes(v):
    return jnp.pad(v, [(0, 0)] * (v.ndim - 1) + [(0, LANES - HEAD_DIM)])


def _kvproj_body(x_ref, g_ref, w_ref, wvt_ref, kn_ref, kc_ref, vc_ref, ks_ref, vst_ref, kw_ref, vw_ref, *,
                 tm, d_model):
    i = pl.program_id(1)
    xn = _rms(x_ref[0], g_ref[...], d_model).astype(BF16)
    lane = lax.broadcasted_iota(jnp.int32, (tm, LANES), 1)
    ones_col = jnp.where(lane == FLAG_LANE, 1.0, 0.0)
    ones_rows = jnp.where(lax.broadcasted_iota(jnp.int32, (LANES - HEAD_DIM, tm), 0) == 0, 1.0, 0.0)
    blk = (i * tm + lax.broadcasted_iota(jnp.int32, (tm, LANES), 0)) // SLC_BLOCK
    onehot = jnp.where(lane == blk, 1.0, 0.0).astype(BF16)
    G = N_KV_GROUPS
    pairs = [_dot(xn, w_ref[p]) for p in range(w_ref.shape[0])]
    vst = _dot_t(wvt_ref[...], xn)
    for g in range(G):
        def col(br, kv):
            s = (br * 2 + kv) * G + g
            blk2 = pairs[s // 2] if s % 2 == 0 else pltpu.roll(pairs[s // 2], HEAD_DIM, axis=1)
            return jnp.where(lane < HEAD_DIM, blk2, 0.0)
        kc_ref[0, g] = col(0, 0)
        vc_ref[0, g] = col(0, 1)
        ks = _rms(col(1, 0), kn_ref[1:2, :], HEAD_DIM).astype(BF16)
        ks_ref[0, g] = jnp.concatenate([ks, onehot], axis=1)
        vst_ref[0, g] = jnp.concatenate([vst[g * HEAD_DIM:(g + 1) * HEAD_DIM], ones_rows], axis=0).astype(BF16)
        kw_ref[0, g] = _rms(col(2, 0), kn_ref[2:3, :], HEAD_DIM).astype(BF16)
        vw_ref[0, g] = (col(2, 1) + ones_col).astype(BF16)


def _kv_proj(h, kv_norm, w_kv, k_norm, *, batch, tm=256):
    rows, d = h.shape
    seq = rows // batch
    G = N_KV_GROUPS
    n_sl = N_BRANCH * 2 * G
    n_pair = n_sl * HEAD_DIM // LANES
    w = w_kv.reshape(d, n_pair, LANES).transpose(1, 0, 2).astype(BF16)
    v_sel = (1 * 2 + 1) * G * HEAD_DIM
    wvt = w_kv[:, v_sel:v_sel + G * HEAD_DIM].T.astype(BF16)
    kn = _pad_lanes(k_norm)
    sds = lambda lanes, dt: jax.ShapeDtypeStruct((batch, G, seq, lanes), dt)
    ospec = lambda lanes: pl.BlockSpec((1, G, tm, lanes), lambda b, i: (b, 0, i, 0))
    kc_raw, vc_raw, ks, vst, kw, vw = pl.pallas_call(
        functools.partial(_kvproj_body, tm=tm, d_model=d),
        grid=(batch, seq // tm),
        in_specs=[
            pl.BlockSpec((1, tm, d), lambda b, i: (b, i, 0)),
            _const_spec((1, d)),
            _const_spec((n_pair, d, LANES)),
            _const_spec((G * HEAD_DIM, d)),
            _const_spec((N_BRANCH, LANES)),
        ],
        out_specs=[ospec(LANES), ospec(LANES), ospec(2 * LANES),
                   pl.BlockSpec((1, G, LANES, tm), lambda b, i: (b, 0, 0, i)), ospec(LANES), ospec(LANES)],
        out_shape=[sds(LANES, F32), sds(LANES, F32), sds(2 * LANES, BF16),
                   jax.ShapeDtypeStruct((batch, G, LANES, seq), BF16), sds(LANES, BF16), sds(LANES, BF16)],
        compiler_params=_cparams(("parallel", "parallel")),
        name="kv_proj",
    )(h.reshape(batch, seq, d), kv_norm.reshape(1, d), w, wvt, kn)
    vst = vst.reshape(batch, G, LANES, seq // KCHUNK, KCHUNK).transpose(0, 1, 3, 2, 4)
    return kc_raw, vc_raw, ks, vst, kw, vw


def _gelu_tanh(x):
    return 0.5 * x * (1.0 + jnp.tanh(math.sqrt(2.0 / math.pi) * (x + 0.044715 * (x * x * x))))


def _compress_body(kr_ref, vr_ref, pek_ref, pev_ref, w1k_ref, w2k_ref, w1v_ref, w2v_ref, kn_ref,
                   kc_ref, vc_ref, *, ncw, front):
    row = lax.broadcasted_iota(jnp.int32, (ncw, LANES), 0)
    row_lane = lax.broadcasted_iota(jnp.int32, (ncw, LANES), 1)
    lane = lax.broadcasted_iota(jnp.int32, (front, LANES), 1)

    def mlp(r_ref, pe_ref, w1_ref, w2_ref):
        r = r_ref[0, 0]
        a = _dot((r + pe_ref[0:1, :]).astype(BF16), w1_ref[0])
        b = _dot((r + pe_ref[1:2, :]).astype(BF16), w1_ref[1])
        hid = a + pltpu.roll(b, ncw - 1, axis=0)
        out = _dot(_gelu_tanh(hid).astype(BF16), w2_ref[...])
        return jnp.where(row < ncw - 1, out, 0.0)

    kc = _rms(mlp(kr_ref, pek_ref, w1k_ref, w2k_ref), kn_ref[0:1, :], HEAD_DIM)
    vc = mlp(vr_ref, pev_ref, w1v_ref, w2v_ref)
    kc_ref[0, 0, 0:front, :] = jnp.where(lane == FLAG_LANE, 1.0, 0.0)
    vc_ref[0, 0, 0:front, :] = jnp.zeros((front, LANES), F32)
    kc_ref[0, 0, front:front + ncw, :] = kc
    vc_ref[0, 0, front:front + ncw, :] = vc + jnp.where(row_lane == FLAG_LANE, 1.0, 0.0)


def _compress(kc_raw, vc_raw, pe_k, pe_v, w1_k, w2_k, w1_v, w2_v, k_norm):
    batch, G, seq, _ = kc_raw.shape
    ncw = seq // CMP_STRIDE
    front = ncw - QT // CMP_STRIDE
    tok = CMP_STRIDE * LANES
    hid = w1_k.shape[1]

    def prep_w1(w1):
        w = w1.reshape(CMP_BLOCK, HEAD_DIM, hid)
        w = jnp.pad(w, ((0, 0), (0, LANES - HEAD_DIM), (0, 0)))
        return w.reshape(2, tok, hid).astype(BF16)

    def prep_pe(pe):
        return _pad_lanes(pe).reshape(2, tok)

    def prep_w2(w2):
        return _pad_lanes(w2).astype(BF16)

    rspec = pl.BlockSpec((1, 1, ncw, tok), lambda b, g: (b, g, 0, 0))
    ospec = pl.BlockSpec((1, 1, front + ncw, LANES), lambda b, g: (b, g, 0, 0))
    osds = jax.ShapeDtypeStruct((batch, G, front + ncw, LANES), F32)
    return pl.pallas_call(
        functools.partial(_compress_body, ncw=ncw, front=front),
        grid=(batch, G),
        in_specs=[rspec, rspec, _const_spec((2, tok)), _const_spec((2, tok)),
                  _const_spec((2, tok, hid)), _const_spec((hid, LANES)),
                  _const_spec((2, tok, hid)), _const_spec((hid, LANES)),
                  _const_spec((N_BRANCH, LANES))],
        out_specs=[ospec, ospec],
        out_shape=[osds, osds],
        compiler_params=_cparams(("parallel", "parallel")),
        name="compress",
    )(kc_raw.reshape(batch, G, ncw, tok), vc_raw.reshape(batch, G, ncw, tok),
      prep_pe(pe_k), prep_pe(pe_v), prep_w1(w1_k), prep_w2(w2_k), prep_w1(w1_v), prep_w2(w2_v),
      _pad_lanes(k_norm))


def _qproj_body(x_ref, g_ref, wq_ref, wg_ref, qn_ref, q_ref, gt_ref, *, tm, d_model):
    xn = _rms(x_ref[0], g_ref[...], d_model).astype(BF16)
    lane = lax.broadcasted_iota(jnp.int32, (tm, LANES), 1)
    scale = HEAD_DIM ** -0.5 * LOG2E
    for p in range(N_HEADS // 2):
        pair = _dot(xn, wq_ref[p])
        for half, blk2 in enumerate((pair, pltpu.roll(pair, HEAD_DIM, axis=1))):
            q = _rms(jnp.where(lane < HEAD_DIM, blk2, 0.0), qn_ref[...], HEAD_DIM) * scale
            q_ref[0, 2 * p + half] = jnp.where(lane == FLAG_LANE, NEG, q).astype(BF16)
    gates = 1.0 / (1.0 + jnp.exp(-_dot(xn, wg_ref[...])))
    n_gate = HEADS_PER_GROUP * N_BRANCH
    for g in range(N_KV_GROUPS):
        shifted = gates if g == 0 else pltpu.roll(gates, LANES - g * n_gate, axis=1)
        gt_ref[0, :, g * LANES:(g + 1) * LANES] = shifted


def _q_proj(h, norm_g, w_q, q_norm, *, batch, tm=256):
    rows, d = h.shape
    seq = rows // batch
    G, HG = N_KV_GROUPS, HEADS_PER_GROUP
    nq = N_HEADS * HEAD_DIM
    wq = w_q[:, :nq].reshape(d, nq // LANES, LANES).transpose(1, 0, 2).astype(BF16)
    n_gates = w_q.shape[1] - nq
    assert n_gates <= LANES
    wg = jnp.pad(w_q[:, nq:], ((0, 0), (0, LANES - n_gates))).astype(BF16)
    return pl.pallas_call(
        functools.partial(_qproj_body, tm=tm, d_model=d),
        grid=(batch, seq // tm),
        in_specs=[
            pl.BlockSpec((1, tm, d), lambda b, i: (b, i, 0)),
            _const_spec((1, d)),
            _const_spec((nq // LANES, d, LANES)),
            _const_spec((d, LANES)),
            _const_spec((1, LANES)),
        ],
        out_specs=[pl.BlockSpec((1, N_HEADS, tm, LANES), lambda b, i: (b, 0, i, 0)),
                   pl.BlockSpec((1, tm, G * LANES), lambda b, i: (b, i, 0))],
        out_shape=[jax.ShapeDtypeStruct((batch, N_HEADS, seq, LANES), BF16),
                   jax.ShapeDtypeStruct((batch, seq, G * LANES), F32)],
        compiler_params=_cparams(("parallel", "parallel")),
        name="q_proj",
    )(h.reshape(batch, seq, d), norm_g.reshape(1, d), wq, wg, _pad_lanes(q_norm.reshape(1, HEAD_DIM)))


def _tables_body(thr_ref, rb_ref, tblt_ref, tblw_ref, cb_ref, *, ncw, c_off):
    h = pl.program_id(0)

    def bias_of(rel):
        v = jnp.full(rel.shape, rb_ref[0, h], F32)
        for k in range(1, N_BUCKETS):
            v = jnp.where(rel >= thr_ref[k], rb_ref[k, h], v)
        return v * LOG2E

    row = lax.broadcasted_iota(jnp.int32, (QT, QT), 0)
    col = lax.broadcasted_iota(jnp.int32, (QT, QT), 1)

    def tile_t(d, carry):
        rel = (col - row) + d * QT
        tblt_ref[0, d] = jnp.where(rel < 0, NEG, bias_of(rel))
        return carry

    lax.fori_loop(0, FAR_TILE + 1, tile_t, 0)
    tblt_ref[0, MASK_TILE] = jnp.full((QT, QT), NEG, F32)

    def tile_w(d, carry):
        rel = (row - col) + d * QT
        tblw_ref[0, d] = jnp.where((rel < 0) | (rel >= WINDOW), NEG, bias_of(rel))
        return carry

    lax.fori_loop(0, WIN_TILES, tile_w, 0)
    tblw_ref[0, WIN_MASK_TILE] = jnp.full((QT, QT), NEG, F32)
    relc = (lax.broadcasted_iota(jnp.int32, (QT, ncw), 0)
            - CMP_STRIDE * lax.broadcasted_iota(jnp.int32, (QT, ncw), 1) + c_off)
    cb_ref[0] = jnp.where(relc < 0, NEG, bias_of(relc))


def _bucket_thresholds(seq):
    n = jnp.arange(seq)
    nf = jnp.maximum(n, 1).astype(jnp.float32)
    large = MAX_EXACT + (jnp.log(nf / MAX_EXACT) / math.log(MAX_DISTANCE / MAX_EXACT)
                         * (N_BUCKETS - MAX_EXACT)).astype(jnp.int32)
    large = jnp.minimum(large, N_BUCKETS - 1)
    bucket = jnp.where(n < MAX_EXACT, n, large)
    return jnp.sum(bucket[None, :] < jnp.arange(N_BUCKETS)[:, None], axis=1).astype(jnp.int32)


def _bias_tables(rel_bias, seq):
    ncw = seq // CMP_STRIDE
    front = ncw - QT // CMP_STRIDE
    c_off = CMP_STRIDE * front - (CMP_BLOCK - 1)
    smem = pl.BlockSpec(memory_space=pltpu.SMEM)
    return pl.pallas_call(
        functools.partial(_tables_body, ncw=ncw, c_off=c_off),
        grid=(N_HEADS,),
        in_specs=[smem, smem],
        out_specs=[pl.BlockSpec((1, N_TILES, QT, QT), lambda h: (h, 0, 0, 0)),
                   pl.BlockSpec((1, WIN_TILES + 1, QT, QT), lambda h: (h, 0, 0, 0)),
                   pl.BlockSpec((1, QT, ncw), lambda h: (h, 0, 0))],
        out_shape=[jax.ShapeDtypeStruct((N_HEADS, N_TILES, QT, QT), F32),
                   jax.ShapeDtypeStruct((N_HEADS, WIN_TILES + 1, QT, QT), F32),
                   jax.ShapeDtypeStruct((N_HEADS, QT, ncw), F32)],
        compiler_params=_cparams(("parallel",)),
        name="bias_tables",
    )(_bucket_thresholds(seq), rel_bias)


def _nsa_body(q_ref, gt_ref, ks_ref, vst_ref, kw_ref, vw_ref, kc_ref, vc_ref, ov_ref, tblt_ref, tblw_ref, cb_ref,
              o_ref, sa_ref, sb_ref, pa_ref, pb_ref, lhs_ref, m_ref, ala_ref, alb_ref, acct_ref, *, ncw):
    HG = HEADS_PER_GROUP
    R = HG * QT
    qi = pl.program_id(2)
    q2 = q_ref[0].reshape(R, LANES)

    def softmax_strips(s_ref, p_ref, width, bias_fn):
        for st in range(R // STRIP):
            rows = slice(st * STRIP, (st + 1) * STRIP)
            s = s_ref[rows, :width] + bias_fn((st * STRIP) // QT, (st * STRIP) % QT)
            p_ref[rows, :width] = jnp.exp2(s - jnp.max(s, axis=-1, keepdims=True)).astype(BF16)

    c0 = pl.multiple_of(qi * (QT // CMP_STRIDE), SUBLANES)
    kc = kc_ref[0, 0, pl.ds(c0, ncw), :].astype(BF16)
    vo = jnp.concatenate([vc_ref[0, 0, pl.ds(c0, ncw), :], ov_ref[pl.ds(c0, ncw), :]], axis=1).astype(BF16)
    sa_ref[:, :ncw] = _dot_t(q2, kc)
    softmax_strips(sa_ref, pa_ref, ncw, lambda h, r: cb_ref[h, r:r + STRIP, :])
    r = _dot(pa_ref[:, :ncw], vo)
    t_row = qi * QT + (lax.broadcasted_iota(jnp.int32, (R, 1), 0) & (QT - 1))
    r = jnp.where(t_row >= CMP_BLOCK - 1, r / r[:, FLAG_LANE:FLAG_LANE + 1], 0.0)
    o_c = r[:, :LANES]
    imp = r[0:QT, LANES:]
    for h in range(1, HG):
        imp = imp + r[h * QT:(h + 1) * QT, LANES:]

    kst = jnp.maximum(qi - WINDOW // QT, 0)
    w0 = pl.multiple_of(kst * QT, QT)
    ids = []
    for u in range(N_WSUB):
        d = qi - kst - u
        ids.append(jnp.where(d < 0, WIN_MASK_TILE, d))
    sb_ref[:, :WIN_KEYS] = _dot_t(q2, kw_ref[0, 0, pl.ds(w0, WIN_KEYS), :])
    softmax_strips(sb_ref, pb_ref, WIN_KEYS,
                   lambda h, r: jnp.concatenate([tblw_ref[h, t, r:r + STRIP, :] for t in ids], axis=1))
    acc_w = _dot(pb_ref[:, :WIN_KEYS], vw_ref[0, 0, pl.ds(w0, WIN_KEYS), :])
    o_w = acc_w / acc_w[:, FLAG_LANE:FLAG_LANE + 1]

    blk = lax.broadcasted_iota(jnp.int32, (LANES, QT), 0)
    blk_t = (qi * QT + lax.broadcasted_iota(jnp.int32, (LANES, QT), 1)) // SLC_BLOCK
    forced = (blk == 0) | (blk == blk_t) | (blk == blk_t - 1)
    score = jnp.where(forced, REMOVED, jnp.where(blk <= blk_t, imp.T, NEG))
    sel = jnp.where(forced, 1.0, 0.0)
    for _ in range(N_SELECT - N_FORCED):
        top = jnp.max(score, axis=0, keepdims=True)
        first = jnp.min(jnp.where(score == top, blk, LANES), axis=0, keepdims=True)
        hit = blk == first
        sel = jnp.where(hit, 1.0, sel)
        score = jnp.where(hit, REMOVED, score)
    sel_neg = ((sel - 1.0) * BIG).T.astype(BF16)
    lhs_ref[...] = jnp.concatenate([q2, jnp.concatenate([sel_neg] * HG, axis=0)], axis=1)

    m_ref[...] = jnp.full((1, R), REMOVED, F32)
    acct_ref[...] = jnp.zeros((LANES, R), F32)
    n_chunks = qi // N_SUB + 1

    def scores(s_ref, kj):
        k0 = pl.multiple_of(kj * KCHUNK, KCHUNK)
        s_ref[:, :R] = _dot_t(ks_ref[0, 0, pl.ds(k0, KCHUNK), :], lhs_ref[...])

    def accumulate(s_ref, p_ref, al_ref, kj):
        for h in range(HG):
            cols = slice(h * QT, (h + 1) * QT)
            mx = m_ref[:, cols]
            for u in range(N_SUB):
                keys = slice(u * QT, (u + 1) * QT)
                d = qi - kj * N_SUB - u
                tile = jnp.where(d < 0, MASK_TILE, jnp.minimum(d, FAR_TILE))
                sb = s_ref[keys, cols] + tblt_ref[h, tile]
                s_ref[keys, cols] = sb
                mx = jnp.maximum(mx, jnp.max(sb, axis=0, keepdims=True))
            al_ref[:, cols] = jnp.exp2(m_ref[:, cols] - mx)
            m_ref[:, cols] = mx
            p_ref[:, cols] = jnp.exp2(s_ref[:, cols] - mx).astype(BF16)
        acct_ref[...] = al_ref[...] * acct_ref[...] + _dot(vst_ref[0, 0, kj], p_ref[:, :R])

    scores(sa_ref, 0)

    def chunk_pair(j, carry):
        a = 2 * j
        scores(sb_ref, a + 1)
        accumulate(sa_ref, pa_ref, ala_ref, a)
        scores(sa_ref, jnp.minimum(a + 2, n_chunks - 1))
        accumulate(sb_ref, pb_ref, alb_ref, a + 1)
        return carry

    lax.fori_loop(0, n_chunks // 2, chunk_pair, 0)

    @pl.when(n_chunks % 2 == 1)
    def _():
        accumulate(sa_ref, pa_ref, ala_ref, n_chunks - 1)

    acct = acct_ref[...]
    o_st = acct / acct[FLAG_LANE:FLAG_LANE + 1, :]

    gt = gt_ref[0]
    outs = []
    for h in range(HG):
        rows = slice(h * QT, (h + 1) * QT)
        c = h * N_BRANCH
        outs.append(gt[:, c:c + 1] * o_c[rows] + gt[:, c + 1:c + 2] * o_st[:, rows].T
                    + gt[:, c + 2:c + 3] * o_w[rows])
    o_ref[0] = jnp.concatenate(outs, axis=1).astype(BF16)


def _overlap_table(seq):
    ncw = seq // CMP_STRIDE
    front = ncw - QT // CMP_STRIDE
    nc = (seq - CMP_BLOCK) // CMP_STRIDE + 1
    ns = seq // SLC_BLOCK
    c_start = jnp.arange(nc) * CMP_STRIDE
    s_start = jnp.arange(ns) * SLC_BLOCK
    ov = (jnp.clip(jnp.minimum(c_start[:, None] + CMP_BLOCK, s_start[None, :] + SLC_BLOCK)
                   - jnp.maximum(c_start[:, None], s_start[None, :]), 0) / CMP_STRIDE).astype(F32)
    return jnp.pad(ov, ((front, ncw - nc), (0, LANES - ns)))


def _nsa(q, gates, ks, vst, kw, vw, kc, vc, tblt, tblw, cb):
    batch, _, seq, _ = q.shape
    G, HG = N_KV_GROUPS, HEADS_PER_GROUP
    ncw = seq // CMP_STRIDE
    crow = kc.shape[2]
    ov = _overlap_table(seq)
    rows = HG * QT
    assert rows == KCHUNK
    width = max(ncw, WIN_KEYS, KCHUNK)
    kvspec = lambda lanes: pl.BlockSpec((1, 1, seq, lanes), lambda g, b, i: (b, g, 0, 0))
    cspec = pl.BlockSpec((1, 1, crow, LANES), lambda g, b, i: (b, g, 0, 0))
    per_group = lambda n, w: pl.BlockSpec((HG, n, QT, w), lambda g, b, i: (g, 0, 0, 0),
                                          pipeline_mode=pl.Buffered(1))
    return pl.pallas_call(
        functools.partial(_nsa_body, ncw=ncw),
        grid=(G, batch, seq // QT),
        in_specs=[
            pl.BlockSpec((1, HG, QT, LANES), lambda g, b, i: (b, g, i, 0)),
            pl.BlockSpec((1, QT, LANES), lambda g, b, i: (b, i, g)),
            kvspec(2 * LANES),
            pl.BlockSpec((1, 1, seq // KCHUNK, LANES, KCHUNK), lambda g, b, i: (b, g, 0, 0, 0)),
            kvspec(LANES), kvspec(LANES),
            cspec, cspec,
            _const_spec(ov.shape),
            per_group(N_TILES, QT), per_group(WIN_TILES + 1, QT),
            pl.BlockSpec((HG, QT, ncw), lambda g, b, i: (g, 0, 0), pipeline_mode=pl.Buffered(1)),
        ],
        out_specs=pl.BlockSpec((1, QT, HG * LANES), lambda g, b, i: (b, i, g)),
        out_shape=jax.ShapeDtypeStruct((batch, seq, N_HEADS * LANES), BF16),
        scratch_shapes=[pltpu.VMEM((rows, width), F32), pltpu.VMEM((rows, width), F32),
                        pltpu.VMEM((rows, width), BF16), pltpu.VMEM((rows, width), BF16),
                        pltpu.VMEM((rows, 2 * LANES), BF16),
                        pltpu.VMEM((1, rows), F32), pltpu.VMEM((1, rows), F32), pltpu.VMEM((1, rows), F32),
                        pltpu.VMEM((LANES, rows), F32)],
        compiler_params=_cparams(("arbitrary", "arbitrary", "arbitrary")),
        name="nsa",
    )(q, gates, ks, vst, kw, vw, kc, vc, ov, tblt, tblw, cb)


def _oproj_body(x_ref, a_ref, w_ref, o_ref):
    o_ref[...] = x_ref[...] + _dot(a_ref[...], w_ref[...])


def _o_proj(h, attn, w_o, *, tm=512):
    rows, d = h.shape
    k = attn.shape[1]
    w = jnp.pad(w_o.reshape(N_HEADS, HEAD_DIM, d), ((0, 0), (0, LANES - HEAD_DIM), (0, 0)))
    w = w.reshape(k, d).astype(BF16)
    return pl.pallas_call(
        _oproj_body,
        grid=(rows // tm,),
        in_specs=[pl.BlockSpec((tm, d), lambda i: (i, 0)),
                  pl.BlockSpec((tm, k), lambda i: (i, 0)),
                  _const_spec((k, d))],
        out_specs=pl.BlockSpec((tm, d), lambda i: (i, 0)),
        out_shape=jax.ShapeDtypeStruct((rows, d), F32),
        compiler_params=_cparams(("parallel",)),
        name="o_proj",
    )(h, attn, w)


def kernel(x, ffn1_norm, ffn1_w_in, ffn1_w_out, mix_norm, ffn2_norm, ffn2_w_in, ffn2_w_out, conv_w_in, conv_w, conv_w_out, attn_w_q, attn_q_norm, attn_w_o, kv_norm, w_kv, k_norm, cmp_pe_k, cmp_pe_v, cmp_w1_k, cmp_w2_k, cmp_w1_v, cmp_w2_v, rel_bias):
    batch, seq, d = x.shape
    depth = ffn1_norm.shape[0]
    n_a = conv_w_in.shape[0]
    assert seq % KCHUNK == 0 and seq // SLC_BLOCK <= LANES and seq >= WIN_KEYS
    h = x.reshape(batch * seq, d)
    kv = tables = None
    for i in range(depth):
        if i == n_a:
            kc_raw, vc_raw, ks, vst, kw, vw = _kv_proj(h, kv_norm, w_kv, k_norm, batch=batch)
            kc, vc = _compress(kc_raw, vc_raw, cmp_pe_k, cmp_pe_v, cmp_w1_k, cmp_w2_k,
                               cmp_w1_v, cmp_w2_v, k_norm)
            kv = (ks, vst, kw, vw, kc, vc)
            tables = _bias_tables(rel_bias, seq)
        h = _ffn(h, ffn1_norm[i], ffn1_w_in[i], ffn1_w_out[i])
        if i < n_a:
            h = _conv_mixer(h, mix_norm[i], conv_w_in[i], conv_w[i], conv_w_out[i], batch=batch)
        else:
            j = i - n_a
            q, gates = _q_proj(h, mix_norm[i], attn_w_q[j], attn_q_norm[j], batch=batch)
            attn = _nsa(q, gates, *kv, *tables)
            h = _o_proj(h, attn.reshape(batch * seq, -1), attn_w_o[j])
        h = _ffn(h, ffn2_norm[i], ffn2_w_in[i], ffn2_w_out[i])
    return h.reshape(batch, seq, d)
```

```python
import functools
import math

import jax
import jax.numpy as jnp
from jax import lax
from jax.experimental import pallas as pl
from jax.experimental.pallas import tpu as pltpu

F32 = jnp.float32
BF16 = jnp.bfloat16

N_HEADS = 16
N_KV_GROUPS = 4
HEADS_PER_GROUP = N_HEADS // N_KV_GROUPS
HEAD_DIM = 64
N_BRANCH = 3
CMP_BLOCK = 32
CMP_STRIDE = 16
SLC_BLOCK = 64
N_SELECT = 16
N_FORCED = 3
WINDOW = 512
N_BUCKETS = 32
MAX_EXACT = N_BUCKETS // 2
MAX_DISTANCE = 4096
EPS = 1e-6
NEG = -1e30
BIG = 1e30
REMOVED = -3e38
LOG2E = math.log2(math.e)

LANES = 128
SUBLANES = 8
VMEM_LIMIT = 56 * 1024 * 1024

QT = 128
TILES_PER_STEP = 2
KCHUNK = 512
N_SUB = KCHUNK // QT
STRIP = 32
WIN_KEYS = WINDOW + QT
N_WSUB = WIN_KEYS // QT
FAR_TILE = int(math.ceil((MAX_EXACT * (MAX_DISTANCE / MAX_EXACT) ** ((N_BUCKETS - MAX_EXACT - 1) / (N_BUCKETS - MAX_EXACT))
                          + QT) / QT))
MASK_TILE = FAR_TILE + 1
N_TILES = FAR_TILE + 2
WIN_TILES = WINDOW // QT + 1
WIN_MASK_TILE = WIN_TILES
FLAG_LANE = HEAD_DIM


def _cparams(sem):
    return pltpu.CompilerParams(dimension_semantics=sem, vmem_limit_bytes=VMEM_LIMIT)


def _const_spec(shape):
    nd = len(shape)
    return pl.BlockSpec(shape, lambda *_: (0,) * nd, pipeline_mode=pl.Buffered(1))


def _rms(x, g, n):
    ms = jnp.sum(x * x, axis=-1, keepdims=True) * (1.0 / n)
    return x * lax.rsqrt(ms + EPS) * g


def _dot(a, b):
    return jnp.dot(a, b, preferred_element_type=F32)


def _dot_t(a, b):
    return lax.dot_general(a, b, (((1,), (1,)), ((), ())), preferred_element_type=F32)


def _ffn_body(x_ref, g_ref, wg_ref, wu_ref, wo_ref, o_ref, acc_ref, *, n_chunks, d_model):
    x = x_ref[...]
    xn = _rms(x, g_ref[...], d_model).astype(BF16)
    acc_ref[...] = jnp.zeros_like(acc_ref)

    def chunk(c, carry):
        gate = _dot(xn, wg_ref[c])
        up = _dot(xn, wu_ref[c])
        act = (gate * (1.0 / (1.0 + jnp.exp(-gate))) * up).astype(BF16)
        acc_ref[...] += _dot(act, wo_ref[c])
        return carry

    lax.fori_loop(0, n_chunks, chunk, 0)
    o_ref[...] = x + 0.5 * acc_ref[...]


def _ffn(h, norm_g, w_in, w_out, *, tm=1024, ck=256):
    rows, d = h.shape
    dff = w_out.shape[0]
    nch = dff // ck
    wg = w_in[:, :dff].reshape(d, nch, ck).transpose(1, 0, 2).astype(BF16)
    wu = w_in[:, dff:].reshape(d, nch, ck).transpose(1, 0, 2).astype(BF16)
    wo = w_out.reshape(nch, ck, d).astype(BF16)
    return pl.pallas_call(
        functools.partial(_ffn_body, n_chunks=nch, d_model=d),
        grid=(rows // tm,),
        in_specs=[
            pl.BlockSpec((tm, d), lambda i: (i, 0)),
            _const_spec((1, d)),
            _const_spec((nch, d, ck)),
            _const_spec((nch, d, ck)),
            _const_spec((nch, ck, d)),
        ],
        out_specs=pl.BlockSpec((tm, d), lambda i: (i, 0)),
        out_shape=jax.ShapeDtypeStruct((rows, d), F32),
        scratch_shapes=[pltpu.VMEM((tm, d), F32)],
        compiler_params=_cparams(("parallel",)),
        name="ffn",
    )(h, norm_g.reshape(1, d), wg, wu, wo)


def _conv_body(x_ref, g_ref, win_ref, cw_ref, wout_ref, o_ref, ubuf_ref, *, tm, d_model):
    i = pl.program_id(1)
    x = x_ref[0]
    xn = _rms(x, g_ref[...], d_model).astype(BF16)
    b_gate = _dot(xn, win_ref[0])
    c_gate = _dot(xn, win_ref[1])
    v = _dot(xn, win_ref[2])
    u = c_gate * v

    @pl.when(i == 0)
    def _():
        ubuf_ref[0:SUBLANES, :] = jnp.zeros((SUBLANES, d_model), F32)

    ubuf_ref[SUBLANES:SUBLANES + tm, :] = u
    u1 = ubuf_ref[SUBLANES - 1:SUBLANES - 1 + tm, :]
    u2 = ubuf_ref[SUBLANES - 2:SUBLANES - 2 + tm, :]
    y = cw_ref[2:3, :] * u + cw_ref[1:2, :] * u1 + cw_ref[0:1, :] * u2
    ubuf_ref[0:SUBLANES, :] = u[tm - SUBLANES:, :]
    o_ref[0] = x + _dot((b_gate * y).astype(BF16), wout_ref[...])


def _conv_mixer(h, norm_g, w_in, conv_w, w_out, *, batch, tm=256):
    rows, d = h.shape
    seq = rows // batch
    win = w_in.reshape(d, 3, d).transpose(1, 0, 2).astype(BF16)
    out = pl.pallas_call(
        functools.partial(_conv_body, tm=tm, d_model=d),
        grid=(batch, seq // tm),
        in_specs=[
            pl.BlockSpec((1, tm, d), lambda b, i: (b, i, 0)),
            _const_spec((1, d)),
            _const_spec((3, d, d)),
            _const_spec((conv_w.shape[0], d)),
            _const_spec((d, d)),
        ],
        out_specs=pl.BlockSpec((1, tm, d), lambda b, i: (b, i, 0)),
        out_shape=jax.ShapeDtypeStruct((batch, seq, d), F32),
        scratch_shapes=[pltpu.VMEM((tm + SUBLANES, d), F32)],
        compiler_params=_cparams(("arbitrary", "arbitrary")),
        name="conv_mixer",
    )(h.reshape(batch, seq, d), norm_g.reshape(1, d), win, conv_w, w_out.astype(BF16))
    return out.reshape(rows, d)


def _pad_head_cols(w, n_slices):
    d = w.shape[0]
    w = w.reshape(d, n_slices, HEAD_DIM)
    return jnp.pad(w, ((0, 0), (0, 0), (0, LANES - HEAD_DIM))).reshape(d, n_slices * LANES)


def _pad_lanes(v):
    return jnp.pad(v, [(0, 0)] * (v.ndim - 1) + [(0, LANES - HEAD_DIM)])


def _kvproj_body(x_ref, g_ref, w_ref, wvt_ref, kn_ref, kc_ref, vc_ref, ks_ref, vst_ref, kw_ref, vw_ref, *,
                 tm, d_model):
    i = pl.program_id(1)
    xn = _rms(x_ref[0], g_ref[...], d_model).astype(BF16)
    lane = lax.broadcasted_iota(jnp.int32, (tm, LANES), 1)
    ones_col = jnp.where(lane == FLAG_LANE, 1.0, 0.0)
    ones_rows = jnp.where(lax.broadcasted_iota(jnp.int32, (LANES - HEAD_DIM, tm), 0) == 0, 1.0, 0.0)
    blk = (i * tm + lax.broadcasted_iota(jnp.int32, (tm, LANES), 0)) // SLC_BLOCK
    onehot = jnp.where(lane == blk, 1.0, 0.0).astype(BF16)
    G = N_KV_GROUPS
    pairs = [_dot(xn, w_ref[p]) for p in range(w_ref.shape[0])]
    vst = _dot_t(wvt_ref[...], xn)
    for g in range(G):
        def col(br, kv):
            s = (br * 2 + kv) * G + g
            blk2 = pairs[s // 2] if s % 2 == 0 else pltpu.roll(pairs[s // 2], HEAD_DIM, axis=1)
            return jnp.where(lane < HEAD_DIM, blk2, 0.0)
        kc_ref[0, g] = col(0, 0)
        vc_ref[0, g] = col(0, 1)
        ks = _rms(col(1, 0), kn_ref[1:2, :], HEAD_DIM).astype(BF16)
        ks_ref[0, g] = jnp.concatenate([ks, onehot], axis=1)
        vst_ref[0, g] = jnp.concatenate([vst[g * HEAD_DIM:(g + 1) * HEAD_DIM], ones_rows], axis=0).astype(BF16)
        kw_ref[0, g] = _rms(col(2, 0), kn_ref[2:3, :], HEAD_DIM).astype(BF16)
        vw_ref[0, g] = (col(2, 1) + ones_col).astype(BF16)


def _kv_proj(h, kv_norm, w_kv, k_norm, *, batch, tm=256):
    rows, d = h.shape
    seq = rows // batch
    G = N_KV_GROUPS
    n_sl = N_BRANCH * 2 * G
    n_pair = n_sl * HEAD_DIM // LANES
    w = w_kv.reshape(d, n_pair, LANES).transpose(1, 0, 2).astype(BF16)
    v_sel = (1 * 2 + 1) * G * HEAD_DIM
    wvt = w_kv[:, v_sel:v_sel + G * HEAD_DIM].T.astype(BF16)
    kn = _pad_lanes(k_norm)
    sds = lambda lanes, dt: jax.ShapeDtypeStruct((batch, G, seq, lanes), dt)
    ospec = lambda lanes: pl.BlockSpec((1, G, tm, lanes), lambda b, i: (b, 0, i, 0))
    kc_raw, vc_raw, ks, vst, kw, vw = pl.pallas_call(
        functools.partial(_kvproj_body, tm=tm, d_model=d),
        grid=(batch, seq // tm),
        in_specs=[
            pl.BlockSpec((1, tm, d), lambda b, i: (b, i, 0)),
            _const_spec((1, d)),
            _const_spec((n_pair, d, LANES)),
            _const_spec((G * HEAD_DIM, d)),
            _const_spec((N_BRANCH, LANES)),
        ],
        out_specs=[ospec(LANES), ospec(LANES), ospec(2 * LANES),
                   pl.BlockSpec((1, G, LANES, tm), lambda b, i: (b, 0, 0, i)), ospec(LANES), ospec(LANES)],
        out_shape=[sds(LANES, F32), sds(LANES, F32), sds(2 * LANES, BF16),
                   jax.ShapeDtypeStruct((batch, G, LANES, seq), BF16), sds(LANES, BF16), sds(LANES, BF16)],
        compiler_params=_cparams(("parallel", "parallel")),
        name="kv_proj",
    )(h.reshape(batch, seq, d), kv_norm.reshape(1, d), w, wvt, kn)
    vst = vst.reshape(batch, G, LANES, seq // KCHUNK, KCHUNK).transpose(0, 1, 3, 2, 4)
    return kc_raw, vc_raw, ks, vst, kw, vw


def _gelu_tanh(x):
    return 0.5 * x * (1.0 + jnp.tanh(math.sqrt(2.0 / math.pi) * (x + 0.044715 * (x * x * x))))


def _compress_body(kr_ref, vr_ref, pek_ref, pev_ref, w1k_ref, w2k_ref, w1v_ref, w2v_ref, kn_ref,
                   kc_ref, vc_ref, *, ncw, front):
    row = lax.broadcasted_iota(jnp.int32, (ncw, LANES), 0)
    row_lane = lax.broadcasted_iota(jnp.int32, (ncw, LANES), 1)
    lane = lax.broadcasted_iota(jnp.int32, (front, LANES), 1)

    def mlp(r_ref, pe_ref, w1_ref, w2_ref):
        r = r_ref[0, 0]
        a = _dot((r + pe_ref[0:1, :]).astype(BF16), w1_ref[0])
        b = _dot((r + pe_ref[1:2, :]).astype(BF16), w1_ref[1])
        hid = a + pltpu.roll(b, ncw - 1, axis=0)
        out = _dot(_gelu_tanh(hid).astype(BF16), w2_ref[...])
        return jnp.where(row < ncw - 1, out, 0.0)

    kc = _rms(mlp(kr_ref, pek_ref, w1k_ref, w2k_ref), kn_ref[0:1, :], HEAD_DIM)
    vc = mlp(vr_ref, pev_ref, w1v_ref, w2v_ref)
    kc_ref[0, 0, 0:front, :] = jnp.where(lane == FLAG_LANE, 1.0, 0.0)
    vc_ref[0, 0, 0:front, :] = jnp.zeros((front, LANES), F32)
    kc_ref[0, 0, front:front + ncw, :] = kc
    vc_ref[0, 0, front:front + ncw, :] = vc + jnp.where(row_lane == FLAG_LANE, 1.0, 0.0)


def _compress(kc_raw, vc_raw, pe_k, pe_v, w1_k, w2_k, w1_v, w2_v, k_norm):
    batch, G, seq, _ = kc_raw.shape
    ncw = seq // CMP_STRIDE
    front = ncw - QT // CMP_STRIDE
    tok = CMP_STRIDE * LANES
    hid = w1_k.shape[1]

    def prep_w1(w1):
        w = w1.reshape(CMP_BLOCK, HEAD_DIM, hid)
        w = jnp.pad(w, ((0, 0), (0, LANES - HEAD_DIM), (0, 0)))
        return w.reshape(2, tok, hid).astype(BF16)

    def prep_pe(pe):
        return _pad_lanes(pe).reshape(2, tok)

    def prep_w2(w2):
        return _pad_lanes(w2).astype(BF16)

    rspec = pl.BlockSpec((1, 1, ncw, tok), lambda b, g: (b, g, 0, 0))
    ospec = pl.BlockSpec((1, 1, front + ncw, LANES), lambda b, g: (b, g, 0, 0))
    osds = jax.ShapeDtypeStruct((batch, G, front + ncw, LANES), F32)
    return pl.pallas_call(
        functools.partial(_compress_body, ncw=ncw, front=front),
        grid=(batch, G),
        in_specs=[rspec, rspec, _const_spec((2, tok)), _const_spec((2, tok)),
                  _const_spec((2, tok, hid)), _const_spec((hid, LANES)),
                  _const_spec((2, tok, hid)), _const_spec((hid, LANES)),
                  _const_spec((N_BRANCH, LANES))],
        out_specs=[ospec, ospec],
        out_shape=[osds, osds],
        compiler_params=_cparams(("parallel", "parallel")),
        name="compress",
    )(kc_raw.reshape(batch, G, ncw, tok), vc_raw.reshape(batch, G, ncw, tok),
      prep_pe(pe_k), prep_pe(pe_v), prep_w1(w1_k), prep_w2(w2_k), prep_w1(w1_v), prep_w2(w2_v),
      _pad_lanes(k_norm))


def _qproj_body(x_ref, g_ref, wq_ref, wg_ref, qn_ref, q_ref, gt_ref, *, tm, d_model):
    xn = _rms(x_ref[0], g_ref[...], d_model).astype(BF16)
    lane = lax.broadcasted_iota(jnp.int32, (tm, LANES), 1)
    scale = HEAD_DIM ** -0.5 * LOG2E
    for h in range(N_HEADS):
        q = _rms(_dot(xn, wq_ref[h]), qn_ref[...], HEAD_DIM) * scale
        q_ref[0, h] = jnp.where(lane == FLAG_LANE, NEG, q).astype(BF16)
    gates = 1.0 / (1.0 + jnp.exp(-_dot(xn, wg_ref[...])))
    n_gate = HEADS_PER_GROUP * N_BRANCH
    for g in range(N_KV_GROUPS):
        shifted = gates if g == 0 else pltpu.roll(gates, LANES - g * n_gate, axis=1)
        gt_ref[0, :, g * LANES:(g + 1) * LANES] = shifted


def _q_proj(h, norm_g, w_q, q_norm, *, batch, tm=256):
    rows, d = h.shape
    seq = rows // batch
    G, HG = N_KV_GROUPS, HEADS_PER_GROUP
    nq = N_HEADS * HEAD_DIM
    wq = _pad_head_cols(w_q[:, :nq], N_HEADS).reshape(d, N_HEADS, LANES).transpose(1, 0, 2).astype(BF16)
    n_gates = w_q.shape[1] - nq
    assert n_gates <= LANES
    wg = jnp.pad(w_q[:, nq:], ((0, 0), (0, LANES - n_gates))).astype(BF16)
    return pl.pallas_call(
        functools.partial(_qproj_body, tm=tm, d_model=d),
        grid=(batch, seq // tm),
        in_specs=[
            pl.BlockSpec((1, tm, d), lambda b, i: (b, i, 0)),
            _const_spec((1, d)),
            _const_spec((N_HEADS, d, LANES)),
            _const_spec((d, LANES)),
            _const_spec((1, LANES)),
        ],
        out_specs=[pl.BlockSpec((1, N_HEADS, tm, LANES), lambda b, i: (b, 0, i, 0)),
                   pl.BlockSpec((1, tm, G * LANES), lambda b, i: (b, i, 0))],
        out_shape=[jax.ShapeDtypeStruct((batch, N_HEADS, seq, LANES), BF16),
                   jax.ShapeDtypeStruct((batch, seq, G * LANES), F32)],
        compiler_params=_cparams(("parallel", "parallel")),
        name="q_proj",
    )(h.reshape(batch, seq, d), norm_g.reshape(1, d), wq, wg, _pad_lanes(q_norm.reshape(1, HEAD_DIM)))


def _tables_body(thr_ref, rb_ref, tblt_ref, tblw_ref, cb_ref, *, ncw, c_off):
    h = pl.program_id(0)

    def bias_of(rel):
        v = jnp.full(rel.shape, rb_ref[0, h], F32)
        for k in range(1, N_BUCKETS):
            v = jnp.where(rel >= thr_ref[k], rb_ref[k, h], v)
        return v * LOG2E

    row = lax.broadcasted_iota(jnp.int32, (QT, QT), 0)
    col = lax.broadcasted_iota(jnp.int32, (QT, QT), 1)

    def tile_t(d, carry):
        rel = (col - row) + d * QT
        tblt_ref[0, d] = jnp.where(rel < 0, NEG, bias_of(rel))
        return carry

    lax.fori_loop(0, FAR_TILE + 1, tile_t, 0)
    tblt_ref[0, MASK_TILE] = jnp.full((QT, QT), NEG, F32)

    def tile_w(d, carry):
        rel = (row - col) + d * QT
        tblw_ref[0, d] = jnp.where((rel < 0) | (rel >= WINDOW), NEG, bias_of(rel))
        return carry

    lax.fori_loop(0, WIN_TILES, tile_w, 0)
    tblw_ref[0, WIN_MASK_TILE] = jnp.full((QT, QT), NEG, F32)
    relc = (lax.broadcasted_iota(jnp.int32, (QT, ncw), 0)
            - CMP_STRIDE * lax.broadcasted_iota(jnp.int32, (QT, ncw), 1) + c_off)
    cb_ref[0] = jnp.where(relc < 0, NEG, bias_of(relc))


def _bucket_thresholds(seq):
    n = jnp.arange(seq)
    nf = jnp.maximum(n, 1).astype(jnp.float32)
    large = MAX_EXACT + (jnp.log(nf / MAX_EXACT) / math.log(MAX_DISTANCE / MAX_EXACT)
                         * (N_BUCKETS - MAX_EXACT)).astype(jnp.int32)
    large = jnp.minimum(large, N_BUCKETS - 1)
    bucket = jnp.where(n < MAX_EXACT, n, large)
    return jnp.sum(bucket[None, :] < jnp.arange(N_BUCKETS)[:, None], axis=1).astype(jnp.int32)


def _bias_tables(rel_bias, seq):
    ncw = seq // CMP_STRIDE
    front = ncw - QT // CMP_STRIDE
    c_off = CMP_STRIDE * front - (CMP_BLOCK - 1)
    smem = pl.BlockSpec(memory_space=pltpu.SMEM)
    return pl.pallas_call(
        functools.partial(_tables_body, ncw=ncw, c_off=c_off),
        grid=(N_HEADS,),
        in_specs=[smem, smem],
        out_specs=[pl.BlockSpec((1, N_TILES, QT, QT), lambda h: (h, 0, 0, 0)),
                   pl.BlockSpec((1, WIN_TILES + 1, QT, QT), lambda h: (h, 0, 0, 0)),
                   pl.BlockSpec((1, QT, ncw), lambda h: (h, 0, 0))],
        out_shape=[jax.ShapeDtypeStruct((N_HEADS, N_TILES, QT, QT), F32),
                   jax.ShapeDtypeStruct((N_HEADS, WIN_TILES + 1, QT, QT), F32),
                   jax.ShapeDtypeStruct((N_HEADS, QT, ncw), F32)],
        compiler_params=_cparams(("parallel",)),
        name="bias_tables",
    )(_bucket_thresholds(seq), rel_bias)


def _nsa_body(q_ref, gt_ref, ks_ref, vst_ref, kw_ref, vw_ref, kc_ref, vc_ref, ov_ref, tblt_ref, tblw_ref, cb_ref,
              o_ref, sa_ref, sb_ref, pa_ref, pb_ref, pre_s_ref, pre_p_ref, oc_ref, ow_ref, lhs_ref,
              m_ref, ala_ref, alb_ref, acct_ref, *, ncw):
    HG = HEADS_PER_GROUP
    R = HG * QT

    def softmax_strips(s_ref, p_ref, width, bias_fn):
        for st in range(R // STRIP):
            rows = slice(st * STRIP, (st + 1) * STRIP)
            s = s_ref[rows, :width] + bias_fn((st * STRIP) // QT, (st * STRIP) % QT)
            p_ref[rows, :width] = jnp.exp2(s - jnp.max(s, axis=-1, keepdims=True)).astype(BF16)

    def front_stage(sub, sc_ref, pc_ref, sw_ref, pw_ref):
        qi = pl.program_id(2) * TILES_PER_STEP + sub
        q2 = q_ref[0, :, sub * QT:(sub + 1) * QT, :].reshape(R, LANES)

        c0 = pl.multiple_of(qi * (QT // CMP_STRIDE), SUBLANES)
        kc = kc_ref[0, 0, pl.ds(c0, ncw), :].astype(BF16)
        vo = jnp.concatenate([vc_ref[0, 0, pl.ds(c0, ncw), :], ov_ref[pl.ds(c0, ncw), :]],
                             axis=1).astype(BF16)
        sc_ref[:, :ncw] = _dot_t(q2, kc)
        softmax_strips(sc_ref, pc_ref, ncw, lambda h, r: cb_ref[h, r:r + STRIP, :])
        r = _dot(pc_ref[:, :ncw], vo)
        t_row = qi * QT + (lax.broadcasted_iota(jnp.int32, (R, 1), 0) & (QT - 1))
        r = jnp.where(t_row >= CMP_BLOCK - 1, r / r[:, FLAG_LANE:FLAG_LANE + 1], 0.0)
        oc_ref[sub] = r[:, :LANES]
        imp = r[0:QT, LANES:]
        for h in range(1, HG):
            imp = imp + r[h * QT:(h + 1) * QT, LANES:]

        kst = jnp.maximum(qi - WINDOW // QT, 0)
        w0 = pl.multiple_of(kst * QT, QT)
        ids = []
        for u in range(N_WSUB):
            d = qi - kst - u
            ids.append(jnp.where(d < 0, WIN_MASK_TILE, d))
        sw_ref[:, :WIN_KEYS] = _dot_t(q2, kw_ref[0, 0, pl.ds(w0, WIN_KEYS), :])
        softmax_strips(sw_ref, pw_ref, WIN_KEYS,
                       lambda h, r: jnp.concatenate([tblw_ref[h, t, r:r + STRIP, :] for t in ids], axis=1))
        acc_w = _dot(pw_ref[:, :WIN_KEYS], vw_ref[0, 0, pl.ds(w0, WIN_KEYS), :])
        ow_ref[sub] = acc_w / acc_w[:, FLAG_LANE:FLAG_LANE + 1]

        blk = lax.broadcasted_iota(jnp.int32, (LANES, QT), 0)
        blk_t = (qi * QT + lax.broadcasted_iota(jnp.int32, (LANES, QT), 1)) // SLC_BLOCK
        forced = (blk == 0) | (blk == blk_t) | (blk == blk_t - 1)
        score = jnp.where(forced, REMOVED, jnp.where(blk <= blk_t, imp.T, NEG))
        sel = jnp.where(forced, 1.0, 0.0)
        for _ in range(N_SELECT - N_FORCED):
            top = jnp.max(score, axis=0, keepdims=True)
            first = jnp.min(jnp.where(score == top, blk, LANES), axis=0, keepdims=True)
            hit = blk == first
            sel = jnp.where(hit, 1.0, sel)
            score = jnp.where(hit, REMOVED, score)
        sel_neg = ((sel - 1.0) * BIG).T.astype(BF16)
        lhs_ref[sub] = jnp.concatenate([q2, jnp.concatenate([sel_neg] * HG, axis=0)], axis=1)

    def selected_stage(sub):
        qi = pl.program_id(2) * TILES_PER_STEP + sub
        m_ref[...] = jnp.full((1, R), REMOVED, F32)
        acct_ref[...] = jnp.zeros((LANES, R), F32)
        n_chunks = qi // N_SUB + 1

        def scores(s_ref, kj):
            k0 = pl.multiple_of(kj * KCHUNK, KCHUNK)
            s_ref[:, :R] = _dot_t(ks_ref[0, 0, pl.ds(k0, KCHUNK), :], lhs_ref[sub])

        def accumulate(s_ref, p_ref, al_ref, kj):
            for h in range(HG):
                cols = slice(h * QT, (h + 1) * QT)
                mx = m_ref[:, cols]
                for u in range(N_SUB):
                    keys = slice(u * QT, (u + 1) * QT)
                    d = qi - kj * N_SUB - u
                    tile = jnp.where(d < 0, MASK_TILE, jnp.minimum(d, FAR_TILE))
                    sb = s_ref[keys, cols] + tblt_ref[h, tile]
                    s_ref[keys, cols] = sb
                    mx = jnp.maximum(mx, jnp.max(sb, axis=0, keepdims=True))
                al_ref[:, cols] = jnp.exp2(m_ref[:, cols] - mx)
                m_ref[:, cols] = mx
                p_ref[:, cols] = jnp.exp2(s_ref[:, cols] - mx).astype(BF16)
            acct_ref[...] = al_ref[...] * acct_ref[...] + _dot(vst_ref[0, 0, kj], p_ref[:, :R])

        scores(sa_ref, 0)

        def chunk_pair(j, carry):
            a = 2 * j
            scores(sb_ref, a + 1)
            accumulate(sa_ref, pa_ref, ala_ref, a)
            scores(sa_ref, jnp.minimum(a + 2, n_chunks - 1))
            accumulate(sb_ref, pb_ref, alb_ref, a + 1)
            return carry

        lax.fori_loop(0, n_chunks // 2, chunk_pair, 0)

        @pl.when(n_chunks % 2 == 1)
        def _():
            accumulate(sa_ref, pa_ref, ala_ref, n_chunks - 1)

        acct = acct_ref[...]
        o_st = acct / acct[FLAG_LANE:FLAG_LANE + 1, :]

        tile_rows = slice(sub * QT, (sub + 1) * QT)
        gt = gt_ref[0, tile_rows, :]
        o_c = oc_ref[sub]
        o_w = ow_ref[sub]
        outs = []
        for h in range(HG):
            rows = slice(h * QT, (h + 1) * QT)
            c = h * N_BRANCH
            outs.append(gt[:, c:c + 1] * o_c[rows] + gt[:, c + 1:c + 2] * o_st[:, rows].T
                        + gt[:, c + 2:c + 3] * o_w[rows])
        o_ref[0, tile_rows, :] = jnp.concatenate(outs, axis=1).astype(BF16)

    front_stage(0, sa_ref, pa_ref, sb_ref, pb_ref)
    for sub in range(1, TILES_PER_STEP):
        front_stage(sub, pre_s_ref.at[2 * sub - 2], pre_p_ref.at[2 * sub - 2],
                    pre_s_ref.at[2 * sub - 1], pre_p_ref.at[2 * sub - 1])
    for sub in range(TILES_PER_STEP):
        selected_stage(sub)


def _overlap_table(seq):
    ncw = seq // CMP_STRIDE
    front = ncw - QT // CMP_STRIDE
    nc = (seq - CMP_BLOCK) // CMP_STRIDE + 1
    ns = seq // SLC_BLOCK
    c_start = jnp.arange(nc) * CMP_STRIDE
    s_start = jnp.arange(ns) * SLC_BLOCK
    ov = (jnp.clip(jnp.minimum(c_start[:, None] + CMP_BLOCK, s_start[None, :] + SLC_BLOCK)
                   - jnp.maximum(c_start[:, None], s_start[None, :]), 0) / CMP_STRIDE).astype(F32)
    return jnp.pad(ov, ((front, ncw - nc), (0, LANES - ns)))


def _nsa(q, gates, ks, vst, kw, vw, kc, vc, tblt, tblw, cb):
    batch, _, seq, _ = q.shape
    G, HG = N_KV_GROUPS, HEADS_PER_GROUP
    ncw = seq // CMP_STRIDE
    crow = kc.shape[2]
    ov = _overlap_table(seq)
    rows = HG * QT
    assert rows == KCHUNK
    width = max(ncw, WIN_KEYS, KCHUNK)
    kvspec = lambda lanes: pl.BlockSpec((1, 1, seq, lanes), lambda g, b, i: (b, g, 0, 0))
    cspec = pl.BlockSpec((1, 1, crow, LANES), lambda g, b, i: (b, g, 0, 0))
    per_group = lambda n, w: pl.BlockSpec((HG, n, QT, w), lambda g, b, i: (g, 0, 0, 0),
                                          pipeline_mode=pl.Buffered(1))
    tq = TILES_PER_STEP * QT
    n_pre = 2 * (TILES_PER_STEP - 1)
    return pl.pallas_call(
        functools.partial(_nsa_body, ncw=ncw),
        grid=(G, batch, seq // tq),
        in_specs=[
            pl.BlockSpec((1, HG, tq, LANES), lambda g, b, i: (b, g, i, 0)),
            pl.BlockSpec((1, tq, LANES), lambda g, b, i: (b, i, g)),
            kvspec(2 * LANES),
            pl.BlockSpec((1, 1, seq // KCHUNK, LANES, KCHUNK), lambda g, b, i: (b, g, 0, 0, 0)),
            kvspec(LANES), kvspec(LANES),
            cspec, cspec,
            _const_spec(ov.shape),
            per_group(N_TILES, QT), per_group(WIN_TILES + 1, QT),
            pl.BlockSpec((HG, QT, ncw), lambda g, b, i: (g, 0, 0), pipeline_mode=pl.Buffered(1)),
        ],
        out_specs=pl.BlockSpec((1, tq, HG * LANES), lambda g, b, i: (b, i, g)),
        out_shape=jax.ShapeDtypeStruct((batch, seq, N_HEADS * LANES), BF16),
        scratch_shapes=[pltpu.VMEM((rows, width), F32), pltpu.VMEM((rows, width), F32),
                        pltpu.VMEM((rows, width), BF16), pltpu.VMEM((rows, width), BF16),
                        pltpu.VMEM((n_pre, rows, width), F32), pltpu.VMEM((n_pre, rows, width), BF16),
                        pltpu.VMEM((TILES_PER_STEP, rows, LANES), F32),
                        pltpu.VMEM((TILES_PER_STEP, rows, LANES), F32),
                        pltpu.VMEM((TILES_PER_STEP, rows, 2 * LANES), BF16),
                        pltpu.VMEM((1, rows), F32), pltpu.VMEM((1, rows), F32), pltpu.VMEM((1, rows), F32),
                        pltpu.VMEM((LANES, rows), F32)],
        compiler_params=_cparams(("arbitrary", "arbitrary", "arbitrary")),
        name="nsa",
    )(q, gates, ks, vst, kw, vw, kc, vc, ov, tblt, tblw, cb)


def _oproj_body(x_ref, a_ref, w_ref, o_ref):
    o_ref[...] = x_ref[...] + _dot(a_ref[...], w_ref[...])


def _o_proj(h, attn, w_o, *, tm=512):
    rows, d = h.shape
    k = attn.shape[1]
    w = jnp.pad(w_o.reshape(N_HEADS, HEAD_DIM, d), ((0, 0), (0, LANES - HEAD_DIM), (0, 0)))
    w = w.reshape(k, d).astype(BF16)
    return pl.pallas_call(
        _oproj_body,
        grid=(rows // tm,),
        in_specs=[pl.BlockSpec((tm, d), lambda i: (i, 0)),
                  pl.BlockSpec((tm, k), lambda i: (i, 0)),
                  _const_spec((k, d))],
        out_specs=pl.BlockSpec((tm, d), lambda i: (i, 0)),
        out_shape=jax.ShapeDtypeStruct((rows, d), F32),
        compiler_params=_cparams(("parallel",)),
        name="o_proj",
    )(h, attn, w)


def kernel(x, ffn1_norm, ffn1_w_in, ffn1_w_out, mix_norm, ffn2_norm, ffn2_w_in, ffn2_w_out, conv_w_in, conv_w, conv_w_out, attn_w_q, attn_q_norm, attn_w_o, kv_norm, w_kv, k_norm, cmp_pe_k, cmp_pe_v, cmp_w1_k, cmp_w2_k, cmp_w1_v, cmp_w2_v, rel_bias):
    batch, seq, d = x.shape
    depth = ffn1_norm.shape[0]
    n_a = conv_w_in.shape[0]
    assert seq % KCHUNK == 0 and seq // SLC_BLOCK <= LANES and seq >= WIN_KEYS
    h = x.reshape(batch * seq, d)
    kv = tables = None
    for i in range(depth):
        if i == n_a:
            kc_raw, vc_raw, ks, vst, kw, vw = _kv_proj(h, kv_norm, w_kv, k_norm, batch=batch)
            kc, vc = _compress(kc_raw, vc_raw, cmp_pe_k, cmp_pe_v, cmp_w1_k, cmp_w2_k,
                               cmp_w1_v, cmp_w2_v, k_norm)
            kv = (ks, vst, kw, vw, kc, vc)
            tables = _bias_tables(rel_bias, seq)
        h = _ffn(h, ffn1_norm[i], ffn1_w_in[i], ffn1_w_out[i])
        if i < n_a:
            h = _conv_mixer(h, mix_norm[i], conv_w_in[i], conv_w[i], conv_w_out[i], batch=batch)
        else:
            j = i - n_a
            q, gates = _q_proj(h, mix_norm[i], attn_w_q[j], attn_q_norm[j], batch=batch)
            attn = _nsa(q, gates, *kv, *tables)
            h = _o_proj(h, attn.reshape(batch * seq, -1), attn_w_o[j])
        h = _ffn(h, ffn2_norm[i], ffn2_w_in[i], ffn2_w_out[i])
    return h.reshape(batch, seq, d)
```

```python
import functools
import math

import jax
import jax.numpy as jnp
from jax import lax
from jax.experimental import pallas as pl
from jax.experimental.pallas import tpu as pltpu

F32 = jnp.float32
BF16 = jnp.bfloat16

N_HEADS = 16
N_KV_GROUPS = 4
HEADS_PER_GROUP = N_HEADS // N_KV_GROUPS
HEAD_DIM = 64
N_BRANCH = 3
CMP_BLOCK = 32
CMP_STRIDE = 16
SLC_BLOCK = 64
N_SELECT = 16
N_FORCED = 3
WINDOW = 512
N_BUCKETS = 32
MAX_EXACT = N_BUCKETS // 2
MAX_DISTANCE = 4096
EPS = 1e-6
NEG = -1e30
BIG = 1e30
REMOVED = -3e38
LOG2E = math.log2(math.e)

LANES = 128
SUBLANES = 8
VMEM_LIMIT = 56 * 1024 * 1024

QT = 128
TILES_PER_STEP = 2
KCHUNK = 512
N_SUB = KCHUNK // QT
STRIP = 32
WIN_KEYS = WINDOW + QT
N_WSUB = WIN_KEYS // QT
FAR_TILE = int(math.ceil((MAX_EXACT * (MAX_DISTANCE / MAX_EXACT) ** ((N_BUCKETS - MAX_EXACT - 1) / (N_BUCKETS - MAX_EXACT))
                          + QT) / QT))
MASK_TILE = FAR_TILE + 1
N_TILES = FAR_TILE + 2
WIN_TILES = WINDOW // QT + 1
WIN_MASK_TILE = WIN_TILES
FLAG_LANE = HEAD_DIM


def _cparams(sem):
    return pltpu.CompilerParams(dimension_semantics=sem, vmem_limit_bytes=VMEM_LIMIT)


def _const_spec(shape):
    nd = len(shape)
    return pl.BlockSpec(shape, lambda *_: (0,) * nd, pipeline_mode=pl.Buffered(1))


def _rms(x, g, n):
    ms = jnp.sum(x * x, axis=-1, keepdims=True) * (1.0 / n)
    return x * lax.rsqrt(ms + EPS) * g


def _dot(a, b):
    return jnp.dot(a, b, preferred_element_type=F32)


def _dot_t(a, b):
    return lax.dot_general(a, b, (((1,), (1,)), ((), ())), preferred_element_type=F32)


def _ffn_body(x_ref, g_ref, wg_ref, wu_ref, wo_ref, o_ref, acc_ref, *, n_chunks, d_model):
    x = x_ref[...]
    xn = _rms(x, g_ref[...], d_model).astype(BF16)
    acc_ref[...] = jnp.zeros_like(acc_ref)

    def chunk(c, carry):
        gate = _dot(xn, wg_ref[c])
        up = _dot(xn, wu_ref[c])
        act = (gate * (1.0 / (1.0 + jnp.exp(-gate))) * up).astype(BF16)
        acc_ref[...] += _dot(act, wo_ref[c])
        return carry

    lax.fori_loop(0, n_chunks, chunk, 0)
    o_ref[...] = x + 0.5 * acc_ref[...]


def _ffn(h, norm_g, w_in, w_out, *, tm=1024, ck=256):
    rows, d = h.shape
    dff = w_out.shape[0]
    nch = dff // ck
    wg = w_in[:, :dff].reshape(d, nch, ck).transpose(1, 0, 2).astype(BF16)
    wu = w_in[:, dff:].reshape(d, nch, ck).transpose(1, 0, 2).astype(BF16)
    wo = w_out.reshape(nch, ck, d).astype(BF16)
    return pl.pallas_call(
        functools.partial(_ffn_body, n_chunks=nch, d_model=d),
        grid=(rows // tm,),
        in_specs=[
            pl.BlockSpec((tm, d), lambda i: (i, 0)),
            _const_spec((1, d)),
            _const_spec((nch, d, ck)),
            _const_spec((nch, d, ck)),
            _const_spec((nch, ck, d)),
        ],
        out_specs=pl.BlockSpec((tm, d), lambda i: (i, 0)),
        out_shape=jax.ShapeDtypeStruct((rows, d), F32),
        scratch_shapes=[pltpu.VMEM((tm, d), F32)],
        compiler_params=_cparams(("parallel",)),
        name="ffn",
    )(h, norm_g.reshape(1, d), wg, wu, wo)


def _conv_body(x_ref, g_ref, win_ref, cw_ref, wout_ref, o_ref, ubuf_ref, *, tm, d_model):
    i = pl.program_id(1)
    x = x_ref[0]
    xn = _rms(x, g_ref[...], d_model).astype(BF16)
    b_gate = _dot(xn, win_ref[0])
    c_gate = _dot(xn, win_ref[1])
    v = _dot(xn, win_ref[2])
    u = c_gate * v

    @pl.when(i == 0)
    def _():
        ubuf_ref[0:SUBLANES, :] = jnp.zeros((SUBLANES, d_model), F32)

    ubuf_ref[SUBLANES:SUBLANES + tm, :] = u
    u1 = ubuf_ref[SUBLANES - 1:SUBLANES - 1 + tm, :]
    u2 = ubuf_ref[SUBLANES - 2:SUBLANES - 2 + tm, :]
    y = cw_ref[2:3, :] * u + cw_ref[1:2, :] * u1 + cw_ref[0:1, :] * u2
    ubuf_ref[0:SUBLANES, :] = u[tm - SUBLANES:, :]
    o_ref[0] = x + _dot((b_gate * y).astype(BF16), wout_ref[...])


def _conv_mixer(h, norm_g, w_in, conv_w, w_out, *, batch, tm=512):
    rows, d = h.shape
    seq = rows // batch
    win = w_in.reshape(d, 3, d).transpose(1, 0, 2).astype(BF16)
    out = pl.pallas_call(
        functools.partial(_conv_body, tm=tm, d_model=d),
        grid=(batch, seq // tm),
        in_specs=[
            pl.BlockSpec((1, tm, d), lambda b, i: (b, i, 0)),
            _const_spec((1, d)),
            _const_spec((3, d, d)),
            _const_spec((conv_w.shape[0], d)),
            _const_spec((d, d)),
        ],
        out_specs=pl.BlockSpec((1, tm, d), lambda b, i: (b, i, 0)),
        out_shape=jax.ShapeDtypeStruct((batch, seq, d), F32),
        scratch_shapes=[pltpu.VMEM((tm + SUBLANES, d), F32)],
        compiler_params=_cparams(("arbitrary", "arbitrary")),
        name="conv_mixer",
    )(h.reshape(batch, seq, d), norm_g.reshape(1, d), win, conv_w, w_out.astype(BF16))
    return out.reshape(rows, d)


def _pad_head_cols(w, n_slices):
    d = w.shape[0]
    w = w.reshape(d, n_slices, HEAD_DIM)
    return jnp.pad(w, ((0, 0), (0, 0), (0, LANES - HEAD_DIM))).reshape(d, n_slices * LANES)


def _pad_lanes(v):
    return jnp.pad(v, [(0, 0)] * (v.ndim - 1) + [(0, LANES - HEAD_DIM)])


def _kvproj_body(x_ref, g_ref, w_ref, wvt_ref, kn_ref, kc_ref, vc_ref, ks_ref, vst_ref, kw_ref, vw_ref, *,
                 tm, d_model):
    i = pl.program_id(1)
    xn = _rms(x_ref[0], g_ref[...], d_model).astype(BF16)
    lane = lax.broadcasted_iota(jnp.int32, (tm, LANES), 1)
    ones_col = jnp.where(lane == FLAG_LANE, 1.0, 0.0)
    ones_rows = jnp.where(lax.broadcasted_iota(jnp.int32, (LANES - HEAD_DIM, tm), 0) == 0, 1.0, 0.0)
    blk = (i * tm + lax.broadcasted_iota(jnp.int32, (tm, LANES), 0)) // SLC_BLOCK
    onehot = jnp.where(lane == blk, 1.0, 0.0).astype(BF16)
    G = N_KV_GROUPS
    pairs = [_dot(xn, w_ref[p]) for p in range(w_ref.shape[0])]
    vst = _dot_t(wvt_ref[...], xn)
    for g in range(G):
        def col(br, kv):
            s = (br * 2 + kv) * G + g
            blk2 = pairs[s // 2] if s % 2 == 0 else pltpu.roll(pairs[s // 2], HEAD_DIM, axis=1)
            return jnp.where(lane < HEAD_DIM, blk2, 0.0)
        kc_ref[0, g] = col(0, 0)
        vc_ref[0, g] = col(0, 1)
        ks = _rms(col(1, 0), kn_ref[1:2, :], HEAD_DIM).astype(BF16)
        ks_ref[0, g] = jnp.concatenate([ks, onehot], axis=1)
        vst_ref[0, g] = jnp.concatenate([vst[g * HEAD_DIM:(g + 1) * HEAD_DIM], ones_rows], axis=0).astype(BF16)
        kw_ref[0, g] = _rms(col(2, 0), kn_ref[2:3, :], HEAD_DIM).astype(BF16)
        vw_ref[0, g] = (col(2, 1) + ones_col).astype(BF16)


def _kv_proj(h, kv_norm, w_kv, k_norm, *, batch, tm=512):
    rows, d = h.shape
    seq = rows // batch
    G = N_KV_GROUPS
    n_sl = N_BRANCH * 2 * G
    n_pair = n_sl * HEAD_DIM // LANES
    w = w_kv.reshape(d, n_pair, LANES).transpose(1, 0, 2).astype(BF16)
    v_sel = (1 * 2 + 1) * G * HEAD_DIM
    wvt = w_kv[:, v_sel:v_sel + G * HEAD_DIM].T.astype(BF16)
    kn = _pad_lanes(k_norm)
    sds = lambda lanes, dt: jax.ShapeDtypeStruct((batch, G, seq, lanes), dt)
    ospec = lambda lanes: pl.BlockSpec((1, G, tm, lanes), lambda b, i: (b, 0, i, 0))
    kc_raw, vc_raw, ks, vst, kw, vw = pl.pallas_call(
        functools.partial(_kvproj_body, tm=tm, d_model=d),
        grid=(batch, seq // tm),
        in_specs=[
            pl.BlockSpec((1, tm, d), lambda b, i: (b, i, 0)),
            _const_spec((1, d)),
            _const_spec((n_pair, d, LANES)),
            _const_spec((G * HEAD_DIM, d)),
            _const_spec((N_BRANCH, LANES)),
        ],
        out_specs=[ospec(LANES), ospec(LANES), ospec(2 * LANES),
                   pl.BlockSpec((1, G, LANES, tm), lambda b, i: (b, 0, 0, i)), ospec(LANES), ospec(LANES)],
        out_shape=[sds(LANES, F32), sds(LANES, F32), sds(2 * LANES, BF16),
                   jax.ShapeDtypeStruct((batch, G, LANES, seq), BF16), sds(LANES, BF16), sds(LANES, BF16)],
        compiler_params=_cparams(("parallel", "parallel")),
        name="kv_proj",
    )(h.reshape(batch, seq, d), kv_norm.reshape(1, d), w, wvt, kn)
    vst = vst.reshape(batch, G, LANES, seq // KCHUNK, KCHUNK).transpose(0, 1, 3, 2, 4)
    return kc_raw, vc_raw, ks, vst, kw, vw


def _gelu_tanh(x):
    return 0.5 * x * (1.0 + jnp.tanh(math.sqrt(2.0 / math.pi) * (x + 0.044715 * (x * x * x))))


def _compress_body(kr_ref, vr_ref, pek_ref, pev_ref, w1k_ref, w2k_ref, w1v_ref, w2v_ref, kn_ref,
                   kc_ref, vc_ref, *, ncw, front):
    row = lax.broadcasted_iota(jnp.int32, (ncw, LANES), 0)
    row_lane = lax.broadcasted_iota(jnp.int32, (ncw, LANES), 1)
    lane = lax.broadcasted_iota(jnp.int32, (front, LANES), 1)

    def mlp(r_ref, pe_ref, w1_ref, w2_ref):
        r = r_ref[0, 0]
        a = _dot((r + pe_ref[0:1, :]).astype(BF16), w1_ref[0])
        b = _dot((r + pe_ref[1:2, :]).astype(BF16), w1_ref[1])
        hid = a + pltpu.roll(b, ncw - 1, axis=0)
        out = _dot(_gelu_tanh(hid).astype(BF16), w2_ref[...])
        return jnp.where(row < ncw - 1, out, 0.0)

    kc = _rms(mlp(kr_ref, pek_ref, w1k_ref, w2k_ref), kn_ref[0:1, :], HEAD_DIM)
    vc = mlp(vr_ref, pev_ref, w1v_ref, w2v_ref)
    kc_ref[0, 0, 0:front, :] = jnp.where(lane == FLAG_LANE, 1.0, 0.0)
    vc_ref[0, 0, 0:front, :] = jnp.zeros((front, LANES), F32)
    kc_ref[0, 0, front:front + ncw, :] = kc
    vc_ref[0, 0, front:front + ncw, :] = vc + jnp.where(row_lane == FLAG_LANE, 1.0, 0.0)


def _compress(kc_raw, vc_raw, pe_k, pe_v, w1_k, w2_k, w1_v, w2_v, k_norm):
    batch, G, seq, _ = kc_raw.shape
    ncw = seq // CMP_STRIDE
    front = ncw - QT // CMP_STRIDE
    tok = CMP_STRIDE * LANES
    hid = w1_k.shape[1]

    def prep_w1(w1):
        w = w1.reshape(CMP_BLOCK, HEAD_DIM, hid)
        w = jnp.pad(w, ((0, 0), (0, LANES - HEAD_DIM), (0, 0)))
        return w.reshape(2, tok, hid).astype(BF16)

    def prep_pe(pe):
        return _pad_lanes(pe).reshape(2, tok)

    def prep_w2(w2):
        return _pad_lanes(w2).astype(BF16)

    rspec = pl.BlockSpec((1, 1, ncw, tok), lambda b, g: (b, g, 0, 0))
    ospec = pl.BlockSpec((1, 1, front + ncw, LANES), lambda b, g: (b, g, 0, 0))
    osds = jax.ShapeDtypeStruct((batch, G, front + ncw, LANES), F32)
    return pl.pallas_call(
        functools.partial(_compress_body, ncw=ncw, front=front),
        grid=(batch, G),
        in_specs=[rspec, rspec, _const_spec((2, tok)), _const_spec((2, tok)),
                  _const_spec((2, tok, hid)), _const_spec((hid, LANES)),
                  _const_spec((2, tok, hid)), _const_spec((hid, LANES)),
                  _const_spec((N_BRANCH, LANES))],
        out_specs=[ospec, ospec],
        out_shape=[osds, osds],
        compiler_params=_cparams(("parallel", "parallel")),
        name="compress",
    )(kc_raw.reshape(batch, G, ncw, tok), vc_raw.reshape(batch, G, ncw, tok),
      prep_pe(pe_k), prep_pe(pe_v), prep_w1(w1_k), prep_w2(w2_k), prep_w1(w1_v), prep_w2(w2_v),
      _pad_lanes(k_norm))


def _qproj_body(x_ref, g_ref, wq_ref, wg_ref, qn_ref, q_ref, gt_ref, *, tm, d_model):
    xn = _rms(x_ref[0], g_ref[...], d_model).astype(BF16)
    lane = lax.broadcasted_iota(jnp.int32, (tm, LANES), 1)
    scale = HEAD_DIM ** -0.5 * LOG2E
    for h in range(N_HEADS):
        q = _rms(_dot(xn, wq_ref[h]), qn_ref[...], HEAD_DIM) * scale
        q_ref[0, h] = jnp.where(lane == FLAG_LANE, NEG, q).astype(BF16)
    gates = 1.0 / (1.0 + jnp.exp(-_dot(xn, wg_ref[...])))
    n_gate = HEADS_PER_GROUP * N_BRANCH
    for g in range(N_KV_GROUPS):
        shifted = gates if g == 0 else pltpu.roll(gates, LANES - g * n_gate, axis=1)
        gt_ref[0, :, g * LANES:(g + 1) * LANES] = shifted


def _q_proj(h, norm_g, w_q, q_norm, *, batch, tm=512):
    rows, d = h.shape
    seq = rows // batch
    G, HG = N_KV_GROUPS, HEADS_PER_GROUP
    nq = N_HEADS * HEAD_DIM
    wq = _pad_head_cols(w_q[:, :nq], N_HEADS).reshape(d, N_HEADS, LANES).transpose(1, 0, 2).astype(BF16)
    n_gates = w_q.shape[1] - nq
    assert n_gates <= LANES
    wg = jnp.pad(w_q[:, nq:], ((0, 0), (0, LANES - n_gates))).astype(BF16)
    return pl.pallas_call(
        functools.partial(_qproj_body, tm=tm, d_model=d),
        grid=(batch, seq // tm),
        in_specs=[
            pl.BlockSpec((1, tm, d), lambda b, i: (b, i, 0)),
            _const_spec((1, d)),
            _const_spec((N_HEADS, d, LANES)),
            _const_spec((d, LANES)),
            _const_spec((1, LANES)),
        ],
        out_specs=[pl.BlockSpec((1, N_HEADS, tm, LANES), lambda b, i: (b, 0, i, 0)),
                   pl.BlockSpec((1, tm, G * LANES), lambda b, i: (b, i, 0))],
        out_shape=[jax.ShapeDtypeStruct((batch, N_HEADS, seq, LANES), BF16),
                   jax.ShapeDtypeStruct((batch, seq, G * LANES), F32)],
        compiler_params=_cparams(("parallel", "parallel")),
        name="q_proj",
    )(h.reshape(batch, seq, d), norm_g.reshape(1, d), wq, wg, _pad_lanes(q_norm.reshape(1, HEAD_DIM)))


def _tables_body(thr_ref, rb_ref, tblt_ref, tblw_ref, cb_ref, *, ncw, c_off):
    h = pl.program_id(0)

    def bias_of(rel):
        v = jnp.full(rel.shape, rb_ref[0, h], F32)
        for k in range(1, N_BUCKETS):
            v = jnp.where(rel >= thr_ref[k], rb_ref[k, h], v)
        return v * LOG2E

    row = lax.broadcasted_iota(jnp.int32, (QT, QT), 0)
    col = lax.broadcasted_iota(jnp.int32, (QT, QT), 1)

    def tile_t(d, carry):
        rel = (col - row) + d * QT
        tblt_ref[0, d] = jnp.where(rel < 0, NEG, bias_of(rel))
        return carry

    lax.fori_loop(0, FAR_TILE + 1, tile_t, 0)
    tblt_ref[0, MASK_TILE] = jnp.full((QT, QT), NEG, F32)

    def tile_w(d, carry):
        rel = (row - col) + d * QT
        tblw_ref[0, d] = jnp.where((rel < 0) | (rel >= WINDOW), NEG, bias_of(rel))
        return carry

    lax.fori_loop(0, WIN_TILES, tile_w, 0)
    tblw_ref[0, WIN_MASK_TILE] = jnp.full((QT, QT), NEG, F32)
    relc = (lax.broadcasted_iota(jnp.int32, (QT, ncw), 0)
            - CMP_STRIDE * lax.broadcasted_iota(jnp.int32, (QT, ncw), 1) + c_off)
    cb_ref[0] = jnp.where(relc < 0, NEG, bias_of(relc))


def _bucket_thresholds(seq):
    n = jnp.arange(seq)
    nf = jnp.maximum(n, 1).astype(jnp.float32)
    large = MAX_EXACT + (jnp.log(nf / MAX_EXACT) / math.log(MAX_DISTANCE / MAX_EXACT)
                         * (N_BUCKETS - MAX_EXACT)).astype(jnp.int32)
    large = jnp.minimum(large, N_BUCKETS - 1)
    bucket = jnp.where(n < MAX_EXACT, n, large)
    return jnp.sum(bucket[None, :] < jnp.arange(N_BUCKETS)[:, None], axis=1).astype(jnp.int32)


def _bias_tables(rel_bias, seq):
    ncw = seq // CMP_STRIDE
    front = ncw - QT // CMP_STRIDE
    c_off = CMP_STRIDE * front - (CMP_BLOCK - 1)
    smem = pl.BlockSpec(memory_space=pltpu.SMEM)
    return pl.pallas_call(
        functools.partial(_tables_body, ncw=ncw, c_off=c_off),
        grid=(N_HEADS,),
        in_specs=[smem, smem],
        out_specs=[pl.BlockSpec((1, N_TILES, QT, QT), lambda h: (h, 0, 0, 0)),
                   pl.BlockSpec((1, WIN_TILES + 1, QT, QT), lambda h: (h, 0, 0, 0)),
                   pl.BlockSpec((1, QT, ncw), lambda h: (h, 0, 0))],
        out_shape=[jax.ShapeDtypeStruct((N_HEADS, N_TILES, QT, QT), F32),
                   jax.ShapeDtypeStruct((N_HEADS, WIN_TILES + 1, QT, QT), F32),
                   jax.ShapeDtypeStruct((N_HEADS, QT, ncw), F32)],
        compiler_params=_cparams(("parallel",)),
        name="bias_tables",
    )(_bucket_thresholds(seq), rel_bias)


def _nsa_body(q_ref, gt_ref, ks_ref, vst_ref, kw_ref, vw_ref, kc_ref, vc_ref, ov_ref, tblt_ref, tblw_ref, cb_ref,
              o_ref, sa_ref, sb_ref, pa_ref, pb_ref, pre_s_ref, pre_p_ref, oc_ref, ow_ref, lhs_ref,
              m_ref, ala_ref, alb_ref, acct_ref, *, ncw):
    HG = HEADS_PER_GROUP
    R = HG * QT

    def softmax_strips(s_ref, p_ref, width, bias_fn):
        for st in range(R // STRIP):
            rows = slice(st * STRIP, (st + 1) * STRIP)
            s = s_ref[rows, :width] + bias_fn((st * STRIP) // QT, (st * STRIP) % QT)
            p_ref[rows, :width] = jnp.exp2(s - jnp.max(s, axis=-1, keepdims=True)).astype(BF16)

    def front_stage(sub, sc_ref, pc_ref, sw_ref, pw_ref):
        qi = pl.program_id(2) * TILES_PER_STEP + sub
        q2 = q_ref[0, :, sub * QT:(sub + 1) * QT, :].reshape(R, LANES)

        c0 = pl.multiple_of(qi * (QT // CMP_STRIDE), SUBLANES)
        kc = kc_ref[0, 0, pl.ds(c0, ncw), :].astype(BF16)
        vo = jnp.concatenate([vc_ref[0, 0, pl.ds(c0, ncw), :], ov_ref[pl.ds(c0, ncw), :]],
                             axis=1).astype(BF16)
        sc_ref[:, :ncw] = _dot_t(q2, kc)
        softmax_strips(sc_ref, pc_ref, ncw, lambda h, r: cb_ref[h, r:r + STRIP, :])
        r = _dot(pc_ref[:, :ncw], vo)
        t_row = qi * QT + (lax.broadcasted_iota(jnp.int32, (R, 1), 0) & (QT - 1))
        r = jnp.where(t_row >= CMP_BLOCK - 1, r / r[:, FLAG_LANE:FLAG_LANE + 1], 0.0)
        oc_ref[sub] = r[:, :LANES]
        imp = r[0:QT, LANES:]
        for h in range(1, HG):
            imp = imp + r[h * QT:(h + 1) * QT, LANES:]

        kst = jnp.maximum(qi - WINDOW // QT, 0)
        w0 = pl.multiple_of(kst * QT, QT)
        ids = []
        for u in range(N_WSUB):
            d = qi - kst - u
            ids.append(jnp.where(d < 0, WIN_MASK_TILE, d))
        sw_ref[:, :WIN_KEYS] = _dot_t(q2, kw_ref[0, 0, pl.ds(w0, WIN_KEYS), :])
        softmax_strips(sw_ref, pw_ref, WIN_KEYS,
                       lambda h, r: jnp.concatenate([tblw_ref[h, t, r:r + STRIP, :] for t in ids], axis=1))
        acc_w = _dot(pw_ref[:, :WIN_KEYS], vw_ref[0, 0, pl.ds(w0, WIN_KEYS), :])
        ow_ref[sub] = acc_w / acc_w[:, FLAG_LANE:FLAG_LANE + 1]

        blk = lax.broadcasted_iota(jnp.int32, (LANES, QT), 0)
        blk_t = (qi * QT + lax.broadcasted_iota(jnp.int32, (LANES, QT), 1)) // SLC_BLOCK
        forced = (blk == 0) | (blk == blk_t) | (blk == blk_t - 1)
        score = jnp.where(forced, REMOVED, jnp.where(blk <= blk_t, imp.T, NEG))
        sel = jnp.where(forced, 1.0, 0.0)
        for _ in range(N_SELECT - N_FORCED):
            top = jnp.max(score, axis=0, keepdims=True)
            first = jnp.min(jnp.where(score == top, blk, LANES), axis=0, keepdims=True)
            hit = blk == first
            sel = jnp.where(hit, 1.0, sel)
            score = jnp.where(hit, REMOVED, score)
        sel_neg = ((sel - 1.0) * BIG).T.astype(BF16)
        lhs_ref[sub] = jnp.concatenate([q2, jnp.concatenate([sel_neg] * HG, axis=0)], axis=1)

    def selected_stage(sub):
        qi = pl.program_id(2) * TILES_PER_STEP + sub
        m_ref[...] = jnp.full((1, R), REMOVED, F32)
        acct_ref[...] = jnp.zeros((LANES, R), F32)
        n_chunks = qi // N_SUB + 1

        def scores(s_ref, kj):
            k0 = pl.multiple_of(kj * KCHUNK, KCHUNK)
            s_ref[:, :R] = _dot_t(ks_ref[0, 0, pl.ds(k0, KCHUNK), :], lhs_ref[sub])

        def accumulate(s_ref, p_ref, al_ref, kj):
            for h in range(HG):
                cols = slice(h * QT, (h + 1) * QT)
                mx = m_ref[:, cols]
                for u in range(N_SUB):
                    keys = slice(u * QT, (u + 1) * QT)
                    d = qi - kj * N_SUB - u
                    tile = jnp.where(d < 0, MASK_TILE, jnp.minimum(d, FAR_TILE))
                    sb = s_ref[keys, cols] + tblt_ref[h, tile]
                    s_ref[keys, cols] = sb
                    mx = jnp.maximum(mx, jnp.max(sb, axis=0, keepdims=True))
                al_ref[:, cols] = jnp.exp2(m_ref[:, cols] - mx)
                m_ref[:, cols] = mx
                p_ref[:, cols] = jnp.exp2(s_ref[:, cols] - mx).astype(BF16)
            acct_ref[...] = al_ref[...] * acct_ref[...] + _dot(vst_ref[0, 0, kj], p_ref[:, :R])

        scores(sa_ref, 0)

        def chunk_pair(j, carry):
            a = 2 * j
            scores(sb_ref, a + 1)
            accumulate(sa_ref, pa_ref, ala_ref, a)
            scores(sa_ref, jnp.minimum(a + 2, n_chunks - 1))
            accumulate(sb_ref, pb_ref, alb_ref, a + 1)
            return carry

        lax.fori_loop(0, n_chunks // 2, chunk_pair, 0)

        @pl.when(n_chunks % 2 == 1)
        def _():
            accumulate(sa_ref, pa_ref, ala_ref, n_chunks - 1)

        acct = acct_ref[...]
        o_st = acct / acct[FLAG_LANE:FLAG_LANE + 1, :]

        tile_rows = slice(sub * QT, (sub + 1) * QT)
        gt = gt_ref[0, tile_rows, :]
        o_c = oc_ref[sub]
        o_w = ow_ref[sub]
        outs = []
        for h in range(HG):
            rows = slice(h * QT, (h + 1) * QT)
            c = h * N_BRANCH
            outs.append(gt[:, c:c + 1] * o_c[rows] + gt[:, c + 1:c + 2] * o_st[:, rows].T
                        + gt[:, c + 2:c + 3] * o_w[rows])
        o_ref[0, tile_rows, :] = jnp.concatenate(outs, axis=1).astype(BF16)

    front_stage(0, sa_ref, pa_ref, sb_ref, pb_ref)
    for sub in range(1, TILES_PER_STEP):
        front_stage(sub, pre_s_ref.at[2 * sub - 2], pre_p_ref.at[2 * sub - 2],
                    pre_s_ref.at[2 * sub - 1], pre_p_ref.at[2 * sub - 1])
    for sub in range(TILES_PER_STEP):
        selected_stage(sub)


def _overlap_table(seq):
    ncw = seq // CMP_STRIDE
    front = ncw - QT // CMP_STRIDE
    nc = (seq - CMP_BLOCK) // CMP_STRIDE + 1
    ns = seq // SLC_BLOCK
    c_start = jnp.arange(nc) * CMP_STRIDE
    s_start = jnp.arange(ns) * SLC_BLOCK
    ov = (jnp.clip(jnp.minimum(c_start[:, None] + CMP_BLOCK, s_start[None, :] + SLC_BLOCK)
                   - jnp.maximum(c_start[:, None], s_start[None, :]), 0) / CMP_STRIDE).astype(F32)
    return jnp.pad(ov, ((front, ncw - nc), (0, LANES - ns)))


def _nsa(q, gates, ks, vst, kw, vw, kc, vc, tblt, tblw, cb):
    batch, _, seq, _ = q.shape
    G, HG = N_KV_GROUPS, HEADS_PER_GROUP
    ncw = seq // CMP_STRIDE
    crow = kc.shape[2]
    ov = _overlap_table(seq)
    rows = HG * QT
    assert rows == KCHUNK
    width = max(ncw, WIN_KEYS, KCHUNK)
    kvspec = lambda lanes: pl.BlockSpec((1, 1, seq, lanes), lambda g, b, i: (b, g, 0, 0))
    cspec = pl.BlockSpec((1, 1, crow, LANES), lambda g, b, i: (b, g, 0, 0))
    per_group = lambda n, w: pl.BlockSpec((HG, n, QT, w), lambda g, b, i: (g, 0, 0, 0),
                                          pipeline_mode=pl.Buffered(1))
    tq = TILES_PER_STEP * QT
    n_pre = 2 * (TILES_PER_STEP - 1)
    return pl.pallas_call(
        functools.partial(_nsa_body, ncw=ncw),
        grid=(G, batch, seq // tq),
        in_specs=[
            pl.BlockSpec((1, HG, tq, LANES), lambda g, b, i: (b, g, i, 0)),
            pl.BlockSpec((1, tq, LANES), lambda g, b, i: (b, i, g)),
            kvspec(2 * LANES),
            pl.BlockSpec((1, 1, seq // KCHUNK, LANES, KCHUNK), lambda g, b, i: (b, g, 0, 0, 0)),
            kvspec(LANES), kvspec(LANES),
            cspec, cspec,
            _const_spec(ov.shape),
            per_group(N_TILES, QT), per_group(WIN_TILES + 1, QT),
            pl.BlockSpec((HG, QT, ncw), lambda g, b, i: (g, 0, 0), pipeline_mode=pl.Buffered(1)),
        ],
        out_specs=pl.BlockSpec((1, tq, HG * LANES), lambda g, b, i: (b, i, g)),
        out_shape=jax.ShapeDtypeStruct((batch, seq, N_HEADS * LANES), BF16),
        scratch_shapes=[pltpu.VMEM((rows, width), F32), pltpu.VMEM((rows, width), F32),
                        pltpu.VMEM((rows, width), BF16), pltpu.VMEM((rows, width), BF16),
                        pltpu.VMEM((n_pre, rows, width), F32), pltpu.VMEM((n_pre, rows, width), BF16),
                        pltpu.VMEM((TILES_PER_STEP, rows, LANES), F32),
                        pltpu.VMEM((TILES_PER_STEP, rows, LANES), F32),
                        pltpu.VMEM((TILES_PER_STEP, rows, 2 * LANES), BF16),
                        pltpu.VMEM((1, rows), F32), pltpu.VMEM((1, rows), F32), pltpu.VMEM((1, rows), F32),
                        pltpu.VMEM((LANES, rows), F32)],
        compiler_params=_cparams(("arbitrary", "arbitrary", "arbitrary")),
        name="nsa",
    )(q, gates, ks, vst, kw, vw, kc, vc, ov, tblt, tblw, cb)


def _oproj_body(x_ref, a_ref, w_ref, o_ref):
    o_ref[...] = x_ref[...] + _dot(a_ref[...], w_ref[...])


def _o_proj(h, attn, w_o, *, tm=1024):
    rows, d = h.shape
    k = attn.shape[1]
    w = jnp.pad(w_o.reshape(N_HEADS, HEAD_DIM, d), ((0, 0), (0, LANES - HEAD_DIM), (0, 0)))
    w = w.reshape(k, d).astype(BF16)
    return pl.pallas_call(
        _oproj_body,
        grid=(rows // tm,),
        in_specs=[pl.BlockSpec((tm, d), lambda i: (i, 0)),
                  pl.BlockSpec((tm, k), lambda i: (i, 0)),
                  _const_spec((k, d))],
        out_specs=pl.BlockSpec((tm, d), lambda i: (i, 0)),
        out_shape=jax.ShapeDtypeStruct((rows, d), F32),
        compiler_params=_cparams(("parallel",)),
        name="o_proj",
    )(h, attn, w)


def kernel(x, ffn1_norm, ffn1_w_in, ffn1_w_out, mix_norm, ffn2_norm, ffn2_w_in, ffn2_w_out, conv_w_in, conv_w, conv_w_out, attn_w_q, attn_q_norm, attn_w_o, kv_norm, w_kv, k_norm, cmp_pe_k, cmp_pe_v, cmp_w1_k, cmp_w2_k, cmp_w1_v, cmp_w2_v, rel_bias):
    batch, seq, d = x.shape
    depth = ffn1_norm.shape[0]
    n_a = conv_w_in.shape[0]
    assert seq % KCHUNK == 0 and seq // SLC_BLOCK <= LANES and seq >= WIN_KEYS
    h = x.reshape(batch * seq, d)
    kv = tables = None
    for i in range(depth):
        if i == n_a:
            kc_raw, vc_raw, ks, vst, kw, vw = _kv_proj(h, kv_norm, w_kv, k_norm, batch=batch)
            kc, vc = _compress(kc_raw, vc_raw, cmp_pe_k, cmp_pe_v, cmp_w1_k, cmp_w2_k,
                               cmp_w1_v, cmp_w2_v, k_norm)
            kv = (ks, vst, kw, vw, kc, vc)
            tables = _bias_tables(rel_bias, seq)
        h = _ffn(h, ffn1_norm[i], ffn1_w_in[i], ffn1_w_out[i])
        if i < n_a:
            h = _conv_mixer(h, mix_norm[i], conv_w_in[i], conv_w[i], conv_w_out[i], batch=batch)
        else:
            j = i - n_a
            q, gates = _q_proj(h, mix_norm[i], attn_w_q[j], attn_q_norm[j], batch=batch)
            attn = _nsa(q, gates, *kv, *tables)
            h = _o_proj(h, attn.reshape(batch * seq, -1), attn_w_o[j])
        h = _ffn(h, ffn2_norm[i], ffn2_w_in[i], ffn2_w_out[i])
    return h.reshape(batch, seq, d)
```

```python
import functools
import math

import jax
import jax.numpy as jnp
from jax import lax
from jax.experimental import pallas as pl
from jax.experimental.pallas import tpu as pltpu

F32 = jnp.float32
BF16 = jnp.bfloat16

N_HEADS = 16
N_KV_GROUPS = 4
HEADS_PER_GROUP = N_HEADS // N_KV_GROUPS
HEAD_DIM = 64
N_BRANCH = 3
CMP_BLOCK = 32
CMP_STRIDE = 16
SLC_BLOCK = 64
N_SELECT = 16
N_FORCED = 3
WINDOW = 512
N_BUCKETS = 32
MAX_EXACT = N_BUCKETS // 2
MAX_DISTANCE = 4096
EPS = 1e-6
NEG = -1e30
BIG = 1e30
REMOVED = -3e38
LOG2E = math.log2(math.e)

LANES = 128
SUBLANES = 8
VMEM_LIMIT = 56 * 1024 * 1024

QT = 128
TILES_PER_STEP = 2
KCHUNK = 512
N_SUB = KCHUNK // QT
STRIP = 32
WIN_KEYS = WINDOW + QT
N_WSUB = WIN_KEYS // QT
FAR_TILE = int(math.ceil((MAX_EXACT * (MAX_DISTANCE / MAX_EXACT) ** ((N_BUCKETS - MAX_EXACT - 1) / (N_BUCKETS - MAX_EXACT))
                          + QT) / QT))
MASK_TILE = FAR_TILE + 1
N_TILES = FAR_TILE + 2
WIN_TILES = WINDOW // QT + 1
WIN_MASK_TILE = WIN_TILES
FLAG_LANE = HEAD_DIM


def _cparams(sem):
    return pltpu.CompilerParams(dimension_semantics=sem, vmem_limit_bytes=VMEM_LIMIT)


def _const_spec(shape):
    nd = len(shape)
    return pl.BlockSpec(shape, lambda *_: (0,) * nd, pipeline_mode=pl.Buffered(1))


def _rms(x, g, n):
    ms = jnp.sum(x * x, axis=-1, keepdims=True) * (1.0 / n)
    return x * lax.rsqrt(ms + EPS) * g


def _dot(a, b):
    return jnp.dot(a, b, preferred_element_type=F32)


def _dot_t(a, b):
    return lax.dot_general(a, b, (((1,), (1,)), ((), ())), preferred_element_type=F32)


def _ffn_body(x_ref, g_ref, wg_ref, wu_ref, wo_ref, o_ref, acc_ref, *, n_chunks, d_model):
    x = x_ref[...]
    xn = _rms(x, g_ref[...], d_model).astype(BF16)
    acc_ref[...] = jnp.zeros_like(acc_ref)

    def chunk(c, carry):
        gate = _dot(xn, wg_ref[c])
        up = _dot(xn, wu_ref[c])
        act = (gate * (1.0 / (1.0 + jnp.exp(-gate))) * up).astype(BF16)
        acc_ref[...] += _dot(act, wo_ref[c])
        return carry

    lax.fori_loop(0, n_chunks, chunk, 0)
    o_ref[...] = x + 0.5 * acc_ref[...]


def _ffn(h, norm_g, w_in, w_out, *, tm=1024, ck=256):
    rows, d = h.shape
    dff = w_out.shape[0]
    nch = dff // ck
    wg = w_in[:, :dff].reshape(d, nch, ck).transpose(1, 0, 2).astype(BF16)
    wu = w_in[:, dff:].reshape(d, nch, ck).transpose(1, 0, 2).astype(BF16)
    wo = w_out.reshape(nch, ck, d).astype(BF16)
    return pl.pallas_call(
        functools.partial(_ffn_body, n_chunks=nch, d_model=d),
        grid=(rows // tm,),
        in_specs=[
            pl.BlockSpec((tm, d), lambda i: (i, 0)),
            _const_spec((1, d)),
            _const_spec((nch, d, ck)),
            _const_spec((nch, d, ck)),
            _const_spec((nch, ck, d)),
        ],
        out_specs=pl.BlockSpec((tm, d), lambda i: (i, 0)),
        out_shape=jax.ShapeDtypeStruct((rows, d), F32),
        scratch_shapes=[pltpu.VMEM((tm, d), F32)],
        compiler_params=_cparams(("parallel",)),
        name="ffn",
    )(h, norm_g.reshape(1, d), wg, wu, wo)


def _conv_body(x_ref, g_ref, win_ref, cw_ref, wout_ref, o_ref, ubuf_ref, *, tm, d_model):
    i = pl.program_id(1)
    x = x_ref[0]
    xn = _rms(x, g_ref[...], d_model).astype(BF16)
    b_gate = _dot(xn, win_ref[0])
    c_gate = _dot(xn, win_ref[1])
    v = _dot(xn, win_ref[2])
    u = c_gate * v

    @pl.when(i == 0)
    def _():
        ubuf_ref[0:SUBLANES, :] = jnp.zeros((SUBLANES, d_model), F32)

    ubuf_ref[SUBLANES:SUBLANES + tm, :] = u
    u1 = ubuf_ref[SUBLANES - 1:SUBLANES - 1 + tm, :]
    u2 = ubuf_ref[SUBLANES - 2:SUBLANES - 2 + tm, :]
    y = cw_ref[2:3, :] * u + cw_ref[1:2, :] * u1 + cw_ref[0:1, :] * u2
    ubuf_ref[0:SUBLANES, :] = u[tm - SUBLANES:, :]
    o_ref[0] = x + _dot((b_gate * y).astype(BF16), wout_ref[...])


def _conv_mixer(h, norm_g, w_in, conv_w, w_out, *, batch, tm=512):
    rows, d = h.shape
    seq = rows // batch
    win = w_in.reshape(d, 3, d).transpose(1, 0, 2).astype(BF16)
    out = pl.pallas_call(
        functools.partial(_conv_body, tm=tm, d_model=d),
        grid=(batch, seq // tm),
        in_specs=[
            pl.BlockSpec((1, tm, d), lambda b, i: (b, i, 0)),
            _const_spec((1, d)),
            _const_spec((3, d, d)),
            _const_spec((conv_w.shape[0], d)),
            _const_spec((d, d)),
        ],
        out_specs=pl.BlockSpec((1, tm, d), lambda b, i: (b, i, 0)),
        out_shape=jax.ShapeDtypeStruct((batch, seq, d), F32),
        scratch_shapes=[pltpu.VMEM((tm + SUBLANES, d), F32)],
        compiler_params=_cparams(("arbitrary", "arbitrary")),
        name="conv_mixer",
    )(h.reshape(batch, seq, d), norm_g.reshape(1, d), win, conv_w, w_out.astype(BF16))
    return out.reshape(rows, d)


def _pad_head_cols(w, n_slices):
    d = w.shape[0]
    w = w.reshape(d, n_slices, HEAD_DIM)
    return jnp.pad(w, ((0, 0), (0, 0), (0, LANES - HEAD_DIM))).reshape(d, n_slices * LANES)


def _pad_lanes(v):
    return jnp.pad(v, [(0, 0)] * (v.ndim - 1) + [(0, LANES - HEAD_DIM)])


def _kvproj_body(x_ref, g_ref, w_ref, wvt_ref, kn_ref, kc_ref, vc_ref, ks_ref, vst_ref, kw_ref, vw_ref,
                 fold_ref, *, tm, d_model):
    i = pl.program_id(1)
    xn = _rms(x_ref[0], g_ref[...], d_model).astype(BF16)
    lane = lax.broadcasted_iota(jnp.int32, (tm, LANES), 1)
    ones_col = jnp.where(lane == FLAG_LANE, 1.0, 0.0)
    ones_rows = jnp.where(lax.broadcasted_iota(jnp.int32, (LANES - HEAD_DIM, tm), 0) == 0, 1.0, 0.0)
    blk = (i * tm + lax.broadcasted_iota(jnp.int32, (tm, LANES), 0)) // SLC_BLOCK
    onehot = jnp.where(lane == blk, 1.0, 0.0).astype(BF16)
    G = N_KV_GROUPS
    pairs = [_dot(xn, w_ref[p]) for p in range(w_ref.shape[0])]
    vst = _dot_t(wvt_ref[...], xn)
    for g in range(G):
        def col(br, kv):
            s = (br * 2 + kv) * G + g
            blk2 = pairs[s // 2] if s % 2 == 0 else pltpu.roll(pairs[s // 2], HEAD_DIM, axis=1)
            return jnp.where(lane < HEAD_DIM, blk2, 0.0)
        for kv, out_ref in ((0, kc_ref), (1, vc_ref)):
            stage = fold_ref.at[2 * g + kv]
            stage[...] = col(0, kv)
            for r in range(CMP_STRIDE):
                out_ref[0, g, :, r * LANES:(r + 1) * LANES] = stage[pl.ds(r, tm // CMP_STRIDE, stride=CMP_STRIDE), :]
        ks = _rms(col(1, 0), kn_ref[1:2, :], HEAD_DIM).astype(BF16)
        ks_ref[0, g] = jnp.concatenate([ks, onehot], axis=1)
        vst_ref[0, g] = jnp.concatenate([vst[g * HEAD_DIM:(g + 1) * HEAD_DIM], ones_rows], axis=0).astype(BF16)
        kw_ref[0, g] = _rms(col(2, 0), kn_ref[2:3, :], HEAD_DIM).astype(BF16)
        vw_ref[0, g] = (col(2, 1) + ones_col).astype(BF16)


def _kv_proj(h, kv_norm, w_kv, k_norm, *, batch, tm=512):
    rows, d = h.shape
    seq = rows // batch
    G = N_KV_GROUPS
    n_sl = N_BRANCH * 2 * G
    n_pair = n_sl * HEAD_DIM // LANES
    w = w_kv.reshape(d, n_pair, LANES).transpose(1, 0, 2).astype(BF16)
    v_sel = (1 * 2 + 1) * G * HEAD_DIM
    wvt = w_kv[:, v_sel:v_sel + G * HEAD_DIM].T.astype(BF16)
    kn = _pad_lanes(k_norm)
    sds = lambda lanes, dt: jax.ShapeDtypeStruct((batch, G, seq, lanes), dt)
    ospec = lambda lanes: pl.BlockSpec((1, G, tm, lanes), lambda b, i: (b, 0, i, 0))
    fsds = jax.ShapeDtypeStruct((batch, G, seq // CMP_STRIDE, CMP_STRIDE * LANES), F32)
    fspec = pl.BlockSpec((1, G, tm // CMP_STRIDE, CMP_STRIDE * LANES), lambda b, i: (b, 0, i, 0))
    kc_raw, vc_raw, ks, vst, kw, vw = pl.pallas_call(
        functools.partial(_kvproj_body, tm=tm, d_model=d),
        grid=(batch, seq // tm),
        in_specs=[
            pl.BlockSpec((1, tm, d), lambda b, i: (b, i, 0)),
            _const_spec((1, d)),
            _const_spec((n_pair, d, LANES)),
            _const_spec((G * HEAD_DIM, d)),
            _const_spec((N_BRANCH, LANES)),
        ],
        out_specs=[fspec, fspec, ospec(2 * LANES),
                   pl.BlockSpec((1, G, LANES, tm), lambda b, i: (b, 0, 0, i)), ospec(LANES), ospec(LANES)],
        out_shape=[fsds, fsds, sds(2 * LANES, BF16),
                   jax.ShapeDtypeStruct((batch, G, LANES, seq), BF16), sds(LANES, BF16), sds(LANES, BF16)],
        scratch_shapes=[pltpu.VMEM((2 * G, tm, LANES), F32)],
        compiler_params=_cparams(("parallel", "parallel")),
        name="kv_proj",
    )(h.reshape(batch, seq, d), kv_norm.reshape(1, d), w, wvt, kn)
    vst = vst.reshape(batch, G, LANES, seq // KCHUNK, KCHUNK).transpose(0, 1, 3, 2, 4)
    return kc_raw, vc_raw, ks, vst, kw, vw


def _gelu_tanh(x):
    return 0.5 * x * (1.0 + jnp.tanh(math.sqrt(2.0 / math.pi) * (x + 0.044715 * (x * x * x))))


def _compress_body(kr_ref, vr_ref, pek_ref, pev_ref, w1k_ref, w2k_ref, w1v_ref, w2v_ref, kn_ref,
                   kc_ref, vc_ref, *, ncw, front):
    row = lax.broadcasted_iota(jnp.int32, (ncw, LANES), 0)
    row_lane = lax.broadcasted_iota(jnp.int32, (ncw, LANES), 1)
    lane = lax.broadcasted_iota(jnp.int32, (front, LANES), 1)

    def mlp(r_ref, pe_ref, w1_ref, w2_ref):
        r = r_ref[0, 0]
        a = _dot((r + pe_ref[0:1, :]).astype(BF16), w1_ref[0])
        b = _dot((r + pe_ref[1:2, :]).astype(BF16), w1_ref[1])
        hid = a + pltpu.roll(b, ncw - 1, axis=0)
        out = _dot(_gelu_tanh(hid).astype(BF16), w2_ref[...])
        return jnp.where(row < ncw - 1, out, 0.0)

    kc = _rms(mlp(kr_ref, pek_ref, w1k_ref, w2k_ref), kn_ref[0:1, :], HEAD_DIM)
    vc = mlp(vr_ref, pev_ref, w1v_ref, w2v_ref)
    kc_ref[0, 0, 0:front, :] = jnp.where(lane == FLAG_LANE, 1.0, 0.0)
    vc_ref[0, 0, 0:front, :] = jnp.zeros((front, LANES), F32)
    kc_ref[0, 0, front:front + ncw, :] = kc
    vc_ref[0, 0, front:front + ncw, :] = vc + jnp.where(row_lane == FLAG_LANE, 1.0, 0.0)


def _compress(kc_raw, vc_raw, pe_k, pe_v, w1_k, w2_k, w1_v, w2_v, k_norm):
    batch, G, ncw, tok = kc_raw.shape
    front = ncw - QT // CMP_STRIDE
    hid = w1_k.shape[1]

    def prep_w1(w1):
        w = w1.reshape(CMP_BLOCK, HEAD_DIM, hid)
        w = jnp.pad(w, ((0, 0), (0, LANES - HEAD_DIM), (0, 0)))
        return w.reshape(2, tok, hid).astype(BF16)

    def prep_pe(pe):
        return _pad_lanes(pe).reshape(2, tok)

    def prep_w2(w2):
        return _pad_lanes(w2).astype(BF16)

    rspec = pl.BlockSpec((1, 1, ncw, tok), lambda b, g: (b, g, 0, 0))
    ospec = pl.BlockSpec((1, 1, front + ncw, LANES), lambda b, g: (b, g, 0, 0))
    osds = jax.ShapeDtypeStruct((batch, G, front + ncw, LANES), F32)
    return pl.pallas_call(
        functools.partial(_compress_body, ncw=ncw, front=front),
        grid=(batch, G),
        in_specs=[rspec, rspec, _const_spec((2, tok)), _const_spec((2, tok)),
                  _const_spec((2, tok, hid)), _const_spec((hid, LANES)),
                  _const_spec((2, tok, hid)), _const_spec((hid, LANES)),
                  _const_spec((N_BRANCH, LANES))],
        out_specs=[ospec, ospec],
        out_shape=[osds, osds],
        compiler_params=_cparams(("parallel", "parallel")),
        name="compress",
    )(kc_raw, vc_raw,
      prep_pe(pe_k), prep_pe(pe_v), prep_w1(w1_k), prep_w2(w2_k), prep_w1(w1_v), prep_w2(w2_v),
      _pad_lanes(k_norm))


def _qproj_body(x_ref, g_ref, wq_ref, wg_ref, qn_ref, q_ref, gt_ref, *, tm, d_model):
    xn = _rms(x_ref[0], g_ref[...], d_model).astype(BF16)
    lane = lax.broadcasted_iota(jnp.int32, (tm, LANES), 1)
    scale = HEAD_DIM ** -0.5 * LOG2E
    for h in range(N_HEADS):
        q = _rms(_dot(xn, wq_ref[h]), qn_ref[...], HEAD_DIM) * scale
        q_ref[0, h] = jnp.where(lane == FLAG_LANE, NEG, q).astype(BF16)
    gates = 1.0 / (1.0 + jnp.exp(-_dot(xn, wg_ref[...])))
    n_gate = HEADS_PER_GROUP * N_BRANCH
    for g in range(N_KV_GROUPS):
        shifted = gates if g == 0 else pltpu.roll(gates, LANES - g * n_gate, axis=1)
        gt_ref[0, :, g * LANES:(g + 1) * LANES] = shifted


def _q_proj(h, norm_g, w_q, q_norm, *, batch, tm=256):
    rows, d = h.shape
    seq = rows // batch
    G, HG = N_KV_GROUPS, HEADS_PER_GROUP
    nq = N_HEADS * HEAD_DIM
    wq = _pad_head_cols(w_q[:, :nq], N_HEADS).reshape(d, N_HEADS, LANES).transpose(1, 0, 2).astype(BF16)
    n_gates = w_q.shape[1] - nq
    assert n_gates <= LANES
    wg = jnp.pad(w_q[:, nq:], ((0, 0), (0, LANES - n_gates))).astype(BF16)
    return pl.pallas_call(
        functools.partial(_qproj_body, tm=tm, d_model=d),
        grid=(batch, seq // tm),
        in_specs=[
            pl.BlockSpec((1, tm, d), lambda b, i: (b, i, 0)),
            _const_spec((1, d)),
            _const_spec((N_HEADS, d, LANES)),
            _const_spec((d, LANES)),
            _const_spec((1, LANES)),
        ],
        out_specs=[pl.BlockSpec((1, N_HEADS, tm, LANES), lambda b, i: (b, 0, i, 0)),
                   pl.BlockSpec((1, tm, G * LANES), lambda b, i: (b, i, 0))],
        out_shape=[jax.ShapeDtypeStruct((batch, N_HEADS, seq, LANES), BF16),
                   jax.ShapeDtypeStruct((batch, seq, G * LANES), F32)],
        compiler_params=_cparams(("parallel", "parallel")),
        name="q_proj",
    )(h.reshape(batch, seq, d), norm_g.reshape(1, d), wq, wg, _pad_lanes(q_norm.reshape(1, HEAD_DIM)))


def _tables_body(thr_ref, rb_ref, tblt_ref, tblw_ref, cb_ref, *, ncw, c_off):
    h = pl.program_id(0)

    def bias_of(rel):
        v = jnp.full(rel.shape, rb_ref[0, h], F32)
        for k in range(1, N_BUCKETS):
            v = jnp.where(rel >= thr_ref[k], rb_ref[k, h], v)
        return v * LOG2E

    row = lax.broadcasted_iota(jnp.int32, (QT, QT), 0)
    col = lax.broadcasted_iota(jnp.int32, (QT, QT), 1)

    def tile_t(d, carry):
        rel = (col - row) + d * QT
        tblt_ref[0, d] = jnp.where(rel < 0, NEG, bias_of(rel))
        return carry

    lax.fori_loop(0, FAR_TILE + 1, tile_t, 0)
    tblt_ref[0, MASK_TILE] = jnp.full((QT, QT), NEG, F32)

    def tile_w(d, carry):
        rel = (row - col) + d * QT
        tblw_ref[0, d] = jnp.where((rel < 0) | (rel >= WINDOW), NEG, bias_of(rel))
        return carry

    lax.fori_loop(0, WIN_TILES, tile_w, 0)
    tblw_ref[0, WIN_MASK_TILE] = jnp.full((QT, QT), NEG, F32)
    relc = (lax.broadcasted_iota(jnp.int32, (QT, ncw), 0)
            - CMP_STRIDE * lax.broadcasted_iota(jnp.int32, (QT, ncw), 1) + c_off)
    cb_ref[0] = jnp.where(relc < 0, NEG, bias_of(relc))


def _bucket_thresholds(seq):
    n = jnp.arange(seq)
    nf = jnp.maximum(n, 1).astype(jnp.float32)
    large = MAX_EXACT + (jnp.log(nf / MAX_EXACT) / math.log(MAX_DISTANCE / MAX_EXACT)
                         * (N_BUCKETS - MAX_EXACT)).astype(jnp.int32)
    large = jnp.minimum(large, N_BUCKETS - 1)
    bucket = jnp.where(n < MAX_EXACT, n, large)
    return jnp.sum(bucket[None, :] < jnp.arange(N_BUCKETS)[:, None], axis=1).astype(jnp.int32)


def _bias_tables(rel_bias, seq):
    ncw = seq // CMP_STRIDE
    front = ncw - QT // CMP_STRIDE
    c_off = CMP_STRIDE * front - (CMP_BLOCK - 1)
    smem = pl.BlockSpec(memory_space=pltpu.SMEM)
    return pl.pallas_call(
        functools.partial(_tables_body, ncw=ncw, c_off=c_off),
        grid=(N_HEADS,),
        in_specs=[smem, smem],
        out_specs=[pl.BlockSpec((1, N_TILES, QT, QT), lambda h: (h, 0, 0, 0)),
                   pl.BlockSpec((1, WIN_TILES + 1, QT, QT), lambda h: (h, 0, 0, 0)),
                   pl.BlockSpec((1, QT, ncw), lambda h: (h, 0, 0))],
        out_shape=[jax.ShapeDtypeStruct((N_HEADS, N_TILES, QT, QT), F32),
                   jax.ShapeDtypeStruct((N_HEADS, WIN_TILES + 1, QT, QT), F32),
                   jax.ShapeDtypeStruct((N_HEADS, QT, ncw), F32)],
        compiler_params=_cparams(("parallel",)),
        name="bias_tables",
    )(_bucket_thresholds(seq), rel_bias)


def _nsa_body(q_ref, gt_ref, ks_ref, vst_ref, kw_ref, vw_ref, kc_ref, vc_ref, ov_ref, tblt_ref, tblw_ref, cb_ref,
              o_ref, sa_ref, sb_ref, pa_ref, pb_ref, pre_s_ref, pre_p_ref, oc_ref, ow_ref, lhs_ref,
              m_ref, ala_ref, alb_ref, acct_ref, *, ncw):
    HG = HEADS_PER_GROUP
    R = HG * QT

    def softmax_strips(s_ref, p_ref, width, bias_fn):
        for st in range(R // STRIP):
            rows = slice(st * STRIP, (st + 1) * STRIP)
            s = s_ref[rows, :width] + bias_fn((st * STRIP) // QT, (st * STRIP) % QT)
            p_ref[rows, :width] = jnp.exp2(s - jnp.max(s, axis=-1, keepdims=True)).astype(BF16)

    def front_stage(sub, sc_ref, pc_ref, sw_ref, pw_ref):
        qi = pl.program_id(2) * TILES_PER_STEP + sub
        q2 = q_ref[0, :, sub * QT:(sub + 1) * QT, :].reshape(R, LANES)

        c0 = pl.multiple_of(qi * (QT // CMP_STRIDE), SUBLANES)
        kc = kc_ref[0, 0, pl.ds(c0, ncw), :].astype(BF16)
        vo = jnp.concatenate([vc_ref[0, 0, pl.ds(c0, ncw), :], ov_ref[pl.ds(c0, ncw), :]],
                             axis=1).astype(BF16)
        sc_ref[:, :ncw] = _dot_t(q2, kc)
        softmax_strips(sc_ref, pc_ref, ncw, lambda h, r: cb_ref[h, r:r + STRIP, :])
        r = _dot(pc_ref[:, :ncw], vo)
        t_row = qi * QT + (lax.broadcasted_iota(jnp.int32, (R, 1), 0) & (QT - 1))
        r = jnp.where(t_row >= CMP_BLOCK - 1, r / r[:, FLAG_LANE:FLAG_LANE + 1], 0.0)
        oc_ref[sub] = r[:, :LANES]
        imp = r[0:QT, LANES:]
        for h in range(1, HG):
            imp = imp + r[h * QT:(h + 1) * QT, LANES:]

        kst = jnp.maximum(qi - WINDOW // QT, 0)
        w0 = pl.multiple_of(kst * QT, QT)
        ids = []
        for u in range(N_WSUB):
            d = qi - kst - u
            ids.append(jnp.where(d < 0, WIN_MASK_TILE, d))
        sw_ref[:, :WIN_KEYS] = _dot_t(q2, kw_ref[0, 0, pl.ds(w0, WIN_KEYS), :])
        softmax_strips(sw_ref, pw_ref, WIN_KEYS,
                       lambda h, r: jnp.concatenate([tblw_ref[h, t, r:r + STRIP, :] for t in ids], axis=1))
        acc_w = _dot(pw_ref[:, :WIN_KEYS], vw_ref[0, 0, pl.ds(w0, WIN_KEYS), :])
        ow_ref[sub] = acc_w / acc_w[:, FLAG_LANE:FLAG_LANE + 1]

        blk = lax.broadcasted_iota(jnp.int32, (LANES, QT), 0)
        blk_t = (qi * QT + lax.broadcasted_iota(jnp.int32, (LANES, QT), 1)) // SLC_BLOCK
        forced = (blk == 0) | (blk == blk_t) | (blk == blk_t - 1)
        score = jnp.where(forced, REMOVED, jnp.where(blk <= blk_t, imp.T, NEG))
        sel = jnp.where(forced, 1.0, 0.0)
        for _ in range(N_SELECT - N_FORCED):
            top = jnp.max(score, axis=0, keepdims=True)
            first = jnp.min(jnp.where(score == top, blk, LANES), axis=0, keepdims=True)
            hit = blk == first
            sel = jnp.where(hit, 1.0, sel)
            score = jnp.where(hit, REMOVED, score)
        sel_neg = ((sel - 1.0) * BIG).T.astype(BF16)
        lhs_ref[sub] = jnp.concatenate([q2, jnp.concatenate([sel_neg] * HG, axis=0)], axis=1)

    def selected_stage(sub):
        qi = pl.program_id(2) * TILES_PER_STEP + sub
        m_ref[...] = jnp.full((1, R), REMOVED, F32)
        acct_ref[...] = jnp.zeros((LANES, R), F32)
        n_chunks = qi // N_SUB + 1

        def scores(s_ref, kj):
            k0 = pl.multiple_of(kj * KCHUNK, KCHUNK)
            s_ref[:, :R] = _dot_t(ks_ref[0, 0, pl.ds(k0, KCHUNK), :], lhs_ref[sub])

        def accumulate(s_ref, p_ref, al_ref, kj):
            for h in range(HG):
                cols = slice(h * QT, (h + 1) * QT)
                mx = m_ref[:, cols]
                for u in range(N_SUB):
                    keys = slice(u * QT, (u + 1) * QT)
                    d = qi - kj * N_SUB - u
                    tile = jnp.where(d < 0, MASK_TILE, jnp.minimum(d, FAR_TILE))
                    sb = s_ref[keys, cols] + tblt_ref[h, tile]
                    s_ref[keys, cols] = sb
                    mx = jnp.maximum(mx, jnp.max(sb, axis=0, keepdims=True))
                al_ref[:, cols] = jnp.exp2(m_ref[:, cols] - mx)
                m_ref[:, cols] = mx
                p_ref[:, cols] = jnp.exp2(s_ref[:, cols] - mx).astype(BF16)
            acct_ref[...] = al_ref[...] * acct_ref[...] + _dot(vst_ref[0, 0, kj], p_ref[:, :R])

        scores(sa_ref, 0)

        def chunk_pair(j, carry):
            a = 2 * j
            scores(sb_ref, a + 1)
            accumulate(sa_ref, pa_ref, ala_ref, a)
            scores(sa_ref, jnp.minimum(a + 2, n_chunks - 1))
            accumulate(sb_ref, pb_ref, alb_ref, a + 1)
            return carry

        lax.fori_loop(0, n_chunks // 2, chunk_pair, 0)

        @pl.when(n_chunks % 2 == 1)
        def _():
            accumulate(sa_ref, pa_ref, ala_ref, n_chunks - 1)

        acct = acct_ref[...]
        o_st = acct / acct[FLAG_LANE:FLAG_LANE + 1, :]

        tile_rows = slice(sub * QT, (sub + 1) * QT)
        gt = gt_ref[0, tile_rows, :]
        o_c = oc_ref[sub]
        o_w = ow_ref[sub]
        outs = []
        for h in range(HG):
            rows = slice(h * QT, (h + 1) * QT)
            c = h * N_BRANCH
            outs.append(gt[:, c:c + 1] * o_c[rows] + gt[:, c + 1:c + 2] * o_st[:, rows].T
                        + gt[:, c + 2:c + 3] * o_w[rows])
        o_ref[0, tile_rows, :] = jnp.concatenate(outs, axis=1).astype(BF16)

    front_stage(0, sa_ref, pa_ref, sb_ref, pb_ref)
    for sub in range(1, TILES_PER_STEP):
        front_stage(sub, pre_s_ref.at[2 * sub - 2], pre_p_ref.at[2 * sub - 2],
                    pre_s_ref.at[2 * sub - 1], pre_p_ref.at[2 * sub - 1])
    for sub in range(TILES_PER_STEP):
        selected_stage(sub)


def _overlap_table(seq):
    ncw = seq // CMP_STRIDE
    front = ncw - QT // CMP_STRIDE
    nc = (seq - CMP_BLOCK) // CMP_STRIDE + 1
    ns = seq // SLC_BLOCK
    c_start = jnp.arange(nc) * CMP_STRIDE
    s_start = jnp.arange(ns) * SLC_BLOCK
    ov = (jnp.clip(jnp.minimum(c_start[:, None] + CMP_BLOCK, s_start[None, :] + SLC_BLOCK)
                   - jnp.maximum(c_start[:, None], s_start[None, :]), 0) / CMP_STRIDE).astype(F32)
    return jnp.pad(ov, ((front, ncw - nc), (0, LANES - ns)))


def _nsa(q, gates, ks, vst, kw, vw, kc, vc, tblt, tblw, cb):
    batch, _, seq, _ = q.shape
    G, HG = N_KV_GROUPS, HEADS_PER_GROUP
    ncw = seq // CMP_STRIDE
    crow = kc.shape[2]
    ov = _overlap_table(seq)
    rows = HG * QT
    assert rows == KCHUNK
    width = max(ncw, WIN_KEYS, KCHUNK)
    kvspec = lambda lanes: pl.BlockSpec((1, 1, seq, lanes), lambda g, b, i: (b, g, 0, 0))
    cspec = pl.BlockSpec((1, 1, crow, LANES), lambda g, b, i: (b, g, 0, 0))
    per_group = lambda n, w: pl.BlockSpec((HG, n, QT, w), lambda g, b, i: (g, 0, 0, 0),
                                          pipeline_mode=pl.Buffered(1))
    tq = TILES_PER_STEP * QT
    n_pre = 2 * (TILES_PER_STEP - 1)
    return pl.pallas_call(
        functools.partial(_nsa_body, ncw=ncw),
        grid=(G, batch, seq // tq),
        in_specs=[
            pl.BlockSpec((1, HG, tq, LANES), lambda g, b, i: (b, g, i, 0)),
            pl.BlockSpec((1, tq, LANES), lambda g, b, i: (b, i, g)),
            kvspec(2 * LANES),
            pl.BlockSpec((1, 1, seq // KCHUNK, LANES, KCHUNK), lambda g, b, i: (b, g, 0, 0, 0)),
            kvspec(LANES), kvspec(LANES),
            cspec, cspec,
            _const_spec(ov.shape),
            per_group(N_TILES, QT), per_group(WIN_TILES + 1, QT),
            pl.BlockSpec((HG, QT, ncw), lambda g, b, i: (g, 0, 0), pipeline_mode=pl.Buffered(1)),
        ],
        out_specs=pl.BlockSpec((1, tq, HG * LANES), lambda g, b, i: (b, i, g)),
        out_shape=jax.ShapeDtypeStruct((batch, seq, N_HEADS * LANES), BF16),
        scratch_shapes=[pltpu.VMEM((rows, width), F32), pltpu.VMEM((rows, width), F32),
                        pltpu.VMEM((rows, width), BF16), pltpu.VMEM((rows, width), BF16),
                        pltpu.VMEM((n_pre, rows, width), F32), pltpu.VMEM((n_pre, rows, width), BF16),
                        pltpu.VMEM((TILES_PER_STEP, rows, LANES), F32),
                        pltpu.VMEM((TILES_PER_STEP, rows, LANES), F32),
                        pltpu.VMEM((TILES_PER_STEP, rows, 2 * LANES), BF16),
                        pltpu.VMEM((1, rows), F32), pltpu.VMEM((1, rows), F32), pltpu.VMEM((1, rows), F32),
                        pltpu.VMEM((LANES, rows), F32)],
        compiler_params=_cparams(("arbitrary", "arbitrary", "arbitrary")),
        name="nsa",
    )(q, gates, ks, vst, kw, vw, kc, vc, ov, tblt, tblw, cb)


def _oproj_body(x_ref, a_ref, w_ref, o_ref):
    o_ref[...] = x_ref[...] + _dot(a_ref[...], w_ref[...])


def _o_proj(h, attn, w_o, *, tm=1024):
    rows, d = h.shape
    k = attn.shape[1]
    w = jnp.pad(w_o.reshape(N_HEADS, HEAD_DIM, d), ((0, 0), (0, LANES - HEAD_DIM), (0, 0)))
    w = w.reshape(k, d).astype(BF16)
    return pl.pallas_call(
        _oproj_body,
        grid=(rows // tm,),
        in_specs=[pl.BlockSpec((tm, d), lambda i: (i, 0)),
                  pl.BlockSpec((tm, k), lambda i: (i, 0)),
                  _const_spec((k, d))],
        out_specs=pl.BlockSpec((tm, d), lambda i: (i, 0)),
        out_shape=jax.ShapeDtypeStruct((rows, d), F32),
        compiler_params=_cparams(("parallel",)),
        name="o_proj",
    )(h, attn, w)


def kernel(x, ffn1_norm, ffn1_w_in, ffn1_w_out, mix_norm, ffn2_norm, ffn2_w_in, ffn2_w_out, conv_w_in, conv_w, conv_w_out, attn_w_q, attn_q_norm, attn_w_o, kv_norm, w_kv, k_norm, cmp_pe_k, cmp_pe_v, cmp_w1_k, cmp_w2_k, cmp_w1_v, cmp_w2_v, rel_bias):
    batch, seq, d = x.shape
    depth = ffn1_norm.shape[0]
    n_a = conv_w_in.shape[0]
    assert seq % KCHUNK == 0 and seq // SLC_BLOCK <= LANES and seq >= WIN_KEYS
    h = x.reshape(batch * seq, d)
    kv = tables = None
    for i in range(depth):
        if i == n_a:
            kc_raw, vc_raw, ks, vst, kw, vw = _kv_proj(h, kv_norm, w_kv, k_norm, batch=batch)
            kc, vc = _compress(kc_raw, vc_raw, cmp_pe_k, cmp_pe_v, cmp_w1_k, cmp_w2_k,
                               cmp_w1_v, cmp_w2_v, k_norm)
            kv = (ks, vst, kw, vw, kc, vc)
            tables = _bias_tables(rel_bias, seq)
        h = _ffn(h, ffn1_norm[i], ffn1_w_in[i], ffn1_w_out[i])
        if i < n_a:
            h = _conv_mixer(h, mix_norm[i], conv_w_in[i], conv_w[i], conv_w_out[i], batch=batch)
        else:
            j = i - n_a
            q, gates = _q_proj(h, mix_norm[i], attn_w_q[j], attn_q_norm[j], batch=batch)
            attn = _nsa(q, gates, *kv, *tables)
            h = _o_proj(h, attn.reshape(batch * seq, -1), attn_w_o[j])
        h = _ffn(h, ffn2_norm[i], ffn2_w_in[i], ffn2_w_out[i])
    return h.reshape(batch, seq, d)
```

```python
import functools
import math

import jax
import jax.numpy as jnp
from jax import lax
from jax.experimental import pallas as pl
from jax.experimental.pallas import tpu as pltpu

F32 = jnp.float32
BF16 = jnp.bfloat16

N_HEADS = 16
N_KV_GROUPS = 4
HEADS_PER_GROUP = N_HEADS // N_KV_GROUPS
HEAD_DIM = 64
N_BRANCH = 3
CMP_BLOCK = 32
CMP_STRIDE = 16
SLC_BLOCK = 64
N_SELECT = 16
N_FORCED = 3
WINDOW = 512
N_BUCKETS = 32
MAX_EXACT = N_BUCKETS // 2
MAX_DISTANCE = 4096
EPS = 1e-6
NEG = -1e30
BIG = 1e30
REMOVED = -3e38
LOG2E = math.log2(math.e)

LANES = 128
SUBLANES = 8
VMEM_LIMIT = 56 * 1024 * 1024

QT = 128
TILES_PER_STEP = 2
KCHUNK = 512
N_SUB = KCHUNK // QT
STRIP = 32
WIN_KEYS = WINDOW + QT
N_WSUB = WIN_KEYS // QT
FAR_TILE = int(math.ceil((MAX_EXACT * (MAX_DISTANCE / MAX_EXACT) ** ((N_BUCKETS - MAX_EXACT - 1) / (N_BUCKETS - MAX_EXACT))
                          + QT) / QT))
MASK_TILE = FAR_TILE + 1
N_TILES = FAR_TILE + 2
WIN_TILES = WINDOW // QT + 1
WIN_MASK_TILE = WIN_TILES
FLAG_LANE = HEAD_DIM


def _cparams(sem):
    return pltpu.CompilerParams(dimension_semantics=sem, vmem_limit_bytes=VMEM_LIMIT)


def _const_spec(shape):
    nd = len(shape)
    return pl.BlockSpec(shape, lambda *_: (0,) * nd, pipeline_mode=pl.Buffered(1))


def _rms(x, g, n):
    ms = jnp.sum(x * x, axis=-1, keepdims=True) * (1.0 / n)
    return x * lax.rsqrt(ms + EPS) * g


def _dot(a, b):
    return jnp.dot(a, b, preferred_element_type=F32)


def _dot_t(a, b):
    return lax.dot_general(a, b, (((1,), (1,)), ((), ())), preferred_element_type=F32)


def _ffn_body(x_ref, g_ref, wg_ref, wu_ref, wo_ref, o_ref, acc_ref, *, n_chunks, d_model):
    x = x_ref[...]
    xn = _rms(x, g_ref[...], d_model).astype(BF16)
    acc_ref[...] = jnp.zeros_like(acc_ref)

    def chunk(c, carry):
        gate = _dot(xn, wg_ref[c])
        up = _dot(xn, wu_ref[c])
        act = (gate * (1.0 / (1.0 + jnp.exp(-gate))) * up).astype(BF16)
        acc_ref[...] += _dot(act, wo_ref[c])
        return carry

    lax.fori_loop(0, n_chunks, chunk, 0)
    o_ref[...] = x + 0.5 * acc_ref[...]


def _ffn(h, norm_g, w_in, w_out, *, tm=1024, ck=256):
    rows, d = h.shape
    dff = w_out.shape[0]
    nch = dff // ck
    wg = w_in[:, :dff].reshape(d, nch, ck).transpose(1, 0, 2).astype(BF16)
    wu = w_in[:, dff:].reshape(d, nch, ck).transpose(1, 0, 2).astype(BF16)
    wo = w_out.reshape(nch, ck, d).astype(BF16)
    return pl.pallas_call(
        functools.partial(_ffn_body, n_chunks=nch, d_model=d),
        grid=(rows // tm,),
        in_specs=[
            pl.BlockSpec((tm, d), lambda i: (i, 0)),
            _const_spec((1, d)),
            _const_spec((nch, d, ck)),
            _const_spec((nch, d, ck)),
            _const_spec((nch, ck, d)),
        ],
        out_specs=pl.BlockSpec((tm, d), lambda i: (i, 0)),
        out_shape=jax.ShapeDtypeStruct((rows, d), F32),
        scratch_shapes=[pltpu.VMEM((tm, d), F32)],
        compiler_params=_cparams(("parallel",)),
        name="ffn",
    )(h, norm_g.reshape(1, d), wg, wu, wo)


def _conv_body(x_ref, g_ref, win_ref, cw_ref, wout_ref, o_ref, ubuf_ref, *, tm, d_model):
    i = pl.program_id(1)
    x = x_ref[0]
    xn = _rms(x, g_ref[...], d_model).astype(BF16)
    b_gate = _dot(xn, win_ref[0])
    c_gate = _dot(xn, win_ref[1])
    v = _dot(xn, win_ref[2])
    u = c_gate * v

    @pl.when(i == 0)
    def _():
        ubuf_ref[0:SUBLANES, :] = jnp.zeros((SUBLANES, d_model), F32)

    ubuf_ref[SUBLANES:SUBLANES + tm, :] = u
    u1 = ubuf_ref[SUBLANES - 1:SUBLANES - 1 + tm, :]
    u2 = ubuf_ref[SUBLANES - 2:SUBLANES - 2 + tm, :]
    y = cw_ref[2:3, :] * u + cw_ref[1:2, :] * u1 + cw_ref[0:1, :] * u2
    ubuf_ref[0:SUBLANES, :] = u[tm - SUBLANES:, :]
    o_ref[0] = x + _dot((b_gate * y).astype(BF16), wout_ref[...])


def _conv_mixer(h, norm_g, w_in, conv_w, w_out, *, batch, tm=512):
    rows, d = h.shape
    seq = rows // batch
    win = w_in.reshape(d, 3, d).transpose(1, 0, 2).astype(BF16)
    out = pl.pallas_call(
        functools.partial(_conv_body, tm=tm, d_model=d),
        grid=(batch, seq // tm),
        in_specs=[
            pl.BlockSpec((1, tm, d), lambda b, i: (b, i, 0)),
            _const_spec((1, d)),
            _const_spec((3, d, d)),
            _const_spec((conv_w.shape[0], d)),
            _const_spec((d, d)),
        ],
        out_specs=pl.BlockSpec((1, tm, d), lambda b, i: (b, i, 0)),
        out_shape=jax.ShapeDtypeStruct((batch, seq, d), F32),
        scratch_shapes=[pltpu.VMEM((tm + SUBLANES, d), F32)],
        compiler_params=_cparams(("arbitrary", "arbitrary")),
        name="conv_mixer",
    )(h.reshape(batch, seq, d), norm_g.reshape(1, d), win, conv_w, w_out.astype(BF16))
    return out.reshape(rows, d)


def _pad_head_cols(w, n_slices):
    d = w.shape[0]
    w = w.reshape(d, n_slices, HEAD_DIM)
    return jnp.pad(w, ((0, 0), (0, 0), (0, LANES - HEAD_DIM))).reshape(d, n_slices * LANES)


def _pad_lanes(v):
    return jnp.pad(v, [(0, 0)] * (v.ndim - 1) + [(0, LANES - HEAD_DIM)])


def _kvproj_body(x_ref, g_ref, w_ref, wvt_ref, kn_ref, kc_ref, vc_ref, ks_ref, vst_ref, kw_ref, vw_ref,
                 fold_ref, *, tm, d_model):
    i = pl.program_id(1)
    xn = _rms(x_ref[0], g_ref[...], d_model).astype(BF16)
    lane = lax.broadcasted_iota(jnp.int32, (tm, LANES), 1)
    ones_col = jnp.where(lane == FLAG_LANE, 1.0, 0.0)
    ones_rows = jnp.where(lax.broadcasted_iota(jnp.int32, (LANES - HEAD_DIM, tm), 0) == 0, 1.0, 0.0)
    blk = (i * tm + lax.broadcasted_iota(jnp.int32, (tm, LANES), 0)) // SLC_BLOCK
    onehot = jnp.where(lane == blk, 1.0, 0.0).astype(BF16)
    G = N_KV_GROUPS
    pairs = [_dot(xn, w_ref[p]) for p in range(w_ref.shape[0])]
    vst = _dot_t(wvt_ref[...], xn)
    for g in range(G):
        def col(br, kv):
            s = (br * 2 + kv) * G + g
            blk2 = pairs[s // 2] if s % 2 == 0 else pltpu.roll(pairs[s // 2], HEAD_DIM, axis=1)
            return jnp.where(lane < HEAD_DIM, blk2, 0.0)
        for kv, out_ref in ((0, kc_ref), (1, vc_ref)):
            stage = fold_ref.at[2 * g + kv]
            stage[...] = col(0, kv)
            for r in range(CMP_STRIDE):
                out_ref[0, g, :, r * LANES:(r + 1) * LANES] = stage[pl.ds(r, tm // CMP_STRIDE, stride=CMP_STRIDE), :]
        ks = _rms(col(1, 0), kn_ref[1:2, :], HEAD_DIM).astype(BF16)
        ks_ref[0, g] = jnp.concatenate([ks, onehot], axis=1)
        vst_ref[0, g] = jnp.concatenate([vst[g * HEAD_DIM:(g + 1) * HEAD_DIM], ones_rows], axis=0).astype(BF16)
        kw_ref[0, g] = _rms(col(2, 0), kn_ref[2:3, :], HEAD_DIM).astype(BF16)
        vw_ref[0, g] = (col(2, 1) + ones_col).astype(BF16)


def _kv_proj(h, kv_norm, w_kv, k_norm, *, batch, tm=512):
    rows, d = h.shape
    seq = rows // batch
    G = N_KV_GROUPS
    n_sl = N_BRANCH * 2 * G
    n_pair = n_sl * HEAD_DIM // LANES
    w = w_kv.reshape(d, n_pair, LANES).transpose(1, 0, 2).astype(BF16)
    v_sel = (1 * 2 + 1) * G * HEAD_DIM
    wvt = w_kv[:, v_sel:v_sel + G * HEAD_DIM].T.astype(BF16)
    kn = _pad_lanes(k_norm)
    sds = lambda lanes, dt: jax.ShapeDtypeStruct((batch, G, seq, lanes), dt)
    ospec = lambda lanes: pl.BlockSpec((1, G, tm, lanes), lambda b, i: (b, 0, i, 0))
    fsds = jax.ShapeDtypeStruct((batch, G, seq // CMP_STRIDE, CMP_STRIDE * LANES), F32)
    fspec = pl.BlockSpec((1, G, tm // CMP_STRIDE, CMP_STRIDE * LANES), lambda b, i: (b, 0, i, 0))
    kc_raw, vc_raw, ks, vst, kw, vw = pl.pallas_call(
        functools.partial(_kvproj_body, tm=tm, d_model=d),
        grid=(batch, seq // tm),
        in_specs=[
            pl.BlockSpec((1, tm, d), lambda b, i: (b, i, 0)),
            _const_spec((1, d)),
            _const_spec((n_pair, d, LANES)),
            _const_spec((G * HEAD_DIM, d)),
            _const_spec((N_BRANCH, LANES)),
        ],
        out_specs=[fspec, fspec, ospec(2 * LANES),
                   pl.BlockSpec((1, G, LANES, tm), lambda b, i: (b, 0, 0, i)), ospec(LANES), ospec(LANES)],
        out_shape=[fsds, fsds, sds(2 * LANES, BF16),
                   jax.ShapeDtypeStruct((batch, G, LANES, seq), BF16), sds(LANES, BF16), sds(LANES, BF16)],
        scratch_shapes=[pltpu.VMEM((2 * G, tm, LANES), F32)],
        compiler_params=_cparams(("parallel", "parallel")),
        name="kv_proj",
    )(h.reshape(batch, seq, d), kv_norm.reshape(1, d), w, wvt, kn)
    vst = vst.reshape(batch, G, LANES, seq // KCHUNK, KCHUNK).transpose(0, 1, 3, 2, 4)
    return kc_raw, vc_raw, ks, vst, kw, vw


def _gelu_tanh(x):
    return 0.5 * x * (1.0 + jnp.tanh(math.sqrt(2.0 / math.pi) * (x + 0.044715 * (x * x * x))))


def _compress_body(kr_ref, vr_ref, pek_ref, pev_ref, w1k_ref, w2k_ref, w1v_ref, w2v_ref, kn_ref,
                   kc_ref, vc_ref, *, ncw, front):
    row = lax.broadcasted_iota(jnp.int32, (ncw, LANES), 0)
    row_lane = lax.broadcasted_iota(jnp.int32, (ncw, LANES), 1)
    lane = lax.broadcasted_iota(jnp.int32, (front, LANES), 1)

    def mlp(r_ref, pe_ref, w1_ref, w2_ref):
        r = r_ref[0, 0]
        a = _dot((r + pe_ref[0:1, :]).astype(BF16), w1_ref[0])
        b = _dot((r + pe_ref[1:2, :]).astype(BF16), w1_ref[1])
        hid = a + pltpu.roll(b, ncw - 1, axis=0)
        out = _dot(_gelu_tanh(hid).astype(BF16), w2_ref[...])
        return jnp.where(row < ncw - 1, out, 0.0)

    kc = _rms(mlp(kr_ref, pek_ref, w1k_ref, w2k_ref), kn_ref[0:1, :], HEAD_DIM)
    vc = mlp(vr_ref, pev_ref, w1v_ref, w2v_ref)
    kc_ref[0, 0, 0:front, :] = jnp.where(lane == FLAG_LANE, 1.0, 0.0)
    vc_ref[0, 0, 0:front, :] = jnp.zeros((front, LANES), F32)
    kc_ref[0, 0, front:front + ncw, :] = kc
    vc_ref[0, 0, front:front + ncw, :] = vc + jnp.where(row_lane == FLAG_LANE, 1.0, 0.0)


def _compress(kc_raw, vc_raw, pe_k, pe_v, w1_k, w2_k, w1_v, w2_v, k_norm):
    batch, G, ncw, tok = kc_raw.shape
    front = ncw - QT // CMP_STRIDE
    hid = w1_k.shape[1]

    def prep_w1(w1):
        w = w1.reshape(CMP_BLOCK, HEAD_DIM, hid)
        w = jnp.pad(w, ((0, 0), (0, LANES - HEAD_DIM), (0, 0)))
        return w.reshape(2, tok, hid).astype(BF16)

    def prep_pe(pe):
        return _pad_lanes(pe).reshape(2, tok)

    def prep_w2(w2):
        return _pad_lanes(w2).astype(BF16)

    rspec = pl.BlockSpec((1, 1, ncw, tok), lambda b, g: (b, g, 0, 0))
    ospec = pl.BlockSpec((1, 1, front + ncw, LANES), lambda b, g: (b, g, 0, 0))
    osds = jax.ShapeDtypeStruct((batch, G, front + ncw, LANES), F32)
    return pl.pallas_call(
        functools.partial(_compress_body, ncw=ncw, front=front),
        grid=(batch, G),
        in_specs=[rspec, rspec, _const_spec((2, tok)), _const_spec((2, tok)),
                  _const_spec((2, tok, hid)), _const_spec((hid, LANES)),
                  _const_spec((2, tok, hid)), _const_spec((hid, LANES)),
                  _const_spec((N_BRANCH, LANES))],
        out_specs=[ospec, ospec],
        out_shape=[osds, osds],
        compiler_params=_cparams(("parallel", "parallel")),
        name="compress",
    )(kc_raw, vc_raw,
      prep_pe(pe_k), prep_pe(pe_v), prep_w1(w1_k), prep_w2(w2_k), prep_w1(w1_v), prep_w2(w2_v),
      _pad_lanes(k_norm))


def _qproj_body(x_ref, g_ref, wq_ref, wg_ref, qn_ref, q_ref, gt_ref, *, tm, d_model):
    xn = _rms(x_ref[0], g_ref[...], d_model).astype(BF16)
    lane = lax.broadcasted_iota(jnp.int32, (tm, LANES), 1)
    scale = HEAD_DIM ** -0.5 * LOG2E
    for h in range(N_HEADS):
        q = _rms(_dot(xn, wq_ref[h]), qn_ref[...], HEAD_DIM) * scale
        q_ref[0, h] = jnp.where(lane == FLAG_LANE, NEG, q).astype(BF16)
    gates = 1.0 / (1.0 + jnp.exp(-_dot(xn, wg_ref[...])))
    n_gate = HEADS_PER_GROUP * N_BRANCH
    for g in range(N_KV_GROUPS):
        shifted = gates if g == 0 else pltpu.roll(gates, LANES - g * n_gate, axis=1)
        gt_ref[0, :, g * LANES:(g + 1) * LANES] = shifted


def _q_proj(h, norm_g, w_q, q_norm, *, batch, tm=256):
    rows, d = h.shape
    seq = rows // batch
    G, HG = N_KV_GROUPS, HEADS_PER_GROUP
    nq = N_HEADS * HEAD_DIM
    wq = _pad_head_cols(w_q[:, :nq], N_HEADS).reshape(d, N_HEADS, LANES).transpose(1, 0, 2).astype(BF16)
    n_gates = w_q.shape[1] - nq
    assert n_gates <= LANES
    wg = jnp.pad(w_q[:, nq:], ((0, 0), (0, LANES - n_gates))).astype(BF16)
    return pl.pallas_call(
        functools.partial(_qproj_body, tm=tm, d_model=d),
        grid=(batch, seq // tm),
        in_specs=[
            pl.BlockSpec((1, tm, d), lambda b, i: (b, i, 0)),
            _const_spec((1, d)),
            _const_spec((N_HEADS, d, LANES)),
            _const_spec((d, LANES)),
            _const_spec((1, LANES)),
        ],
        out_specs=[pl.BlockSpec((1, N_HEADS, tm, LANES), lambda b, i: (b, 0, i, 0)),
                   pl.BlockSpec((1, tm, G * LANES), lambda b, i: (b, i, 0))],
        out_shape=[jax.ShapeDtypeStruct((batch, N_HEADS, seq, LANES), BF16),
                   jax.ShapeDtypeStruct((batch, seq, G * LANES), F32)],
        compiler_params=_cparams(("parallel", "parallel")),
        name="q_proj",
    )(h.reshape(batch, seq, d), norm_g.reshape(1, d), wq, wg, _pad_lanes(q_norm.reshape(1, HEAD_DIM)))


def _tables_body(thr_ref, rb_ref, tblt_ref, tblw_ref, cb_ref, *, ncw, c_off):
    h = pl.program_id(0)

    def bias_of(rel):
        v = jnp.full(rel.shape, rb_ref[0, h], F32)
        for k in range(1, N_BUCKETS):
            v = jnp.where(rel >= thr_ref[k], rb_ref[k, h], v)
        return v * LOG2E

    row = lax.broadcasted_iota(jnp.int32, (QT, QT), 0)
    col = lax.broadcasted_iota(jnp.int32, (QT, QT), 1)

    def tile_t(d, carry):
        rel = (col - row) + d * QT
        tblt_ref[0, d] = jnp.where(rel < 0, NEG, bias_of(rel)).astype(BF16)
        return carry

    lax.fori_loop(0, FAR_TILE + 1, tile_t, 0)
    tblt_ref[0, MASK_TILE] = jnp.full((QT, QT), NEG, BF16)

    def tile_w(d, carry):
        rel = (row - col) + d * QT
        tblw_ref[0, d] = jnp.where((rel < 0) | (rel >= WINDOW), NEG, bias_of(rel))
        return carry

    lax.fori_loop(0, WIN_TILES, tile_w, 0)
    tblw_ref[0, WIN_MASK_TILE] = jnp.full((QT, QT), NEG, F32)
    relc = (lax.broadcasted_iota(jnp.int32, (QT, ncw), 0)
            - CMP_STRIDE * lax.broadcasted_iota(jnp.int32, (QT, ncw), 1) + c_off)
    cb_ref[0] = jnp.where(relc < 0, NEG, bias_of(relc))


def _bucket_thresholds(seq):
    n = jnp.arange(seq)
    nf = jnp.maximum(n, 1).astype(jnp.float32)
    large = MAX_EXACT + (jnp.log(nf / MAX_EXACT) / math.log(MAX_DISTANCE / MAX_EXACT)
                         * (N_BUCKETS - MAX_EXACT)).astype(jnp.int32)
    large = jnp.minimum(large, N_BUCKETS - 1)
    bucket = jnp.where(n < MAX_EXACT, n, large)
    return jnp.sum(bucket[None, :] < jnp.arange(N_BUCKETS)[:, None], axis=1).astype(jnp.int32)


def _bias_tables(rel_bias, seq):
    ncw = seq // CMP_STRIDE
    front = ncw - QT // CMP_STRIDE
    c_off = CMP_STRIDE * front - (CMP_BLOCK - 1)
    smem = pl.BlockSpec(memory_space=pltpu.SMEM)
    return pl.pallas_call(
        functools.partial(_tables_body, ncw=ncw, c_off=c_off),
        grid=(N_HEADS,),
        in_specs=[smem, smem],
        out_specs=[pl.BlockSpec((1, N_TILES, QT, QT), lambda h: (h, 0, 0, 0)),
                   pl.BlockSpec((1, WIN_TILES + 1, QT, QT), lambda h: (h, 0, 0, 0)),
                   pl.BlockSpec((1, QT, ncw), lambda h: (h, 0, 0))],
        out_shape=[jax.ShapeDtypeStruct((N_HEADS, N_TILES, QT, QT), BF16),
                   jax.ShapeDtypeStruct((N_HEADS, WIN_TILES + 1, QT, QT), F32),
                   jax.ShapeDtypeStruct((N_HEADS, QT, ncw), F32)],
        compiler_params=_cparams(("parallel",)),
        name="bias_tables",
    )(_bucket_thresholds(seq), rel_bias)


def _nsa_body(q_ref, gt_ref, ks_ref, vst_ref, kw_ref, vw_ref, kc_ref, vc_ref, ov_ref, tblt_ref, tblw_ref, cb_ref,
              o_ref, sa_ref, sb_ref, pa_ref, pb_ref, pre_s_ref, pre_p_ref, oc_ref, ow_ref, lhs_ref,
              m_ref, ala_ref, alb_ref, acct_ref, *, ncw):
    HG = HEADS_PER_GROUP
    R = HG * QT

    def softmax_strips(s_ref, p_ref, width, bias_fn):
        for st in range(R // STRIP):
            rows = slice(st * STRIP, (st + 1) * STRIP)
            s = s_ref[rows, :width] + bias_fn((st * STRIP) // QT, (st * STRIP) % QT)
            p_ref[rows, :width] = jnp.exp2(s - jnp.max(s, axis=-1, keepdims=True)).astype(BF16)

    def front_stage(sub, sc_ref, pc_ref, sw_ref, pw_ref):
        qi = pl.program_id(2) * TILES_PER_STEP + sub
        q2 = q_ref[0, :, sub * QT:(sub + 1) * QT, :].reshape(R, LANES)

        c0 = pl.multiple_of(qi * (QT // CMP_STRIDE), SUBLANES)
        kc = kc_ref[0, 0, pl.ds(c0, ncw), :].astype(BF16)
        vo = jnp.concatenate([vc_ref[0, 0, pl.ds(c0, ncw), :], ov_ref[pl.ds(c0, ncw), :]],
                             axis=1).astype(BF16)
        sc_ref[:, :ncw] = _dot_t(q2, kc)
        softmax_strips(sc_ref, pc_ref, ncw, lambda h, r: cb_ref[h, r:r + STRIP, :])
        r = _dot(pc_ref[:, :ncw], vo)
        t_row = qi * QT + (lax.broadcasted_iota(jnp.int32, (R, 1), 0) & (QT - 1))
        r = jnp.where(t_row >= CMP_BLOCK - 1, r / r[:, FLAG_LANE:FLAG_LANE + 1], 0.0)
        oc_ref[sub] = r[:, :LANES]
        imp = r[0:QT, LANES:]
        for h in range(1, HG):
            imp = imp + r[h * QT:(h + 1) * QT, LANES:]

        kst = jnp.maximum(qi - WINDOW // QT, 0)
        w0 = pl.multiple_of(kst * QT, QT)
        ids = []
        for u in range(N_WSUB):
            d = qi - kst - u
            ids.append(jnp.where(d < 0, WIN_MASK_TILE, d))
        sw_ref[:, :WIN_KEYS] = _dot_t(q2, kw_ref[0, 0, pl.ds(w0, WIN_KEYS), :])
        softmax_strips(sw_ref, pw_ref, WIN_KEYS,
                       lambda h, r: jnp.concatenate([tblw_ref[h, t, r:r + STRIP, :] for t in ids], axis=1))
        acc_w = _dot(pw_ref[:, :WIN_KEYS], vw_ref[0, 0, pl.ds(w0, WIN_KEYS), :])
        ow_ref[sub] = acc_w / acc_w[:, FLAG_LANE:FLAG_LANE + 1]

        blk = lax.broadcasted_iota(jnp.int32, (LANES, QT), 0)
        blk_t = (qi * QT + lax.broadcasted_iota(jnp.int32, (LANES, QT), 1)) // SLC_BLOCK
        forced = (blk == 0) | (blk == blk_t) | (blk == blk_t - 1)
        score = jnp.where(forced, REMOVED, jnp.where(blk <= blk_t, imp.T, NEG))
        sel = jnp.where(forced, 1.0, 0.0)
        for _ in range(N_SELECT - N_FORCED):
            top = jnp.max(score, axis=0, keepdims=True)
            first = jnp.min(jnp.where(score == top, blk, LANES), axis=0, keepdims=True)
            hit = blk == first
            sel = jnp.where(hit, 1.0, sel)
            score = jnp.where(hit, REMOVED, score)
        sel_neg = ((sel - 1.0) * BIG).T.astype(BF16)
        lhs_ref[sub] = jnp.concatenate([q2, jnp.concatenate([sel_neg] * HG, axis=0)], axis=1)

    def selected_stage():
        subs = range(TILES_PER_STEP)
        q0 = pl.program_id(2) * TILES_PER_STEP
        m_ref[...] = jnp.full((TILES_PER_STEP, R), REMOVED, F32)
        acct_ref[...] = jnp.zeros((TILES_PER_STEP, LANES, R), F32)
        n_chunks = q0 // N_SUB + 1

        def scores(s_ref, sub, kj):
            k0 = pl.multiple_of(kj * KCHUNK, KCHUNK)
            s_ref[sub] = _dot_t(ks_ref[0, 0, pl.ds(k0, KCHUNK), :], lhs_ref[sub]).astype(BF16)

        def accumulate(s_ref, p_ref, al_ref, sub, kj):
            row = slice(sub, sub + 1)
            for h in range(HG):
                cols = slice(h * QT, (h + 1) * QT)
                mx = m_ref[row, cols]
                for u in range(N_SUB):
                    keys = slice(u * QT, (u + 1) * QT)
                    d = q0 + sub - kj * N_SUB - u
                    tile = jnp.where(d < 0, MASK_TILE, jnp.minimum(d, FAR_TILE))
                    sb = s_ref[sub, keys, cols] + tblt_ref[h, tile]
                    s_ref[sub, keys, cols] = sb
                    mx = jnp.maximum(mx, jnp.max(sb, axis=0, keepdims=True).astype(F32))
                al_ref[row, cols] = jnp.exp2(m_ref[row, cols] - mx)
                m_ref[row, cols] = mx
                p_ref[sub, :, cols] = jnp.exp2(s_ref[sub, :, cols] - mx.astype(BF16))
            acct_ref[sub] = al_ref[row, :] * acct_ref[sub] + _dot(vst_ref[0, 0, kj], p_ref[sub])

        for sub in subs:
            scores(sa_ref, sub, 0)

        def chunk_pair(j, carry):
            a = 2 * j
            for sub in subs:
                scores(sb_ref, sub, a + 1)
            for sub in subs:
                accumulate(sa_ref, pa_ref, ala_ref, sub, a)
            for sub in subs:
                scores(sa_ref, sub, jnp.minimum(a + 2, n_chunks - 1))
            for sub in subs:
                accumulate(sb_ref, pb_ref, alb_ref, sub, a + 1)
            return carry

        lax.fori_loop(0, n_chunks // 2, chunk_pair, 0)

        @pl.when(n_chunks % 2 == 1)
        def _():
            for sub in subs:
                accumulate(sa_ref, pa_ref, ala_ref, sub, n_chunks - 1)

        for sub in subs:
            acct = acct_ref[sub]
            o_st = acct / acct[FLAG_LANE:FLAG_LANE + 1, :]
            tile_rows = slice(sub * QT, (sub + 1) * QT)
            gt = gt_ref[0, tile_rows, :]
            o_c = oc_ref[sub]
            o_w = ow_ref[sub]
            outs = []
            for h in range(HG):
                rows = slice(h * QT, (h + 1) * QT)
                c = h * N_BRANCH
                outs.append(gt[:, c:c + 1] * o_c[rows] + gt[:, c + 1:c + 2] * o_st[:, rows].T
                            + gt[:, c + 2:c + 3] * o_w[rows])
            o_ref[0, tile_rows, :] = jnp.concatenate(outs, axis=1).astype(BF16)

    for sub in range(TILES_PER_STEP):
        front_stage(sub, pre_s_ref.at[2 * sub], pre_p_ref.at[2 * sub],
                    pre_s_ref.at[2 * sub + 1], pre_p_ref.at[2 * sub + 1])
    selected_stage()


def _overlap_table(seq):
    ncw = seq // CMP_STRIDE
    front = ncw - QT // CMP_STRIDE
    nc = (seq - CMP_BLOCK) // CMP_STRIDE + 1
    ns = seq // SLC_BLOCK
    c_start = jnp.arange(nc) * CMP_STRIDE
    s_start = jnp.arange(ns) * SLC_BLOCK
    ov = (jnp.clip(jnp.minimum(c_start[:, None] + CMP_BLOCK, s_start[None, :] + SLC_BLOCK)
                   - jnp.maximum(c_start[:, None], s_start[None, :]), 0) / CMP_STRIDE).astype(F32)
    return jnp.pad(ov, ((front, ncw - nc), (0, LANES - ns)))


def _nsa(q, gates, ks, vst, kw, vw, kc, vc, tblt, tblw, cb):
    batch, _, seq, _ = q.shape
    G, HG = N_KV_GROUPS, HEADS_PER_GROUP
    ncw = seq // CMP_STRIDE
    crow = kc.shape[2]
    ov = _overlap_table(seq)
    rows = HG * QT
    width = max(ncw, WIN_KEYS)
    kvspec = lambda lanes: pl.BlockSpec((1, 1, seq, lanes), lambda g, b, i: (b, g, 0, 0))
    cspec = pl.BlockSpec((1, 1, crow, LANES), lambda g, b, i: (b, g, 0, 0))
    per_group = lambda n, w: pl.BlockSpec((HG, n, QT, w), lambda g, b, i: (g, 0, 0, 0),
                                          pipeline_mode=pl.Buffered(1))
    tq = TILES_PER_STEP * QT
    n_pre = 2 * TILES_PER_STEP
    assert N_SUB % TILES_PER_STEP == 0
    per_tile = lambda shape, dt: pltpu.VMEM((TILES_PER_STEP,) + shape, dt)
    return pl.pallas_call(
        functools.partial(_nsa_body, ncw=ncw),
        grid=(G, batch, seq // tq),
        in_specs=[
            pl.BlockSpec((1, HG, tq, LANES), lambda g, b, i: (b, g, i, 0)),
            pl.BlockSpec((1, tq, LANES), lambda g, b, i: (b, i, g)),
            kvspec(2 * LANES),
            pl.BlockSpec((1, 1, seq // KCHUNK, LANES, KCHUNK), lambda g, b, i: (b, g, 0, 0, 0)),
            kvspec(LANES), kvspec(LANES),
            cspec, cspec,
            _const_spec(ov.shape),
            per_group(N_TILES, QT), per_group(WIN_TILES + 1, QT),
            pl.BlockSpec((HG, QT, ncw), lambda g, b, i: (g, 0, 0), pipeline_mode=pl.Buffered(1)),
        ],
        out_specs=pl.BlockSpec((1, tq, HG * LANES), lambda g, b, i: (b, i, g)),
        out_shape=jax.ShapeDtypeStruct((batch, seq, N_HEADS * LANES), BF16),
        scratch_shapes=[per_tile((KCHUNK, rows), BF16), per_tile((KCHUNK, rows), BF16),
                        per_tile((KCHUNK, rows), BF16), per_tile((KCHUNK, rows), BF16),
                        pltpu.VMEM((n_pre, rows, width), F32), pltpu.VMEM((n_pre, rows, width), BF16),
                        per_tile((rows, LANES), F32), per_tile((rows, LANES), F32),
                        per_tile((rows, 2 * LANES), BF16),
                        per_tile((rows,), F32), per_tile((rows,), F32), per_tile((rows,), F32),
                        per_tile((LANES, rows), F32)],
        compiler_params=_cparams(("arbitrary", "arbitrary", "arbitrary")),
        name="nsa",
    )(q, gates, ks, vst, kw, vw, kc, vc, ov, tblt, tblw, cb)


def _oproj_body(x_ref, a_ref, w_ref, o_ref):
    o_ref[...] = x_ref[...] + _dot(a_ref[...], w_ref[...])


def _o_proj(h, attn, w_o, *, tm=1024):
    rows, d = h.shape
    k = attn.shape[1]
    w = jnp.pad(w_o.reshape(N_HEADS, HEAD_DIM, d), ((0, 0), (0, LANES - HEAD_DIM), (0, 0)))
    w = w.reshape(k, d).astype(BF16)
    return pl.pallas_call(
        _oproj_body,
        grid=(rows // tm,),
        in_specs=[pl.BlockSpec((tm, d), lambda i: (i, 0)),
                  pl.BlockSpec((tm, k), lambda i: (i, 0)),
                  _const_spec((k, d))],
        out_specs=pl.BlockSpec((tm, d), lambda i: (i, 0)),
        out_shape=jax.ShapeDtypeStruct((rows, d), F32),
        compiler_params=_cparams(("parallel",)),
        name="o_proj",
    )(h, attn, w)


def kernel(x, ffn1_norm, ffn1_w_in, ffn1_w_out, mix_norm, ffn2_norm, ffn2_w_in, ffn2_w_out, conv_w_in, conv_w, conv_w_out, attn_w_q, attn_q_norm, attn_w_o, kv_norm, w_kv, k_norm, cmp_pe_k, cmp_pe_v, cmp_w1_k, cmp_w2_k, cmp_w1_v, cmp_w2_v, rel_bias):
    batch, seq, d = x.shape
    depth = ffn1_norm.shape[0]
    n_a = conv_w_in.shape[0]
    assert seq % KCHUNK == 0 and seq // SLC_BLOCK <= LANES and seq >= WIN_KEYS
    h = x.reshape(batch * seq, d)
    kv = tables = None
    for i in range(depth):
        if i == n_a:
            kc_raw, vc_raw, ks, vst, kw, vw = _kv_proj(h, kv_norm, w_kv, k_norm, batch=batch)
            kc, vc = _compress(kc_raw, vc_raw, cmp_pe_k, cmp_pe_v, cmp_w1_k, cmp_w2_k,
                               cmp_w1_v, cmp_w2_v, k_norm)
            kv = (ks, vst, kw, vw, kc, vc)
            tables = _bias_tables(rel_bias, seq)
        h = _ffn(h, ffn1_norm[i], ffn1_w_in[i], ffn1_w_out[i])
        if i < n_a:
            h = _conv_mixer(h, mix_norm[i], conv_w_in[i], conv_w[i], conv_w_out[i], batch=batch)
        else:
            j = i - n_a
            q, gates = _q_proj(h, mix_norm[i], attn_w_q[j], attn_q_norm[j], batch=batch)
            attn = _nsa(q, gates, *kv, *tables)
            h = _o_proj(h, attn.reshape(batch * seq, -1), attn_w_o[j])
        h = _ffn(h, ffn2_norm[i], ffn2_w_in[i], ffn2_w_out[i])
    return h.reshape(batch, seq, d)
```

```python
import functools
import math

import jax
import jax.numpy as jnp
from jax import lax
from jax.experimental import pallas as pl
from jax.experimental.pallas import tpu as pltpu

F32 = jnp.float32
BF16 = jnp.bfloat16

N_HEADS = 16
N_KV_GROUPS = 4
HEADS_PER_GROUP = N_HEADS // N_KV_GROUPS
HEAD_DIM = 64
N_BRANCH = 3
CMP_BLOCK = 32
CMP_STRIDE = 16
SLC_BLOCK = 64
N_SELECT = 16
N_FORCED = 3
WINDOW = 512
N_BUCKETS = 32
MAX_EXACT = N_BUCKETS // 2
MAX_DISTANCE = 4096
EPS = 1e-6
NEG = -1e30
BIG = 1e30
REMOVED = -3e38
LOG2E = math.log2(math.e)

LANES = 128
SUBLANES = 8
VMEM_LIMIT = 56 * 1024 * 1024

QT = 128
TILES_PER_STEP = 2
KCHUNK = 512
N_SUB = KCHUNK // QT
STRIP = 32
WIN_KEYS = WINDOW + QT
N_WSUB = WIN_KEYS // QT
FAR_TILE = int(math.ceil((MAX_EXACT * (MAX_DISTANCE / MAX_EXACT) ** ((N_BUCKETS - MAX_EXACT - 1) / (N_BUCKETS - MAX_EXACT))
                          + QT) / QT))
MASK_TILE = FAR_TILE + 1
N_TILES = FAR_TILE + 2
WIN_TILES = WINDOW // QT + 1
WIN_MASK_TILE = WIN_TILES
FLAG_LANE = HEAD_DIM


def _cparams(sem):
    return pltpu.CompilerParams(dimension_semantics=sem, vmem_limit_bytes=VMEM_LIMIT)


def _const_spec(shape):
    nd = len(shape)
    return pl.BlockSpec(shape, lambda *_: (0,) * nd, pipeline_mode=pl.Buffered(1))


def _rms(x, g, n):
    ms = jnp.sum(x * x, axis=-1, keepdims=True) * (1.0 / n)
    return x * lax.rsqrt(ms + EPS) * g


def _dot(a, b):
    return jnp.dot(a, b, preferred_element_type=F32)


def _dot_t(a, b):
    return lax.dot_general(a, b, (((1,), (1,)), ((), ())), preferred_element_type=F32)


def _ffn_body(x_ref, g_ref, wg_ref, wu_ref, wo_ref, o_ref, acc_ref, *, n_chunks, d_model):
    x = x_ref[...]
    xn = _rms(x, g_ref[...], d_model).astype(BF16)
    acc_ref[...] = jnp.zeros_like(acc_ref)

    def chunk(c, carry):
        gate = _dot(xn, wg_ref[c])
        up = _dot(xn, wu_ref[c])
        act = (gate * (1.0 / (1.0 + jnp.exp(-gate))) * up).astype(BF16)
        acc_ref[...] += _dot(act, wo_ref[c])
        return carry

    lax.fori_loop(0, n_chunks, chunk, 0)
    o_ref[...] = x + 0.5 * acc_ref[...]


def _ffn(h, norm_g, w_in, w_out, *, tm=1024, ck=256):
    rows, d = h.shape
    dff = w_out.shape[0]
    nch = dff // ck
    wg = w_in[:, :dff].reshape(d, nch, ck).transpose(1, 0, 2).astype(BF16)
    wu = w_in[:, dff:].reshape(d, nch, ck).transpose(1, 0, 2).astype(BF16)
    wo = w_out.reshape(nch, ck, d).astype(BF16)
    return pl.pallas_call(
        functools.partial(_ffn_body, n_chunks=nch, d_model=d),
        grid=(rows // tm,),
        in_specs=[
            pl.BlockSpec((tm, d), lambda i: (i, 0)),
            _const_spec((1, d)),
            _const_spec((nch, d, ck)),
            _const_spec((nch, d, ck)),
            _const_spec((nch, ck, d)),
        ],
        out_specs=pl.BlockSpec((tm, d), lambda i: (i, 0)),
        out_shape=jax.ShapeDtypeStruct((rows, d), F32),
        scratch_shapes=[pltpu.VMEM((tm, d), F32)],
        compiler_params=_cparams(("parallel",)),
        name="ffn",
    )(h, norm_g.reshape(1, d), wg, wu, wo)


def _conv_body(x_ref, g_ref, win_ref, cw_ref, wout_ref, o_ref, ubuf_ref, *, tm, d_model):
    i = pl.program_id(1)
    x = x_ref[0]
    xn = _rms(x, g_ref[...], d_model).astype(BF16)
    b_gate = _dot(xn, win_ref[0])
    c_gate = _dot(xn, win_ref[1])
    v = _dot(xn, win_ref[2])
    u = c_gate * v

    @pl.when(i == 0)
    def _():
        ubuf_ref[0:SUBLANES, :] = jnp.zeros((SUBLANES, d_model), F32)

    ubuf_ref[SUBLANES:SUBLANES + tm, :] = u
    u1 = ubuf_ref[SUBLANES - 1:SUBLANES - 1 + tm, :]
    u2 = ubuf_ref[SUBLANES - 2:SUBLANES - 2 + tm, :]
    y = cw_ref[2:3, :] * u + cw_ref[1:2, :] * u1 + cw_ref[0:1, :] * u2
    ubuf_ref[0:SUBLANES, :] = u[tm - SUBLANES:, :]
    o_ref[0] = x + _dot((b_gate * y).astype(BF16), wout_ref[...])


def _conv_mixer(h, norm_g, w_in, conv_w, w_out, *, batch, tm=512):
    rows, d = h.shape
    seq = rows // batch
    win = w_in.reshape(d, 3, d).transpose(1, 0, 2).astype(BF16)
    out = pl.pallas_call(
        functools.partial(_conv_body, tm=tm, d_model=d),
        grid=(batch, seq // tm),
        in_specs=[
            pl.BlockSpec((1, tm, d), lambda b, i: (b, i, 0)),
            _const_spec((1, d)),
            _const_spec((3, d, d)),
            _const_spec((conv_w.shape[0], d)),
            _const_spec((d, d)),
        ],
        out_specs=pl.BlockSpec((1, tm, d), lambda b, i: (b, i, 0)),
        out_shape=jax.ShapeDtypeStruct((batch, seq, d), F32),
        scratch_shapes=[pltpu.VMEM((tm + SUBLANES, d), F32)],
        compiler_params=_cparams(("arbitrary", "arbitrary")),
        name="conv_mixer",
    )(h.reshape(batch, seq, d), norm_g.reshape(1, d), win, conv_w, w_out.astype(BF16))
    return out.reshape(rows, d)


def _pad_head_cols(w, n_slices):
    d = w.shape[0]
    w = w.reshape(d, n_slices, HEAD_DIM)
    return jnp.pad(w, ((0, 0), (0, 0), (0, LANES - HEAD_DIM))).reshape(d, n_slices * LANES)


def _pad_lanes(v):
    return jnp.pad(v, [(0, 0)] * (v.ndim - 1) + [(0, LANES - HEAD_DIM)])


def _kvproj_body(x_ref, g_ref, w_ref, wvt_ref, kn_ref, kc_ref, vc_ref, ks_ref, vst_ref, kw_ref, vw_ref,
                 fold_ref, *, tm, d_model):
    i = pl.program_id(1)
    xn = _rms(x_ref[0], g_ref[...], d_model).astype(BF16)
    lane = lax.broadcasted_iota(jnp.int32, (tm, LANES), 1)
    ones_col = jnp.where(lane == FLAG_LANE, 1.0, 0.0)
    ones_rows = jnp.where(lax.broadcasted_iota(jnp.int32, (LANES - HEAD_DIM, tm), 0) == 0, 1.0, 0.0)
    blk = (i * tm + lax.broadcasted_iota(jnp.int32, (tm, LANES), 0)) // SLC_BLOCK
    onehot = jnp.where(lane == blk, 1.0, 0.0).astype(BF16)
    G = N_KV_GROUPS
    pairs = [_dot(xn, w_ref[p]) for p in range(w_ref.shape[0])]
    vst = _dot_t(wvt_ref[...], xn)
    for g in range(G):
        def col(br, kv):
            s = (br * 2 + kv) * G + g
            blk2 = pairs[s // 2] if s % 2 == 0 else pltpu.roll(pairs[s // 2], HEAD_DIM, axis=1)
            return jnp.where(lane < HEAD_DIM, blk2, 0.0)
        for kv, out_ref in ((0, kc_ref), (1, vc_ref)):
            stage = fold_ref.at[2 * g + kv]
            stage[...] = col(0, kv)
            for r in range(CMP_STRIDE):
                out_ref[0, g, :, r * LANES:(r + 1) * LANES] = stage[pl.ds(r, tm // CMP_STRIDE, stride=CMP_STRIDE), :]
        ks = _rms(col(1, 0), kn_ref[1:2, :], HEAD_DIM).astype(BF16)
        ks_ref[0, g] = jnp.concatenate([ks, onehot], axis=1)
        vst_ref[0, g] = jnp.concatenate([vst[g * HEAD_DIM:(g + 1) * HEAD_DIM], ones_rows], axis=0).astype(BF16)
        kw_ref[0, g] = _rms(col(2, 0), kn_ref[2:3, :], HEAD_DIM).astype(BF16)
        vw_ref[0, g] = (col(2, 1) + ones_col).astype(BF16)


def _kv_proj(h, kv_norm, w_kv, k_norm, *, batch, tm=512):
    rows, d = h.shape
    seq = rows // batch
    G = N_KV_GROUPS
    n_sl = N_BRANCH * 2 * G
    n_pair = n_sl * HEAD_DIM // LANES
    w = w_kv.reshape(d, n_pair, LANES).transpose(1, 0, 2).astype(BF16)
    v_sel = (1 * 2 + 1) * G * HEAD_DIM
    wvt = w_kv[:, v_sel:v_sel + G * HEAD_DIM].T.astype(BF16)
    kn = _pad_lanes(k_norm)
    sds = lambda lanes, dt: jax.ShapeDtypeStruct((batch, G, seq, lanes), dt)
    ospec = lambda lanes: pl.BlockSpec((1, G, tm, lanes), lambda b, i: (b, 0, i, 0))
    fsds = jax.ShapeDtypeStruct((batch, G, seq // CMP_STRIDE, CMP_STRIDE * LANES), F32)
    fspec = pl.BlockSpec((1, G, tm // CMP_STRIDE, CMP_STRIDE * LANES), lambda b, i: (b, 0, i, 0))
    kc_raw, vc_raw, ks, vst, kw, vw = pl.pallas_call(
        functools.partial(_kvproj_body, tm=tm, d_model=d),
        grid=(batch, seq // tm),
        in_specs=[
            pl.BlockSpec((1, tm, d), lambda b, i: (b, i, 0)),
            _const_spec((1, d)),
            _const_spec((n_pair, d, LANES)),
            _const_spec((G * HEAD_DIM, d)),
            _const_spec((N_BRANCH, LANES)),
        ],
        out_specs=[fspec, fspec, ospec(2 * LANES),
                   pl.BlockSpec((1, G, LANES, tm), lambda b, i: (b, 0, 0, i)), ospec(LANES), ospec(LANES)],
        out_shape=[fsds, fsds, sds(2 * LANES, BF16),
                   jax.ShapeDtypeStruct((batch, G, LANES, seq), BF16), sds(LANES, BF16), sds(LANES, BF16)],
        scratch_shapes=[pltpu.VMEM((2 * G, tm, LANES), F32)],
        compiler_params=_cparams(("parallel", "parallel")),
        name="kv_proj",
    )(h.reshape(batch, seq, d), kv_norm.reshape(1, d), w, wvt, kn)
    vst = vst.reshape(batch, G, LANES, seq // KCHUNK, KCHUNK).transpose(0, 1, 3, 2, 4)
    return kc_raw, vc_raw, ks, vst, kw, vw


def _gelu_tanh(x):
    return 0.5 * x * (1.0 + jnp.tanh(math.sqrt(2.0 / math.pi) * (x + 0.044715 * (x * x * x))))


def _compress_body(kr_ref, vr_ref, pek_ref, pev_ref, w1k_ref, w2k_ref, w1v_ref, w2v_ref, kn_ref,
                   kc_ref, vc_ref, *, ncw, front):
    row = lax.broadcasted_iota(jnp.int32, (ncw, LANES), 0)
    row_lane = lax.broadcasted_iota(jnp.int32, (ncw, LANES), 1)
    lane = lax.broadcasted_iota(jnp.int32, (front, LANES), 1)

    def mlp(r_ref, pe_ref, w1_ref, w2_ref):
        r = r_ref[0, 0]
        a = _dot((r + pe_ref[0:1, :]).astype(BF16), w1_ref[0])
        b = _dot((r + pe_ref[1:2, :]).astype(BF16), w1_ref[1])
        hid = a + pltpu.roll(b, ncw - 1, axis=0)
        out = _dot(_gelu_tanh(hid).astype(BF16), w2_ref[...])
        return jnp.where(row < ncw - 1, out, 0.0)

    kc = _rms(mlp(kr_ref, pek_ref, w1k_ref, w2k_ref), kn_ref[0:1, :], HEAD_DIM)
    vc = mlp(vr_ref, pev_ref, w1v_ref, w2v_ref)
    kc_ref[0, 0, 0:front, :] = jnp.where(lane == FLAG_LANE, 1.0, 0.0)
    vc_ref[0, 0, 0:front, :] = jnp.zeros((front, LANES), F32)
    kc_ref[0, 0, front:front + ncw, :] = kc
    vc_ref[0, 0, front:front + ncw, :] = vc + jnp.where(row_lane == FLAG_LANE, 1.0, 0.0)


def _compress(kc_raw, vc_raw, pe_k, pe_v, w1_k, w2_k, w1_v, w2_v, k_norm):
    batch, G, ncw, tok = kc_raw.shape
    front = ncw - QT // CMP_STRIDE
    hid = w1_k.shape[1]

    def prep_w1(w1):
        w = w1.reshape(CMP_BLOCK, HEAD_DIM, hid)
        w = jnp.pad(w, ((0, 0), (0, LANES - HEAD_DIM), (0, 0)))
        return w.reshape(2, tok, hid).astype(BF16)

    def prep_pe(pe):
        return _pad_lanes(pe).reshape(2, tok)

    def prep_w2(w2):
        return _pad_lanes(w2).astype(BF16)

    rspec = pl.BlockSpec((1, 1, ncw, tok), lambda b, g: (b, g, 0, 0))
    ospec = pl.BlockSpec((1, 1, front + ncw, LANES), lambda b, g: (b, g, 0, 0))
    osds = jax.ShapeDtypeStruct((batch, G, front + ncw, LANES), F32)
    return pl.pallas_call(
        functools.partial(_compress_body, ncw=ncw, front=front),
        grid=(batch, G),
        in_specs=[rspec, rspec, _const_spec((2, tok)), _const_spec((2, tok)),
                  _const_spec((2, tok, hid)), _const_spec((hid, LANES)),
                  _const_spec((2, tok, hid)), _const_spec((hid, LANES)),
                  _const_spec((N_BRANCH, LANES))],
        out_specs=[ospec, ospec],
        out_shape=[osds, osds],
        compiler_params=_cparams(("parallel", "parallel")),
        name="compress",
    )(kc_raw, vc_raw,
      prep_pe(pe_k), prep_pe(pe_v), prep_w1(w1_k), prep_w2(w2_k), prep_w1(w1_v), prep_w2(w2_v),
      _pad_lanes(k_norm))


def _qproj_body(x_ref, g_ref, wq_ref, wg_ref, qn_ref, q_ref, gt_ref, *, tm, d_model):
    xn = _rms(x_ref[0], g_ref[...], d_model).astype(BF16)
    lane = lax.broadcasted_iota(jnp.int32, (tm, LANES), 1)
    scale = HEAD_DIM ** -0.5 * LOG2E
    for h in range(N_HEADS):
        q = _rms(_dot(xn, wq_ref[h]), qn_ref[...], HEAD_DIM) * scale
        q_ref[0, h] = jnp.where(lane == FLAG_LANE, NEG, q).astype(BF16)
    gates = 1.0 / (1.0 + jnp.exp(-_dot(xn, wg_ref[...])))
    n_gate = HEADS_PER_GROUP * N_BRANCH
    for g in range(N_KV_GROUPS):
        shifted = gates if g == 0 else pltpu.roll(gates, LANES - g * n_gate, axis=1)
        gt_ref[0, :, g * LANES:(g + 1) * LANES] = shifted


def _q_proj(h, norm_g, w_q, q_norm, *, batch, tm=256):
    rows, d = h.shape
    seq = rows // batch
    G, HG = N_KV_GROUPS, HEADS_PER_GROUP
    nq = N_HEADS * HEAD_DIM
    wq = _pad_head_cols(w_q[:, :nq], N_HEADS).reshape(d, N_HEADS, LANES).transpose(1, 0, 2).astype(BF16)
    n_gates = w_q.shape[1] - nq
    assert n_gates <= LANES
    wg = jnp.pad(w_q[:, nq:], ((0, 0), (0, LANES - n_gates))).astype(BF16)
    return pl.pallas_call(
        functools.partial(_qproj_body, tm=tm, d_model=d),
        grid=(batch, seq // tm),
        in_specs=[
            pl.BlockSpec((1, tm, d), lambda b, i: (b, i, 0)),
            _const_spec((1, d)),
            _const_spec((N_HEADS, d, LANES)),
            _const_spec((d, LANES)),
            _const_spec((1, LANES)),
        ],
        out_specs=[pl.BlockSpec((1, N_HEADS, tm, LANES), lambda b, i: (b, 0, i, 0)),
                   pl.BlockSpec((1, tm, G * LANES), lambda b, i: (b, i, 0))],
        out_shape=[jax.ShapeDtypeStruct((batch, N_HEADS, seq, LANES), BF16),
                   jax.ShapeDtypeStruct((batch, seq, G * LANES), F32)],
        compiler_params=_cparams(("parallel", "parallel")),
        name="q_proj",
    )(h.reshape(batch, seq, d), norm_g.reshape(1, d), wq, wg, _pad_lanes(q_norm.reshape(1, HEAD_DIM)))


def _tables_body(thr_ref, rb_ref, tblt_ref, tblw_ref, cb_ref, *, ncw, c_off):
    h = pl.program_id(0)

    def bias_of(rel):
        v = jnp.full(rel.shape, rb_ref[0, h], F32)
        for k in range(1, N_BUCKETS):
            v = jnp.where(rel >= thr_ref[k], rb_ref[k, h], v)
        return v * LOG2E

    row = lax.broadcasted_iota(jnp.int32, (QT, QT), 0)
    col = lax.broadcasted_iota(jnp.int32, (QT, QT), 1)

    def tile_t(d, carry):
        rel = (col - row) + d * QT
        tblt_ref[0, d] = jnp.where(rel < 0, NEG, bias_of(rel)).astype(BF16)
        return carry

    lax.fori_loop(0, FAR_TILE + 1, tile_t, 0)
    tblt_ref[0, MASK_TILE] = jnp.full((QT, QT), NEG, BF16)

    def tile_w(d, carry):
        rel = (row - col) + d * QT
        tblw_ref[0, d] = jnp.where((rel < 0) | (rel >= WINDOW), NEG, bias_of(rel))
        return carry

    lax.fori_loop(0, WIN_TILES, tile_w, 0)
    tblw_ref[0, WIN_MASK_TILE] = jnp.full((QT, QT), NEG, F32)
    relc = (lax.broadcasted_iota(jnp.int32, (QT, ncw), 0)
            - CMP_STRIDE * lax.broadcasted_iota(jnp.int32, (QT, ncw), 1) + c_off)
    cb_ref[0] = jnp.where(relc < 0, NEG, bias_of(relc))


def _bucket_thresholds(seq):
    n = jnp.arange(seq)
    nf = jnp.maximum(n, 1).astype(jnp.float32)
    large = MAX_EXACT + (jnp.log(nf / MAX_EXACT) / math.log(MAX_DISTANCE / MAX_EXACT)
                         * (N_BUCKETS - MAX_EXACT)).astype(jnp.int32)
    large = jnp.minimum(large, N_BUCKETS - 1)
    bucket = jnp.where(n < MAX_EXACT, n, large)
    return jnp.sum(bucket[None, :] < jnp.arange(N_BUCKETS)[:, None], axis=1).astype(jnp.int32)


def _bias_tables(rel_bias, seq):
    ncw = seq // CMP_STRIDE
    front = ncw - QT // CMP_STRIDE
    c_off = CMP_STRIDE * front - (CMP_BLOCK - 1)
    smem = pl.BlockSpec(memory_space=pltpu.SMEM)
    return pl.pallas_call(
        functools.partial(_tables_body, ncw=ncw, c_off=c_off),
        grid=(N_HEADS,),
        in_specs=[smem, smem],
        out_specs=[pl.BlockSpec((1, N_TILES, QT, QT), lambda h: (h, 0, 0, 0)),
                   pl.BlockSpec((1, WIN_TILES + 1, QT, QT), lambda h: (h, 0, 0, 0)),
                   pl.BlockSpec((1, QT, ncw), lambda h: (h, 0, 0))],
        out_shape=[jax.ShapeDtypeStruct((N_HEADS, N_TILES, QT, QT), BF16),
                   jax.ShapeDtypeStruct((N_HEADS, WIN_TILES + 1, QT, QT), F32),
                   jax.ShapeDtypeStruct((N_HEADS, QT, ncw), F32)],
        compiler_params=_cparams(("parallel",)),
        name="bias_tables",
    )(_bucket_thresholds(seq), rel_bias)


def _nsa_body(q_ref, gt_ref, ks_ref, vst_ref, kw_ref, vw_ref, kc_ref, vc_ref, ov_ref, tblt_ref, tblw_ref, cb_ref,
              o_ref, sa_ref, sb_ref, pa_ref, pb_ref, pre_s_ref, pre_p_ref, oc_ref, ow_ref, lhs_ref,
              m_ref, ala_ref, alb_ref, acct_ref, *, ncw):
    HG = HEADS_PER_GROUP
    R = HG * QT

    def softmax_strips(s_ref, p_ref, width, bias_fn):
        for st in range(R // STRIP):
            rows = slice(st * STRIP, (st + 1) * STRIP)
            s = s_ref[rows, :width] + bias_fn((st * STRIP) // QT, (st * STRIP) % QT)
            p_ref[rows, :width] = jnp.exp2(s - jnp.max(s, axis=-1, keepdims=True)).astype(BF16)

    def front_stage(sub, sc_ref, pc_ref, sw_ref, pw_ref):
        qi = pl.program_id(2) * TILES_PER_STEP + sub
        q2 = q_ref[0, :, sub * QT:(sub + 1) * QT, :].reshape(R, LANES)

        c0 = pl.multiple_of(qi * (QT // CMP_STRIDE), SUBLANES)
        kc = kc_ref[0, 0, pl.ds(c0, ncw), :].astype(BF16)
        vo = jnp.concatenate([vc_ref[0, 0, pl.ds(c0, ncw), :], ov_ref[pl.ds(c0, ncw), :]],
                             axis=1).astype(BF16)
        sc_ref[:, :ncw] = _dot_t(q2, kc)
        softmax_strips(sc_ref, pc_ref, ncw, lambda h, r: cb_ref[h, r:r + STRIP, :])
        r = _dot(pc_ref[:, :ncw], vo)
        t_row = qi * QT + (lax.broadcasted_iota(jnp.int32, (R, 1), 0) & (QT - 1))
        r = jnp.where(t_row >= CMP_BLOCK - 1, r / r[:, FLAG_LANE:FLAG_LANE + 1], 0.0)
        oc_ref[sub] = r[:, :LANES]
        imp = r[0:QT, LANES:]
        for h in range(1, HG):
            imp = imp + r[h * QT:(h + 1) * QT, LANES:]

        kst = jnp.maximum(qi - WINDOW // QT, 0)
        w0 = pl.multiple_of(kst * QT, QT)
        ids = []
        for u in range(N_WSUB):
            d = qi - kst - u
            ids.append(jnp.where(d < 0, WIN_MASK_TILE, d))
        sw_ref[:, :WIN_KEYS] = _dot_t(q2, kw_ref[0, 0, pl.ds(w0, WIN_KEYS), :])
        softmax_strips(sw_ref, pw_ref, WIN_KEYS,
                       lambda h, r: jnp.concatenate([tblw_ref[h, t, r:r + STRIP, :] for t in ids], axis=1))
        acc_w = _dot(pw_ref[:, :WIN_KEYS], vw_ref[0, 0, pl.ds(w0, WIN_KEYS), :])
        ow_ref[sub] = acc_w / acc_w[:, FLAG_LANE:FLAG_LANE + 1]

        blk = lax.broadcasted_iota(jnp.int32, (LANES, QT), 0)
        blk_t = (qi * QT + lax.broadcasted_iota(jnp.int32, (LANES, QT), 1)) // SLC_BLOCK
        forced = (blk == 0) | (blk == blk_t) | (blk == blk_t - 1)
        score = jnp.where(forced, REMOVED, jnp.where(blk <= blk_t, imp.T, NEG))
        sel = jnp.where(forced, 1.0, 0.0)
        for _ in range(N_SELECT - N_FORCED):
            top = jnp.max(score, axis=0, keepdims=True)
            first = jnp.min(jnp.where(score == top, blk, LANES), axis=0, keepdims=True)
            hit = blk == first
            sel = jnp.where(hit, 1.0, sel)
            score = jnp.where(hit, REMOVED, score)
        sel_neg = ((sel - 1.0) * BIG).T.astype(BF16)
        lhs_ref[sub] = jnp.concatenate([q2, jnp.concatenate([sel_neg] * HG, axis=0)], axis=1)

    def selected_stage(sub):
        qi = pl.program_id(2) * TILES_PER_STEP + sub
        m_ref[...] = jnp.full((1, R), REMOVED, F32)
        acct_ref[...] = jnp.zeros((LANES, R), F32)
        n_chunks = qi // N_SUB + 1

        def scores(s_ref, kj):
            k0 = pl.multiple_of(kj * KCHUNK, KCHUNK)
            s_ref[...] = _dot_t(ks_ref[0, 0, pl.ds(k0, KCHUNK), :], lhs_ref[sub]).astype(BF16)

        def accumulate(s_ref, p_ref, al_ref, kj):
            for h in range(HG):
                cols = slice(h * QT, (h + 1) * QT)
                mx = m_ref[:, cols]
                for u in range(N_SUB):
                    keys = slice(u * QT, (u + 1) * QT)
                    d = qi - kj * N_SUB - u
                    tile = jnp.where(d < 0, MASK_TILE, jnp.minimum(d, FAR_TILE))
                    sb = s_ref[keys, cols] + tblt_ref[h, tile]
                    s_ref[keys, cols] = sb
                    mx = jnp.maximum(mx, jnp.max(sb, axis=0, keepdims=True).astype(F32))
                al_ref[:, cols] = jnp.exp2(m_ref[:, cols] - mx)
                m_ref[:, cols] = mx
                p_ref[:, cols] = jnp.exp2(s_ref[:, cols] - mx.astype(BF16))
            acct_ref[...] = al_ref[...] * acct_ref[...] + _dot(vst_ref[0, 0, kj], p_ref[...])

        scores(sa_ref, 0)

        def chunk_pair(j, carry):
            a = 2 * j
            scores(sb_ref, a + 1)
            accumulate(sa_ref, pa_ref, ala_ref, a)
            scores(sa_ref, jnp.minimum(a + 2, n_chunks - 1))
            accumulate(sb_ref, pb_ref, alb_ref, a + 1)
            return carry

        lax.fori_loop(0, n_chunks // 2, chunk_pair, 0)

        @pl.when(n_chunks % 2 == 1)
        def _():
            accumulate(sa_ref, pa_ref, ala_ref, n_chunks - 1)

        acct = acct_ref[...]
        o_st = acct / acct[FLAG_LANE:FLAG_LANE + 1, :]

        tile_rows = slice(sub * QT, (sub + 1) * QT)
        gt = gt_ref[0, tile_rows, :]
        o_c = oc_ref[sub]
        o_w = ow_ref[sub]
        outs = []
        for h in range(HG):
            rows = slice(h * QT, (h + 1) * QT)
            c = h * N_BRANCH
            outs.append(gt[:, c:c + 1] * o_c[rows] + gt[:, c + 1:c + 2] * o_st[:, rows].T
                        + gt[:, c + 2:c + 3] * o_w[rows])
        o_ref[0, tile_rows, :] = jnp.concatenate(outs, axis=1).astype(BF16)

    for sub in range(TILES_PER_STEP):
        front_stage(sub, pre_s_ref.at[2 * sub], pre_p_ref.at[2 * sub],
                    pre_s_ref.at[2 * sub + 1], pre_p_ref.at[2 * sub + 1])
    for sub in range(TILES_PER_STEP):
        selected_stage(sub)


def _overlap_table(seq):
    ncw = seq // CMP_STRIDE
    front = ncw - QT // CMP_STRIDE
    nc = (seq - CMP_BLOCK) // CMP_STRIDE + 1
    ns = seq // SLC_BLOCK
    c_start = jnp.arange(nc) * CMP_STRIDE
    s_start = jnp.arange(ns) * SLC_BLOCK
    ov = (jnp.clip(jnp.minimum(c_start[:, None] + CMP_BLOCK, s_start[None, :] + SLC_BLOCK)
                   - jnp.maximum(c_start[:, None], s_start[None, :]), 0) / CMP_STRIDE).astype(F32)
    return jnp.pad(ov, ((front, ncw - nc), (0, LANES - ns)))


def _nsa(q, gates, ks, vst, kw, vw, kc, vc, tblt, tblw, cb):
    batch, _, seq, _ = q.shape
    G, HG = N_KV_GROUPS, HEADS_PER_GROUP
    ncw = seq // CMP_STRIDE
    crow = kc.shape[2]
    ov = _overlap_table(seq)
    rows = HG * QT
    width = max(ncw, WIN_KEYS)
    kvspec = lambda lanes: pl.BlockSpec((1, 1, seq, lanes), lambda g, b, i: (b, g, 0, 0))
    cspec = pl.BlockSpec((1, 1, crow, LANES), lambda g, b, i: (b, g, 0, 0))
    per_group = lambda n, w: pl.BlockSpec((HG, n, QT, w), lambda g, b, i: (g, 0, 0, 0),
                                          pipeline_mode=pl.Buffered(1))
    tq = TILES_PER_STEP * QT
    n_pre = 2 * TILES_PER_STEP
    return pl.pallas_call(
        functools.partial(_nsa_body, ncw=ncw),
        grid=(G, batch, seq // tq),
        in_specs=[
            pl.BlockSpec((1, HG, tq, LANES), lambda g, b, i: (b, g, i, 0)),
            pl.BlockSpec((1, tq, LANES), lambda g, b, i: (b, i, g)),
            kvspec(2 * LANES),
            pl.BlockSpec((1, 1, seq // KCHUNK, LANES, KCHUNK), lambda g, b, i: (b, g, 0, 0, 0)),
            kvspec(LANES), kvspec(LANES),
            cspec, cspec,
            _const_spec(ov.shape),
            per_group(N_TILES, QT), per_group(WIN_TILES + 1, QT),
            pl.BlockSpec((HG, QT, ncw), lambda g, b, i: (g, 0, 0), pipeline_mode=pl.Buffered(1)),
        ],
        out_specs=pl.BlockSpec((1, tq, HG * LANES), lambda g, b, i: (b, i, g)),
        out_shape=jax.ShapeDtypeStruct((batch, seq, N_HEADS * LANES), BF16),
        scratch_shapes=[pltpu.VMEM((KCHUNK, rows), BF16), pltpu.VMEM((KCHUNK, rows), BF16),
                        pltpu.VMEM((KCHUNK, rows), BF16), pltpu.VMEM((KCHUNK, rows), BF16),
                        pltpu.VMEM((n_pre, rows, width), F32), pltpu.VMEM((n_pre, rows, width), BF16),
                        pltpu.VMEM((TILES_PER_STEP, rows, LANES), F32),
                        pltpu.VMEM((TILES_PER_STEP, rows, LANES), F32),
                        pltpu.VMEM((TILES_PER_STEP, rows, 2 * LANES), BF16),
                        pltpu.VMEM((1, rows), F32), pltpu.VMEM((1, rows), F32), pltpu.VMEM((1, rows), F32),
                        pltpu.VMEM((LANES, rows), F32)],
        compiler_params=_cparams(("arbitrary", "arbitrary", "arbitrary")),
        name="nsa",
    )(q, gates, ks, vst, kw, vw, kc, vc, ov, tblt, tblw, cb)


def _oproj_body(x_ref, a_ref, w_ref, o_ref):
    o_ref[...] = x_ref[...] + _dot(a_ref[...], w_ref[...])


def _o_proj(h, attn, w_o, *, tm=1024):
    rows, d = h.shape
    k = attn.shape[1]
    w = jnp.pad(w_o.reshape(N_HEADS, HEAD_DIM, d), ((0, 0), (0, LANES - HEAD_DIM), (0, 0)))
    w = w.reshape(k, d).astype(BF16)
    return pl.pallas_call(
        _oproj_body,
        grid=(rows // tm,),
        in_specs=[pl.BlockSpec((tm, d), lambda i: (i, 0)),
                  pl.BlockSpec((tm, k), lambda i: (i, 0)),
                  _const_spec((k, d))],
        out_specs=pl.BlockSpec((tm, d), lambda i: (i, 0)),
        out_shape=jax.ShapeDtypeStruct((rows, d), F32),
        compiler_params=_cparams(("parallel",)),
        name="o_proj",
    )(h, attn, w)


def kernel(x, ffn1_norm, ffn1_w_in, ffn1_w_out, mix_norm, ffn2_norm, ffn2_w_in, ffn2_w_out, conv_w_in, conv_w, conv_w_out, attn_w_q, attn_q_norm, attn_w_o, kv_norm, w_kv, k_norm, cmp_pe_k, cmp_pe_v, cmp_w1_k, cmp_w2_k, cmp_w1_v, cmp_w2_v, rel_bias):
    batch, seq, d = x.shape
    depth = ffn1_norm.shape[0]
    n_a = conv_w_in.shape[0]
    assert seq % KCHUNK == 0 and seq // SLC_BLOCK <= LANES and seq >= WIN_KEYS
    h = x.reshape(batch * seq, d)
    kv = tables = None
    for i in range(depth):
        if i == n_a:
            kc_raw, vc_raw, ks, vst, kw, vw = _kv_proj(h, kv_norm, w_kv, k_norm, batch=batch)
            kc, vc = _compress(kc_raw, vc_raw, cmp_pe_k, cmp_pe_v, cmp_w1_k, cmp_w2_k,
                               cmp_w1_v, cmp_w2_v, k_norm)
            kv = (ks, vst, kw, vw, kc, vc)
            tables = _bias_tables(rel_bias, seq)
        h = _ffn(h, ffn1_norm[i], ffn1_w_in[i], ffn1_w_out[i])
        if i < n_a:
            h = _conv_mixer(h, mix_norm[i], conv_w_in[i], conv_w[i], conv_w_out[i], batch=batch)
        else:
            j = i - n_a
            q, gates = _q_proj(h, mix_norm[i], attn_w_q[j], attn_q_norm[j], batch=batch)
            attn = _nsa(q, gates, *kv, *tables)
            h = _o_proj(h, attn.reshape(batch * seq, -1), attn_w_o[j])
        h = _ffn(h, ffn2_norm[i], ffn2_w_in[i], ffn2_w_out[i])
    return h.reshape(batch, seq, d)
```

```python
import functools
import math

import jax
import jax.numpy as jnp
from jax import lax
from jax.experimental import pallas as pl
from jax.experimental.pallas import tpu as pltpu

F32 = jnp.float32
BF16 = jnp.bfloat16

N_HEADS = 16
N_KV_GROUPS = 4
HEADS_PER_GROUP = N_HEADS // N_KV_GROUPS
HEAD_DIM = 64
N_BRANCH = 3
CMP_BLOCK = 32
CMP_STRIDE = 16
SLC_BLOCK = 64
N_SELECT = 16
N_FORCED = 3
WINDOW = 512
N_BUCKETS = 32
MAX_EXACT = N_BUCKETS // 2
MAX_DISTANCE = 4096
EPS = 1e-6
NEG = -1e30
BIG = 1e30
REMOVED = -3e38
LOG2E = math.log2(math.e)

LANES = 128
SUBLANES = 8
VMEM_LIMIT = 56 * 1024 * 1024

QT = 128
TILES_PER_STEP = 4
KCHUNK = 512
N_SUB = KCHUNK // QT
STRIP = 32
WIN_KEYS = WINDOW + QT
N_WSUB = WIN_KEYS // QT
FAR_TILE = int(math.ceil((MAX_EXACT * (MAX_DISTANCE / MAX_EXACT) ** ((N_BUCKETS - MAX_EXACT - 1) / (N_BUCKETS - MAX_EXACT))
                          + QT) / QT))
MASK_TILE = FAR_TILE + 1
N_TILES = FAR_TILE + 2
WIN_TILES = WINDOW // QT + 1
WIN_MASK_TILE = WIN_TILES
FLAG_LANE = HEAD_DIM


def _cparams(sem):
    return pltpu.CompilerParams(dimension_semantics=sem, vmem_limit_bytes=VMEM_LIMIT)


def _const_spec(shape):
    nd = len(shape)
    return pl.BlockSpec(shape, lambda *_: (0,) * nd, pipeline_mode=pl.Buffered(1))


def _rms(x, g, n):
    ms = jnp.sum(x * x, axis=-1, keepdims=True) * (1.0 / n)
    return x * lax.rsqrt(ms + EPS) * g


def _dot(a, b):
    return jnp.dot(a, b, preferred_element_type=F32)


def _dot_t(a, b):
    return lax.dot_general(a, b, (((1,), (1,)), ((), ())), preferred_element_type=F32)


def _ffn_body(x_ref, g_ref, wg_ref, wu_ref, wo_ref, o_ref, acc_ref, *, n_chunks, d_model):
    x = x_ref[...]
    xn = _rms(x, g_ref[...], d_model).astype(BF16)
    acc_ref[...] = jnp.zeros_like(acc_ref)

    def chunk(c, carry):
        gate = _dot(xn, wg_ref[c])
        up = _dot(xn, wu_ref[c])
        act = (gate * (1.0 / (1.0 + jnp.exp(-gate))) * up).astype(BF16)
        acc_ref[...] += _dot(act, wo_ref[c])
        return carry

    lax.fori_loop(0, n_chunks, chunk, 0)
    o_ref[...] = x + 0.5 * acc_ref[...]


def _ffn(h, norm_g, w_in, w_out, *, tm=1024, ck=256):
    rows, d = h.shape
    dff = w_out.shape[0]
    nch = dff // ck
    wg = w_in[:, :dff].reshape(d, nch, ck).transpose(1, 0, 2).astype(BF16)
    wu = w_in[:, dff:].reshape(d, nch, ck).transpose(1, 0, 2).astype(BF16)
    wo = w_out.reshape(nch, ck, d).astype(BF16)
    return pl.pallas_call(
        functools.partial(_ffn_body, n_chunks=nch, d_model=d),
        grid=(rows // tm,),
        in_specs=[
            pl.BlockSpec((tm, d), lambda i: (i, 0)),
            _const_spec((1, d)),
            _const_spec((nch, d, ck)),
            _const_spec((nch, d, ck)),
            _const_spec((nch, ck, d)),
        ],
        out_specs=pl.BlockSpec((tm, d), lambda i: (i, 0)),
        out_shape=jax.ShapeDtypeStruct((rows, d), F32),
        scratch_shapes=[pltpu.VMEM((tm, d), F32)],
        compiler_params=_cparams(("parallel",)),
        name="ffn",
    )(h, norm_g.reshape(1, d), wg, wu, wo)


def _conv_body(x_ref, g_ref, win_ref, cw_ref, wout_ref, o_ref, ubuf_ref, *, tm, d_model):
    i = pl.program_id(1)
    x = x_ref[0]
    xn = _rms(x, g_ref[...], d_model).astype(BF16)
    b_gate = _dot(xn, win_ref[0])
    c_gate = _dot(xn, win_ref[1])
    v = _dot(xn, win_ref[2])
    u = c_gate * v

    @pl.when(i == 0)
    def _():
        ubuf_ref[0:SUBLANES, :] = jnp.zeros((SUBLANES, d_model), F32)

    ubuf_ref[SUBLANES:SUBLANES + tm, :] = u
    u1 = ubuf_ref[SUBLANES - 1:SUBLANES - 1 + tm, :]
    u2 = ubuf_ref[SUBLANES - 2:SUBLANES - 2 + tm, :]
    y = cw_ref[2:3, :] * u + cw_ref[1:2, :] * u1 + cw_ref[0:1, :] * u2
    ubuf_ref[0:SUBLANES, :] = u[tm - SUBLANES:, :]
    o_ref[0] = x + _dot((b_gate * y).astype(BF16), wout_ref[...])


def _conv_mixer(h, norm_g, w_in, conv_w, w_out, *, batch, tm=512):
    rows, d = h.shape
    seq = rows // batch
    win = w_in.reshape(d, 3, d).transpose(1, 0, 2).astype(BF16)
    out = pl.pallas_call(
        functools.partial(_conv_body, tm=tm, d_model=d),
        grid=(batch, seq // tm),
        in_specs=[
            pl.BlockSpec((1, tm, d), lambda b, i: (b, i, 0)),
            _const_spec((1, d)),
            _const_spec((3, d, d)),
            _const_spec((conv_w.shape[0], d)),
            _const_spec((d, d)),
        ],
        out_specs=pl.BlockSpec((1, tm, d), lambda b, i: (b, i, 0)),
        out_shape=jax.ShapeDtypeStruct((batch, seq, d), F32),
        scratch_shapes=[pltpu.VMEM((tm + SUBLANES, d), F32)],
        compiler_params=_cparams(("arbitrary", "arbitrary")),
        name="conv_mixer",
    )(h.reshape(batch, seq, d), norm_g.reshape(1, d), win, conv_w, w_out.astype(BF16))
    return out.reshape(rows, d)


def _pad_head_cols(w, n_slices):
    d = w.shape[0]
    w = w.reshape(d, n_slices, HEAD_DIM)
    return jnp.pad(w, ((0, 0), (0, 0), (0, LANES - HEAD_DIM))).reshape(d, n_slices * LANES)


def _pad_lanes(v):
    return jnp.pad(v, [(0, 0)] * (v.ndim - 1) + [(0, LANES - HEAD_DIM)])


def _kvproj_body(x_ref, g_ref, w_ref, wvt_ref, kn_ref, kc_ref, vc_ref, ks_ref, vst_ref, kw_ref, vw_ref,
                 fold_ref, *, tm, d_model):
    i = pl.program_id(1)
    xn = _rms(x_ref[0], g_ref[...], d_model).astype(BF16)
    lane = lax.broadcasted_iota(jnp.int32, (tm, LANES), 1)
    ones_col = jnp.where(lane == FLAG_LANE, 1.0, 0.0)
    ones_rows = jnp.where(lax.broadcasted_iota(jnp.int32, (LANES - HEAD_DIM, tm), 0) == 0, 1.0, 0.0)
    blk = (i * tm + lax.broadcasted_iota(jnp.int32, (tm, LANES), 0)) // SLC_BLOCK
    onehot = jnp.where(lane == blk, 1.0, 0.0).astype(BF16)
    G = N_KV_GROUPS
    pairs = [_dot(xn, w_ref[p]) for p in range(w_ref.shape[0])]
    vst = _dot_t(wvt_ref[...], xn)
    for g in range(G):
        def col(br, kv):
            s = (br * 2 + kv) * G + g
            blk2 = pairs[s // 2] if s % 2 == 0 else pltpu.roll(pairs[s // 2], HEAD_DIM, axis=1)
            return jnp.where(lane < HEAD_DIM, blk2, 0.0)
        for kv, out_ref in ((0, kc_ref), (1, vc_ref)):
            stage = fold_ref.at[2 * g + kv]
            stage[...] = col(0, kv)
            for r in range(CMP_STRIDE):
                out_ref[0, g, :, r * LANES:(r + 1) * LANES] = stage[pl.ds(r, tm // CMP_STRIDE, stride=CMP_STRIDE), :]
        ks = _rms(col(1, 0), kn_ref[1:2, :], HEAD_DIM).astype(BF16)
        ks_ref[0, g] = jnp.concatenate([ks, onehot], axis=1)
        vst_ref[0, g] = jnp.concatenate([vst[g * HEAD_DIM:(g + 1) * HEAD_DIM], ones_rows], axis=0).astype(BF16)
        kw_ref[0, g] = _rms(col(2, 0), kn_ref[2:3, :], HEAD_DIM).astype(BF16)
        vw_ref[0, g] = (col(2, 1) + ones_col).astype(BF16)


def _kv_proj(h, kv_norm, w_kv, k_norm, *, batch, tm=512):
    rows, d = h.shape
    seq = rows // batch
    G = N_KV_GROUPS
    n_sl = N_BRANCH * 2 * G
    n_pair = n_sl * HEAD_DIM // LANES
    w = w_kv.reshape(d, n_pair, LANES).transpose(1, 0, 2).astype(BF16)
    v_sel = (1 * 2 + 1) * G * HEAD_DIM
    wvt = w_kv[:, v_sel:v_sel + G * HEAD_DIM].T.astype(BF16)
    kn = _pad_lanes(k_norm)
    sds = lambda lanes, dt: jax.ShapeDtypeStruct((batch, G, seq, lanes), dt)
    ospec = lambda lanes: pl.BlockSpec((1, G, tm, lanes), lambda b, i: (b, 0, i, 0))
    fsds = jax.ShapeDtypeStruct((batch, G, seq // CMP_STRIDE, CMP_STRIDE * LANES), F32)
    fspec = pl.BlockSpec((1, G, tm // CMP_STRIDE, CMP_STRIDE * LANES), lambda b, i: (b, 0, i, 0))
    kc_raw, vc_raw, ks, vst, kw, vw = pl.pallas_call(
        functools.partial(_kvproj_body, tm=tm, d_model=d),
        grid=(batch, seq // tm),
        in_specs=[
            pl.BlockSpec((1, tm, d), lambda b, i: (b, i, 0)),
            _const_spec((1, d)),
            _const_spec((n_pair, d, LANES)),
            _const_spec((G * HEAD_DIM, d)),
            _const_spec((N_BRANCH, LANES)),
        ],
        out_specs=[fspec, fspec, ospec(2 * LANES),
                   pl.BlockSpec((1, G, LANES, tm), lambda b, i: (b, 0, 0, i)), ospec(LANES), ospec(LANES)],
        out_shape=[fsds, fsds, sds(2 * LANES, BF16),
                   jax.ShapeDtypeStruct((batch, G, LANES, seq), BF16), sds(LANES, BF16), sds(LANES, BF16)],
        scratch_shapes=[pltpu.VMEM((2 * G, tm, LANES), F32)],
        compiler_params=_cparams(("parallel", "parallel")),
        name="kv_proj",
    )(h.reshape(batch, seq, d), kv_norm.reshape(1, d), w, wvt, kn)
    vst = vst.reshape(batch, G, LANES, seq // KCHUNK, KCHUNK).transpose(0, 1, 3, 2, 4)
    return kc_raw, vc_raw, ks, vst, kw, vw


def _gelu_tanh(x):
    return 0.5 * x * (1.0 + jnp.tanh(math.sqrt(2.0 / math.pi) * (x + 0.044715 * (x * x * x))))


def _compress_body(kr_ref, vr_ref, pek_ref, pev_ref, w1k_ref, w2k_ref, w1v_ref, w2v_ref, kn_ref,
                   kc_ref, vc_ref, *, ncw, front):
    row = lax.broadcasted_iota(jnp.int32, (ncw, LANES), 0)
    row_lane = lax.broadcasted_iota(jnp.int32, (ncw, LANES), 1)
    lane = lax.broadcasted_iota(jnp.int32, (front, LANES), 1)

    def mlp(r_ref, pe_ref, w1_ref, w2_ref):
        r = r_ref[0, 0]
        a = _dot((r + pe_ref[0:1, :]).astype(BF16), w1_ref[0])
        b = _dot((r + pe_ref[1:2, :]).astype(BF16), w1_ref[1])
        hid = a + pltpu.roll(b, ncw - 1, axis=0)
        out = _dot(_gelu_tanh(hid).astype(BF16), w2_ref[...])
        return jnp.where(row < ncw - 1, out, 0.0)

    kc = _rms(mlp(kr_ref, pek_ref, w1k_ref, w2k_ref), kn_ref[0:1, :], HEAD_DIM)
    vc = mlp(vr_ref, pev_ref, w1v_ref, w2v_ref)
    kc_ref[0, 0, 0:front, :] = jnp.where(lane == FLAG_LANE, 1.0, 0.0)
    vc_ref[0, 0, 0:front, :] = jnp.zeros((front, LANES), F32)
    kc_ref[0, 0, front:front + ncw, :] = kc
    vc_ref[0, 0, front:front + ncw, :] = vc + jnp.where(row_lane == FLAG_LANE, 1.0, 0.0)


def _compress(kc_raw, vc_raw, pe_k, pe_v, w1_k, w2_k, w1_v, w2_v, k_norm):
    batch, G, ncw, tok = kc_raw.shape
    front = ncw - QT // CMP_STRIDE
    hid = w1_k.shape[1]

    def prep_w1(w1):
        w = w1.reshape(CMP_BLOCK, HEAD_DIM, hid)
        w = jnp.pad(w, ((0, 0), (0, LANES - HEAD_DIM), (0, 0)))
        return w.reshape(2, tok, hid).astype(BF16)

    def prep_pe(pe):
        return _pad_lanes(pe).reshape(2, tok)

    def prep_w2(w2):
        return _pad_lanes(w2).astype(BF16)

    rspec = pl.BlockSpec((1, 1, ncw, tok), lambda b, g: (b, g, 0, 0))
    ospec = pl.BlockSpec((1, 1, front + ncw, LANES), lambda b, g: (b, g, 0, 0))
    osds = jax.ShapeDtypeStruct((batch, G, front + ncw, LANES), F32)
    return pl.pallas_call(
        functools.partial(_compress_body, ncw=ncw, front=front),
        grid=(batch, G),
        in_specs=[rspec, rspec, _const_spec((2, tok)), _const_spec((2, tok)),
                  _const_spec((2, tok, hid)), _const_spec((hid, LANES)),
                  _const_spec((2, tok, hid)), _const_spec((hid, LANES)),
                  _const_spec((N_BRANCH, LANES))],
        out_specs=[ospec, ospec],
        out_shape=[osds, osds],
        compiler_params=_cparams(("parallel", "parallel")),
        name="compress",
    )(kc_raw, vc_raw,
      prep_pe(pe_k), prep_pe(pe_v), prep_w1(w1_k), prep_w2(w2_k), prep_w1(w1_v), prep_w2(w2_v),
      _pad_lanes(k_norm))


def _qproj_body(x_ref, g_ref, wq_ref, wg_ref, qn_ref, q_ref, gt_ref, *, tm, d_model):
    xn = _rms(x_ref[0], g_ref[...], d_model).astype(BF16)
    lane = lax.broadcasted_iota(jnp.int32, (tm, LANES), 1)
    scale = HEAD_DIM ** -0.5 * LOG2E
    for h in range(N_HEADS):
        q = _rms(_dot(xn, wq_ref[h]), qn_ref[...], HEAD_DIM) * scale
        q_ref[0, h] = jnp.where(lane == FLAG_LANE, NEG, q).astype(BF16)
    gates = 1.0 / (1.0 + jnp.exp(-_dot(xn, wg_ref[...])))
    n_gate = HEADS_PER_GROUP * N_BRANCH
    for g in range(N_KV_GROUPS):
        shifted = gates if g == 0 else pltpu.roll(gates, LANES - g * n_gate, axis=1)
        gt_ref[0, :, g * LANES:(g + 1) * LANES] = shifted


def _q_proj(h, norm_g, w_q, q_norm, *, batch, tm=256):
    rows, d = h.shape
    seq = rows // batch
    G, HG = N_KV_GROUPS, HEADS_PER_GROUP
    nq = N_HEADS * HEAD_DIM
    wq = _pad_head_cols(w_q[:, :nq], N_HEADS).reshape(d, N_HEADS, LANES).transpose(1, 0, 2).astype(BF16)
    n_gates = w_q.shape[1] - nq
    assert n_gates <= LANES
    wg = jnp.pad(w_q[:, nq:], ((0, 0), (0, LANES - n_gates))).astype(BF16)
    return pl.pallas_call(
        functools.partial(_qproj_body, tm=tm, d_model=d),
        grid=(batch, seq // tm),
        in_specs=[
            pl.BlockSpec((1, tm, d), lambda b, i: (b, i, 0)),
            _const_spec((1, d)),
            _const_spec((N_HEADS, d, LANES)),
            _const_spec((d, LANES)),
            _const_spec((1, LANES)),
        ],
        out_specs=[pl.BlockSpec((1, N_HEADS, tm, LANES), lambda b, i: (b, 0, i, 0)),
                   pl.BlockSpec((1, tm, G * LANES), lambda b, i: (b, i, 0))],
        out_shape=[jax.ShapeDtypeStruct((batch, N_HEADS, seq, LANES), BF16),
                   jax.ShapeDtypeStruct((batch, seq, G * LANES), F32)],
        compiler_params=_cparams(("parallel", "parallel")),
        name="q_proj",
    )(h.reshape(batch, seq, d), norm_g.reshape(1, d), wq, wg, _pad_lanes(q_norm.reshape(1, HEAD_DIM)))


def _tables_body(thr_ref, rb_ref, tblt_ref, tblw_ref, cb_ref, *, ncw, c_off):
    h = pl.program_id(0)

    def bias_of(rel):
        v = jnp.full(rel.shape, rb_ref[0, h], F32)
        for k in range(1, N_BUCKETS):
            v = jnp.where(rel >= thr_ref[k], rb_ref[k, h], v)
        return v * LOG2E

    row = lax.broadcasted_iota(jnp.int32, (QT, QT), 0)
    col = lax.broadcasted_iota(jnp.int32, (QT, QT), 1)

    def tile_t(d, carry):
        rel = (col - row) + d * QT
        tblt_ref[0, d] = jnp.where(rel < 0, NEG, bias_of(rel))
        return carry

    lax.fori_loop(0, FAR_TILE + 1, tile_t, 0)
    tblt_ref[0, MASK_TILE] = jnp.full((QT, QT), NEG, F32)

    def tile_w(d, carry):
        rel = (row - col) + d * QT
        tblw_ref[0, d] = jnp.where((rel < 0) | (rel >= WINDOW), NEG, bias_of(rel))
        return carry

    lax.fori_loop(0, WIN_TILES, tile_w, 0)
    tblw_ref[0, WIN_MASK_TILE] = jnp.full((QT, QT), NEG, F32)
    relc = (lax.broadcasted_iota(jnp.int32, (QT, ncw), 0)
            - CMP_STRIDE * lax.broadcasted_iota(jnp.int32, (QT, ncw), 1) + c_off)
    cb_ref[0] = jnp.where(relc < 0, NEG, bias_of(relc))


def _bucket_thresholds(seq):
    n = jnp.arange(seq)
    nf = jnp.maximum(n, 1).astype(jnp.float32)
    large = MAX_EXACT + (jnp.log(nf / MAX_EXACT) / math.log(MAX_DISTANCE / MAX_EXACT)
                         * (N_BUCKETS - MAX_EXACT)).astype(jnp.int32)
    large = jnp.minimum(large, N_BUCKETS - 1)
    bucket = jnp.where(n < MAX_EXACT, n, large)
    return jnp.sum(bucket[None, :] < jnp.arange(N_BUCKETS)[:, None], axis=1).astype(jnp.int32)


def _bias_tables(rel_bias, seq):
    ncw = seq // CMP_STRIDE
    front = ncw - QT // CMP_STRIDE
    c_off = CMP_STRIDE * front - (CMP_BLOCK - 1)
    smem = pl.BlockSpec(memory_space=pltpu.SMEM)
    return pl.pallas_call(
        functools.partial(_tables_body, ncw=ncw, c_off=c_off),
        grid=(N_HEADS,),
        in_specs=[smem, smem],
        out_specs=[pl.BlockSpec((1, N_TILES, QT, QT), lambda h: (h, 0, 0, 0)),
                   pl.BlockSpec((1, WIN_TILES + 1, QT, QT), lambda h: (h, 0, 0, 0)),
                   pl.BlockSpec((1, QT, ncw), lambda h: (h, 0, 0))],
        out_shape=[jax.ShapeDtypeStruct((N_HEADS, N_TILES, QT, QT), F32),
                   jax.ShapeDtypeStruct((N_HEADS, WIN_TILES + 1, QT, QT), F32),
                   jax.ShapeDtypeStruct((N_HEADS, QT, ncw), F32)],
        compiler_params=_cparams(("parallel",)),
        name="bias_tables",
    )(_bucket_thresholds(seq), rel_bias)


def _nsa_body(q_ref, gt_ref, ks_ref, vst_ref, kw_ref, vw_ref, kc_ref, vc_ref, ov_ref, tblt_ref, tblw_ref, cb_ref,
              o_ref, sa_ref, sb_ref, pa_ref, pb_ref, pre_s_ref, pre_p_ref, oc_ref, ow_ref, lhs_ref,
              m_ref, ala_ref, alb_ref, acct_ref, *, ncw):
    HG = HEADS_PER_GROUP
    R = HG * QT

    def softmax_strips(s_ref, p_ref, width, bias_fn):
        for st in range(R // STRIP):
            rows = slice(st * STRIP, (st + 1) * STRIP)
            s = s_ref[rows, :width] + bias_fn((st * STRIP) // QT, (st * STRIP) % QT)
            p_ref[rows, :width] = jnp.exp2(s - jnp.max(s, axis=-1, keepdims=True)).astype(BF16)

    def front_stage(sub, sc_ref, pc_ref, sw_ref, pw_ref):
        qi = pl.program_id(2) * TILES_PER_STEP + sub
        q2 = q_ref[0, :, sub * QT:(sub + 1) * QT, :].reshape(R, LANES)

        c0 = pl.multiple_of(qi * (QT // CMP_STRIDE), SUBLANES)
        kc = kc_ref[0, 0, pl.ds(c0, ncw), :].astype(BF16)
        vo = jnp.concatenate([vc_ref[0, 0, pl.ds(c0, ncw), :], ov_ref[pl.ds(c0, ncw), :]],
                             axis=1).astype(BF16)
        sc_ref[:, :ncw] = _dot_t(q2, kc)
        softmax_strips(sc_ref, pc_ref, ncw, lambda h, r: cb_ref[h, r:r + STRIP, :])
        r = _dot(pc_ref[:, :ncw], vo)
        t_row = qi * QT + (lax.broadcasted_iota(jnp.int32, (R, 1), 0) & (QT - 1))
        r = jnp.where(t_row >= CMP_BLOCK - 1, r / r[:, FLAG_LANE:FLAG_LANE + 1], 0.0)
        oc_ref[sub] = r[:, :LANES]
        imp = r[0:QT, LANES:]
        for h in range(1, HG):
            imp = imp + r[h * QT:(h + 1) * QT, LANES:]

        kst = jnp.maximum(qi - WINDOW // QT, 0)
        w0 = pl.multiple_of(kst * QT, QT)
        ids = []
        for u in range(N_WSUB):
            d = qi - kst - u
            ids.append(jnp.where(d < 0, WIN_MASK_TILE, d))
        sw_ref[:, :WIN_KEYS] = _dot_t(q2, kw_ref[0, 0, pl.ds(w0, WIN_KEYS), :])
        softmax_strips(sw_ref, pw_ref, WIN_KEYS,
                       lambda h, r: jnp.concatenate([tblw_ref[h, t, r:r + STRIP, :] for t in ids], axis=1))
        acc_w = _dot(pw_ref[:, :WIN_KEYS], vw_ref[0, 0, pl.ds(w0, WIN_KEYS), :])
        ow_ref[sub] = acc_w / acc_w[:, FLAG_LANE:FLAG_LANE + 1]

        blk = lax.broadcasted_iota(jnp.int32, (LANES, QT), 0)
        blk_t = (qi * QT + lax.broadcasted_iota(jnp.int32, (LANES, QT), 1)) // SLC_BLOCK
        forced = (blk == 0) | (blk == blk_t) | (blk == blk_t - 1)
        score = jnp.where(forced, REMOVED, jnp.where(blk <= blk_t, imp.T, NEG))
        sel = jnp.where(forced, 1.0, 0.0)
        for _ in range(N_SELECT - N_FORCED):
            top = jnp.max(score, axis=0, keepdims=True)
            first = jnp.min(jnp.where(score == top, blk, LANES), axis=0, keepdims=True)
            hit = blk == first
            sel = jnp.where(hit, 1.0, sel)
            score = jnp.where(hit, REMOVED, score)
        sel_neg = ((sel - 1.0) * BIG).T.astype(BF16)
        lhs_ref[sub] = jnp.concatenate([q2, jnp.concatenate([sel_neg] * HG, axis=0)], axis=1)

    def selected_stage(sub):
        qi = pl.program_id(2) * TILES_PER_STEP + sub
        m_ref[...] = jnp.full((1, R), REMOVED, F32)
        acct_ref[...] = jnp.zeros((LANES, R), F32)
        n_chunks = qi // N_SUB + 1

        def scores(s_ref, kj):
            k0 = pl.multiple_of(kj * KCHUNK, KCHUNK)
            s_ref[:, :R] = _dot_t(ks_ref[0, 0, pl.ds(k0, KCHUNK), :], lhs_ref[sub])

        def accumulate(s_ref, p_ref, al_ref, kj):
            for h in range(HG):
                cols = slice(h * QT, (h + 1) * QT)
                mx = m_ref[:, cols]
                for u in range(N_SUB):
                    keys = slice(u * QT, (u + 1) * QT)
                    d = qi - kj * N_SUB - u
                    tile = jnp.where(d < 0, MASK_TILE, jnp.minimum(d, FAR_TILE))
                    sb = s_ref[keys, cols] + tblt_ref[h, tile]
                    s_ref[keys, cols] = sb
                    mx = jnp.maximum(mx, jnp.max(sb, axis=0, keepdims=True))
                al_ref[:, cols] = jnp.exp2(m_ref[:, cols] - mx)
                m_ref[:, cols] = mx
                p_ref[:, cols] = jnp.exp2(s_ref[:, cols] - mx).astype(BF16)
            acct_ref[...] = al_ref[...] * acct_ref[...] + _dot(vst_ref[0, 0, kj], p_ref[:, :R])

        scores(sa_ref, 0)

        def chunk_pair(j, carry):
            a = 2 * j
            scores(sb_ref, a + 1)
            accumulate(sa_ref, pa_ref, ala_ref, a)
            scores(sa_ref, jnp.minimum(a + 2, n_chunks - 1))
            accumulate(sb_ref, pb_ref, alb_ref, a + 1)
            return carry

        lax.fori_loop(0, n_chunks // 2, chunk_pair, 0)

        @pl.when(n_chunks % 2 == 1)
        def _():
            accumulate(sa_ref, pa_ref, ala_ref, n_chunks - 1)

        acct = acct_ref[...]
        o_st = acct / acct[FLAG_LANE:FLAG_LANE + 1, :]

        tile_rows = slice(sub * QT, (sub + 1) * QT)
        gt = gt_ref[0, tile_rows, :]
        o_c = oc_ref[sub]
        o_w = ow_ref[sub]
        outs = []
        for h in range(HG):
            rows = slice(h * QT, (h + 1) * QT)
            c = h * N_BRANCH
            outs.append(gt[:, c:c + 1] * o_c[rows] + gt[:, c + 1:c + 2] * o_st[:, rows].T
                        + gt[:, c + 2:c + 3] * o_w[rows])
        o_ref[0, tile_rows, :] = jnp.concatenate(outs, axis=1).astype(BF16)

    front_stage(0, sa_ref, pa_ref, sb_ref, pb_ref)
    for sub in range(1, TILES_PER_STEP):
        front_stage(sub, pre_s_ref.at[2 * sub - 2], pre_p_ref.at[2 * sub - 2],
                    pre_s_ref.at[2 * sub - 1], pre_p_ref.at[2 * sub - 1])
    for sub in range(TILES_PER_STEP):
        selected_stage(sub)


def _overlap_table(seq):
    ncw = seq // CMP_STRIDE
    front = ncw - QT // CMP_STRIDE
    nc = (seq - CMP_BLOCK) // CMP_STRIDE + 1
    ns = seq // SLC_BLOCK
    c_start = jnp.arange(nc) * CMP_STRIDE
    s_start = jnp.arange(ns) * SLC_BLOCK
    ov = (jnp.clip(jnp.minimum(c_start[:, None] + CMP_BLOCK, s_start[None, :] + SLC_BLOCK)
                   - jnp.maximum(c_start[:, None], s_start[None, :]), 0) / CMP_STRIDE).astype(F32)
    return jnp.pad(ov, ((front, ncw - nc), (0, LANES - ns)))


def _nsa(q, gates, ks, vst, kw, vw, kc, vc, tblt, tblw, cb):
    batch, _, seq, _ = q.shape
    G, HG = N_KV_GROUPS, HEADS_PER_GROUP
    ncw = seq // CMP_STRIDE
    crow = kc.shape[2]
    ov = _overlap_table(seq)
    rows = HG * QT
    assert rows == KCHUNK
    width = max(ncw, WIN_KEYS, KCHUNK)
    kvspec = lambda lanes: pl.BlockSpec((1, 1, seq, lanes), lambda g, b, i: (b, g, 0, 0))
    cspec = pl.BlockSpec((1, 1, crow, LANES), lambda g, b, i: (b, g, 0, 0))
    per_group = lambda n, w: pl.BlockSpec((HG, n, QT, w), lambda g, b, i: (g, 0, 0, 0),
                                          pipeline_mode=pl.Buffered(1))
    tq = TILES_PER_STEP * QT
    n_pre = 2 * (TILES_PER_STEP - 1)
    return pl.pallas_call(
        functools.partial(_nsa_body, ncw=ncw),
        grid=(G, batch, seq // tq),
        in_specs=[
            pl.BlockSpec((1, HG, tq, LANES), lambda g, b, i: (b, g, i, 0)),
            pl.BlockSpec((1, tq, LANES), lambda g, b, i: (b, i, g)),
            kvspec(2 * LANES),
            pl.BlockSpec((1, 1, seq // KCHUNK, LANES, KCHUNK), lambda g, b, i: (b, g, 0, 0, 0)),
            kvspec(LANES), kvspec(LANES),
            cspec, cspec,
            _const_spec(ov.shape),
            per_group(N_TILES, QT), per_group(WIN_TILES + 1, QT),
            pl.BlockSpec((HG, QT, ncw), lambda g, b, i: (g, 0, 0), pipeline_mode=pl.Buffered(1)),
        ],
        out_specs=pl.BlockSpec((1, tq, HG * LANES), lambda g, b, i: (b, i, g)),
        out_shape=jax.ShapeDtypeStruct((batch, seq, N_HEADS * LANES), BF16),
        scratch_shapes=[pltpu.VMEM((rows, width), F32), pltpu.VMEM((rows, width), F32),
                        pltpu.VMEM((rows, width), BF16), pltpu.VMEM((rows, width), BF16),
                        pltpu.VMEM((n_pre, rows, width), F32), pltpu.VMEM((n_pre, rows, width), BF16),
                        pltpu.VMEM((TILES_PER_STEP, rows, LANES), F32),
                        pltpu.VMEM((TILES_PER_STEP, rows, LANES), F32),
                        pltpu.VMEM((TILES_PER_STEP, rows, 2 * LANES), BF16),
                        pltpu.VMEM((1, rows), F32), pltpu.VMEM((1, rows), F32), pltpu.VMEM((1, rows), F32),
                        pltpu.VMEM((LANES, rows), F32)],
        compiler_params=_cparams(("arbitrary", "arbitrary", "arbitrary")),
        name="nsa",
    )(q, gates, ks, vst, kw, vw, kc, vc, ov, tblt, tblw, cb)


def _oproj_body(x_ref, a_ref, w_ref, o_ref):
    o_ref[...] = x_ref[...] + _dot(a_ref[...], w_ref[...])


def _o_proj(h, attn, w_o, *, tm=1024):
    rows, d = h.shape
    k = attn.shape[1]
    w = jnp.pad(w_o.reshape(N_HEADS, HEAD_DIM, d), ((0, 0), (0, LANES - HEAD_DIM), (0, 0)))
    w = w.reshape(k, d).astype(BF16)
    return pl.pallas_call(
        _oproj_body,
        grid=(rows // tm,),
        in_specs=[pl.BlockSpec((tm, d), lambda i: (i, 0)),
                  pl.BlockSpec((tm, k), lambda i: (i, 0)),
                  _const_spec((k, d))],
        out_specs=pl.BlockSpec((tm, d), lambda i: (i, 0)),
        out_shape=jax.ShapeDtypeStruct((rows, d), F32),
        compiler_params=_cparams(("parallel",)),
        name="o_proj",
    )(h, attn, w)


def kernel(x, ffn1_norm, ffn1_w_in, ffn1_w_out, mix_norm, ffn2_norm, ffn2_w_in, ffn2_w_out, conv_w_in, conv_w, conv_w_out, attn_w_q, attn_q_norm, attn_w_o, kv_norm, w_kv, k_norm, cmp_pe_k, cmp_pe_v, cmp_w1_k, cmp_w2_k, cmp_w1_v, cmp_w2_v, rel_bias):
    batch, seq, d = x.shape
    depth = ffn1_norm.shape[0]
    n_a = conv_w_in.shape[0]
    assert seq % KCHUNK == 0 and seq // SLC_BLOCK <= LANES and seq >= WIN_KEYS
    h = x.reshape(batch * seq, d)
    kv = tables = None
    for i in range(depth):
        if i == n_a:
            kc_raw, vc_raw, ks, vst, kw, vw = _kv_proj(h, kv_norm, w_kv, k_norm, batch=batch)
            kc, vc = _compress(kc_raw, vc_raw, cmp_pe_k, cmp_pe_v, cmp_w1_k, cmp_w2_k,
                               cmp_w1_v, cmp_w2_v, k_norm)
            kv = (ks, vst, kw, vw, kc, vc)
            tables = _bias_tables(rel_bias, seq)
        h = _ffn(h, ffn1_norm[i], ffn1_w_in[i], ffn1_w_out[i])
        if i < n_a:
            h = _conv_mixer(h, mix_norm[i], conv_w_in[i], conv_w[i], conv_w_out[i], batch=batch)
        else:
            j = i - n_a
            q, gates = _q_proj(h, mix_norm[i], attn_w_q[j], attn_q_norm[j], batch=batch)
            attn = _nsa(q, gates, *kv, *tables)
            h = _o_proj(h, attn.reshape(batch * seq, -1), attn_w_o[j])
        h = _ffn(h, ffn2_norm[i], ffn2_w_in[i], ffn2_w_out[i])
    return h.reshape(batch, seq, d)
```

```python
import functools
import math

import jax
import jax.numpy as jnp
from jax import lax
from jax.experimental import pallas as pl
from jax.experimental.pallas import tpu as pltpu

F32 = jnp.float32
BF16 = jnp.bfloat16

N_HEADS = 16
N_KV_GROUPS = 4
HEADS_PER_GROUP = N_HEADS // N_KV_GROUPS
HEAD_DIM = 64
N_BRANCH = 3
CMP_BLOCK = 32
CMP_STRIDE = 16
SLC_BLOCK = 64
N_SELECT = 16
N_FORCED = 3
WINDOW = 512
N_BUCKETS = 32
MAX_EXACT = N_BUCKETS // 2
MAX_DISTANCE = 4096
EPS = 1e-6
NEG = -1e30
BIG = 1e30
REMOVED = -3e38
LOG2E = math.log2(math.e)

LANES = 128
SUBLANES = 8
VMEM_LIMIT = 56 * 1024 * 1024

QT = 128
TILES_PER_STEP = 4
KCHUNK = 512
N_SUB = KCHUNK // QT
STRIP = 32
WIN_KEYS = WINDOW + QT
N_WSUB = WIN_KEYS // QT
FAR_TILE = int(math.ceil((MAX_EXACT * (MAX_DISTANCE / MAX_EXACT) ** ((N_BUCKETS - MAX_EXACT - 1) / (N_BUCKETS - MAX_EXACT))
                          + QT) / QT))
MASK_TILE = FAR_TILE + 1
N_TILES = FAR_TILE + 2
WIN_TILES = WINDOW // QT + 1
WIN_MASK_TILE = WIN_TILES
FLAG_LANE = HEAD_DIM


def _cparams(sem):
    return pltpu.CompilerParams(dimension_semantics=sem, vmem_limit_bytes=VMEM_LIMIT)


def _const_spec(shape):
    nd = len(shape)
    return pl.BlockSpec(shape, lambda *_: (0,) * nd, pipeline_mode=pl.Buffered(1))


def _rms(x, g, n):
    ms = jnp.sum(x * x, axis=-1, keepdims=True) * (1.0 / n)
    return x * lax.rsqrt(ms + EPS) * g


def _dot(a, b):
    return jnp.dot(a, b, preferred_element_type=F32)


def _dot_t(a, b):
    return lax.dot_general(a, b, (((1,), (1,)), ((), ())), preferred_element_type=F32)


def _ffn_body(*refs, n_chunks, d_model, has_mixer_out):
    if has_mixer_out:
        x_ref, a_ref, wa_ref, g_ref, wg_ref, wu_ref, wo_ref, o_ref, acc_ref = refs
        x = x_ref[...] + _dot(a_ref[...], wa_ref[...])
    else:
        x_ref, g_ref, wg_ref, wu_ref, wo_ref, o_ref, acc_ref = refs
        x = x_ref[...]
    xn = _rms(x, g_ref[...], d_model).astype(BF16)
    acc_ref[...] = jnp.zeros_like(acc_ref)

    def chunk(c, carry):
        gate = _dot(xn, wg_ref[c])
        up = _dot(xn, wu_ref[c])
        act = (gate * (1.0 / (1.0 + jnp.exp(-gate))) * up).astype(BF16)
        acc_ref[...] += _dot(act, wo_ref[c])
        return carry

    lax.fori_loop(0, n_chunks, chunk, 0)
    o_ref[...] = x + 0.5 * acc_ref[...]


def _ffn(h, norm_g, w_in, w_out, mixer_out=None, *, tm=1024, ck=256):
    rows, d = h.shape
    dff = w_out.shape[0]
    nch = dff // ck
    wg = w_in[:, :dff].reshape(d, nch, ck).transpose(1, 0, 2).astype(BF16)
    wu = w_in[:, dff:].reshape(d, nch, ck).transpose(1, 0, 2).astype(BF16)
    wo = w_out.reshape(nch, ck, d).astype(BF16)
    row_spec = lambda width: pl.BlockSpec((tm, width), lambda i: (i, 0))
    operands, specs = [h], [row_spec(d)]
    if mixer_out is not None:
        a, wa = mixer_out
        operands += [a, wa.astype(BF16)]
        specs += [row_spec(a.shape[1]), _const_spec(wa.shape)]
    operands += [norm_g.reshape(1, d), wg, wu, wo]
    specs += [_const_spec((1, d)), _const_spec((nch, d, ck)), _const_spec((nch, d, ck)), _const_spec((nch, ck, d))]
    return pl.pallas_call(
        functools.partial(_ffn_body, n_chunks=nch, d_model=d, has_mixer_out=mixer_out is not None),
        grid=(rows // tm,),
        in_specs=specs,
        out_specs=row_spec(d),
        out_shape=jax.ShapeDtypeStruct((rows, d), F32),
        scratch_shapes=[pltpu.VMEM((tm, d), F32)],
        compiler_params=_cparams(("parallel",)),
        name="ffn",
    )(*operands)


def _conv_body(x_ref, g_ref, win_ref, cw_ref, wout_ref, o_ref, ubuf_ref, *, tm, d_model):
    i = pl.program_id(1)
    x = x_ref[0]
    xn = _rms(x, g_ref[...], d_model).astype(BF16)
    b_gate = _dot(xn, win_ref[0])
    c_gate = _dot(xn, win_ref[1])
    v = _dot(xn, win_ref[2])
    u = c_gate * v

    @pl.when(i == 0)
    def _():
        ubuf_ref[0:SUBLANES, :] = jnp.zeros((SUBLANES, d_model), F32)

    ubuf_ref[SUBLANES:SUBLANES + tm, :] = u
    u1 = ubuf_ref[SUBLANES - 1:SUBLANES - 1 + tm, :]
    u2 = ubuf_ref[SUBLANES - 2:SUBLANES - 2 + tm, :]
    y = cw_ref[2:3, :] * u + cw_ref[1:2, :] * u1 + cw_ref[0:1, :] * u2
    ubuf_ref[0:SUBLANES, :] = u[tm - SUBLANES:, :]
    o_ref[0] = x + _dot((b_gate * y).astype(BF16), wout_ref[...])


def _conv_mixer(h, norm_g, w_in, conv_w, w_out, *, batch, tm=512):
    rows, d = h.shape
    seq = rows // batch
    win = w_in.reshape(d, 3, d).transpose(1, 0, 2).astype(BF16)
    out = pl.pallas_call(
        functools.partial(_conv_body, tm=tm, d_model=d),
        grid=(batch, seq // tm),
        in_specs=[
            pl.BlockSpec((1, tm, d), lambda b, i: (b, i, 0)),
            _const_spec((1, d)),
            _const_spec((3, d, d)),
            _const_spec((conv_w.shape[0], d)),
            _const_spec((d, d)),
        ],
        out_specs=pl.BlockSpec((1, tm, d), lambda b, i: (b, i, 0)),
        out_shape=jax.ShapeDtypeStruct((batch, seq, d), F32),
        scratch_shapes=[pltpu.VMEM((tm + SUBLANES, d), F32)],
        compiler_params=_cparams(("arbitrary", "arbitrary")),
        name="conv_mixer",
    )(h.reshape(batch, seq, d), norm_g.reshape(1, d), win, conv_w, w_out.astype(BF16))
    return out.reshape(rows, d)


def _pad_head_cols(w, n_slices):
    d = w.shape[0]
    w = w.reshape(d, n_slices, HEAD_DIM)
    return jnp.pad(w, ((0, 0), (0, 0), (0, LANES - HEAD_DIM))).reshape(d, n_slices * LANES)


def _pad_lanes(v):
    return jnp.pad(v, [(0, 0)] * (v.ndim - 1) + [(0, LANES - HEAD_DIM)])


def _kvproj_body(x_ref, g_ref, w_ref, wvt_ref, kn_ref, kc_ref, vc_ref, ks_ref, vst_ref, kw_ref, vw_ref,
                 fold_ref, *, tm, d_model):
    i = pl.program_id(1)
    xn = _rms(x_ref[0], g_ref[...], d_model).astype(BF16)
    lane = lax.broadcasted_iota(jnp.int32, (tm, LANES), 1)
    ones_col = jnp.where(lane == FLAG_LANE, 1.0, 0.0)
    ones_rows = jnp.where(lax.broadcasted_iota(jnp.int32, (LANES - HEAD_DIM, tm), 0) == 0, 1.0, 0.0)
    blk = (i * tm + lax.broadcasted_iota(jnp.int32, (tm, LANES), 0)) // SLC_BLOCK
    onehot = jnp.where(lane == blk, 1.0, 0.0).astype(BF16)
    G = N_KV_GROUPS
    pairs = [_dot(xn, w_ref[p]) for p in range(w_ref.shape[0])]
    vst = _dot_t(wvt_ref[...], xn)
    for g in range(G):
        def col(br, kv):
            s = (br * 2 + kv) * G + g
            blk2 = pairs[s // 2] if s % 2 == 0 else pltpu.roll(pairs[s // 2], HEAD_DIM, axis=1)
            return jnp.where(lane < HEAD_DIM, blk2, 0.0)
        for kv, out_ref in ((0, kc_ref), (1, vc_ref)):
            stage = fold_ref.at[2 * g + kv]
            stage[...] = col(0, kv)
            for r in range(CMP_STRIDE):
                out_ref[0, g, :, r * LANES:(r + 1) * LANES] = stage[pl.ds(r, tm // CMP_STRIDE, stride=CMP_STRIDE), :]
        ks = _rms(col(1, 0), kn_ref[1:2, :], HEAD_DIM).astype(BF16)
        ks_ref[0, g] = jnp.concatenate([ks, onehot], axis=1)
        vst_ref[0, g] = jnp.concatenate([vst[g * HEAD_DIM:(g + 1) * HEAD_DIM], ones_rows], axis=0).astype(BF16)
        kw_ref[0, g] = _rms(col(2, 0), kn_ref[2:3, :], HEAD_DIM).astype(BF16)
        vw_ref[0, g] = (col(2, 1) + ones_col).astype(BF16)


def _kv_proj(h, kv_norm, w_kv, k_norm, *, batch, tm=512):
    rows, d = h.shape
    seq = rows // batch
    G = N_KV_GROUPS
    n_sl = N_BRANCH * 2 * G
    n_pair = n_sl * HEAD_DIM // LANES
    w = w_kv.reshape(d, n_pair, LANES).transpose(1, 0, 2).astype(BF16)
    v_sel = (1 * 2 + 1) * G * HEAD_DIM
    wvt = w_kv[:, v_sel:v_sel + G * HEAD_DIM].T.astype(BF16)
    kn = _pad_lanes(k_norm)
    sds = lambda lanes, dt: jax.ShapeDtypeStruct((batch, G, seq, lanes), dt)
    ospec = lambda lanes: pl.BlockSpec((1, G, tm, lanes), lambda b, i: (b, 0, i, 0))
    fsds = jax.ShapeDtypeStruct((batch, G, seq // CMP_STRIDE, CMP_STRIDE * LANES), F32)
    fspec = pl.BlockSpec((1, G, tm // CMP_STRIDE, CMP_STRIDE * LANES), lambda b, i: (b, 0, i, 0))
    kc_raw, vc_raw, ks, vst, kw, vw = pl.pallas_call(
        functools.partial(_kvproj_body, tm=tm, d_model=d),
        grid=(batch, seq // tm),
        in_specs=[
            pl.BlockSpec((1, tm, d), lambda b, i: (b, i, 0)),
            _const_spec((1, d)),
            _const_spec((n_pair, d, LANES)),
            _const_spec((G * HEAD_DIM, d)),
            _const_spec((N_BRANCH, LANES)),
        ],
        out_specs=[fspec, fspec, ospec(2 * LANES),
                   pl.BlockSpec((1, G, LANES, tm), lambda b, i: (b, 0, 0, i)), ospec(LANES), ospec(LANES)],
        out_shape=[fsds, fsds, sds(2 * LANES, BF16),
                   jax.ShapeDtypeStruct((batch, G, LANES, seq), BF16), sds(LANES, BF16), sds(LANES, BF16)],
        scratch_shapes=[pltpu.VMEM((2 * G, tm, LANES), F32)],
        compiler_params=_cparams(("parallel", "parallel")),
        name="kv_proj",
    )(h.reshape(batch, seq, d), kv_norm.reshape(1, d), w, wvt, kn)
    vst = vst.reshape(batch, G, LANES, seq // KCHUNK, KCHUNK).transpose(0, 1, 3, 2, 4)
    return kc_raw, vc_raw, ks, vst, kw, vw


def _gelu_tanh(x):
    return 0.5 * x * (1.0 + jnp.tanh(math.sqrt(2.0 / math.pi) * (x + 0.044715 * (x * x * x))))


def _compress_body(kr_ref, vr_ref, pek_ref, pev_ref, w1k_ref, w2k_ref, w1v_ref, w2v_ref, kn_ref,
                   kc_ref, vc_ref, *, ncw, front):
    row = lax.broadcasted_iota(jnp.int32, (ncw, LANES), 0)
    row_lane = lax.broadcasted_iota(jnp.int32, (ncw, LANES), 1)
    lane = lax.broadcasted_iota(jnp.int32, (front, LANES), 1)

    def mlp(r_ref, pe_ref, w1_ref, w2_ref):
        r = r_ref[0, 0]
        a = _dot((r + pe_ref[0:1, :]).astype(BF16), w1_ref[0])
        b = _dot((r + pe_ref[1:2, :]).astype(BF16), w1_ref[1])
        hid = a + pltpu.roll(b, ncw - 1, axis=0)
        out = _dot(_gelu_tanh(hid).astype(BF16), w2_ref[...])
        return jnp.where(row < ncw - 1, out, 0.0)

    kc = _rms(mlp(kr_ref, pek_ref, w1k_ref, w2k_ref), kn_ref[0:1, :], HEAD_DIM)
    vc = mlp(vr_ref, pev_ref, w1v_ref, w2v_ref)
    kc_ref[0, 0, 0:front, :] = jnp.where(lane == FLAG_LANE, 1.0, 0.0)
    vc_ref[0, 0, 0:front, :] = jnp.zeros((front, LANES), F32)
    kc_ref[0, 0, front:front + ncw, :] = kc
    vc_ref[0, 0, front:front + ncw, :] = vc + jnp.where(row_lane == FLAG_LANE, 1.0, 0.0)


def _compress(kc_raw, vc_raw, pe_k, pe_v, w1_k, w2_k, w1_v, w2_v, k_norm):
    batch, G, ncw, tok = kc_raw.shape
    front = ncw - QT // CMP_STRIDE
    hid = w1_k.shape[1]

    def prep_w1(w1):
        w = w1.reshape(CMP_BLOCK, HEAD_DIM, hid)
        w = jnp.pad(w, ((0, 0), (0, LANES - HEAD_DIM), (0, 0)))
        return w.reshape(2, tok, hid).astype(BF16)

    def prep_pe(pe):
        return _pad_lanes(pe).reshape(2, tok)

    def prep_w2(w2):
        return _pad_lanes(w2).astype(BF16)

    rspec = pl.BlockSpec((1, 1, ncw, tok), lambda b, g: (b, g, 0, 0))
    ospec = pl.BlockSpec((1, 1, front + ncw, LANES), lambda b, g: (b, g, 0, 0))
    osds = jax.ShapeDtypeStruct((batch, G, front + ncw, LANES), F32)
    return pl.pallas_call(
        functools.partial(_compress_body, ncw=ncw, front=front),
        grid=(batch, G),
        in_specs=[rspec, rspec, _const_spec((2, tok)), _const_spec((2, tok)),
                  _const_spec((2, tok, hid)), _const_spec((hid, LANES)),
                  _const_spec((2, tok, hid)), _const_spec((hid, LANES)),
                  _const_spec((N_BRANCH, LANES))],
        out_specs=[ospec, ospec],
        out_shape=[osds, osds],
        compiler_params=_cparams(("parallel", "parallel")),
        name="compress",
    )(kc_raw, vc_raw,
      prep_pe(pe_k), prep_pe(pe_v), prep_w1(w1_k), prep_w2(w2_k), prep_w1(w1_v), prep_w2(w2_v),
      _pad_lanes(k_norm))


def _qproj_body(x_ref, g_ref, wq_ref, wg_ref, qn_ref, q_ref, gt_ref, *, tm, d_model):
    xn = _rms(x_ref[0], g_ref[...], d_model).astype(BF16)
    lane = lax.broadcasted_iota(jnp.int32, (tm, LANES), 1)
    scale = HEAD_DIM ** -0.5 * LOG2E
    for h in range(N_HEADS):
        q = _rms(_dot(xn, wq_ref[h]), qn_ref[...], HEAD_DIM) * scale
        q_ref[0, h] = jnp.where(lane == FLAG_LANE, NEG, q).astype(BF16)
    gates = 1.0 / (1.0 + jnp.exp(-_dot(xn, wg_ref[...])))
    n_gate = HEADS_PER_GROUP * N_BRANCH
    for g in range(N_KV_GROUPS):
        shifted = gates if g == 0 else pltpu.roll(gates, LANES - g * n_gate, axis=1)
        gt_ref[0, :, g * LANES:(g + 1) * LANES] = shifted


def _q_proj(h, norm_g, w_q, q_norm, *, batch, tm=256):
    rows, d = h.shape
    seq = rows // batch
    G, HG = N_KV_GROUPS, HEADS_PER_GROUP
    nq = N_HEADS * HEAD_DIM
    wq = _pad_head_cols(w_q[:, :nq], N_HEADS).reshape(d, N_HEADS, LANES).transpose(1, 0, 2).astype(BF16)
    n_gates = w_q.shape[1] - nq
    assert n_gates <= LANES
    wg = jnp.pad(w_q[:, nq:], ((0, 0), (0, LANES - n_gates))).astype(BF16)
    return pl.pallas_call(
        functools.partial(_qproj_body, tm=tm, d_model=d),
        grid=(batch, seq // tm),
        in_specs=[
            pl.BlockSpec((1, tm, d), lambda b, i: (b, i, 0)),
            _const_spec((1, d)),
            _const_spec((N_HEADS, d, LANES)),
            _const_spec((d, LANES)),
            _const_spec((1, LANES)),
        ],
        out_specs=[pl.BlockSpec((1, N_HEADS, tm, LANES), lambda b, i: (b, 0, i, 0)),
                   pl.BlockSpec((1, tm, G * LANES), lambda b, i: (b, i, 0))],
        out_shape=[jax.ShapeDtypeStruct((batch, N_HEADS, seq, LANES), BF16),
                   jax.ShapeDtypeStruct((batch, seq, G * LANES), F32)],
        compiler_params=_cparams(("parallel", "parallel")),
        name="q_proj",
    )(h.reshape(batch, seq, d), norm_g.reshape(1, d), wq, wg, _pad_lanes(q_norm.reshape(1, HEAD_DIM)))


def _tables_body(thr_ref, rb_ref, tblt_ref, tblw_ref, cb_ref, *, ncw, c_off):
    h = pl.program_id(0)

    def bias_of(rel):
        v = jnp.full(rel.shape, rb_ref[0, h], F32)
        for k in range(1, N_BUCKETS):
            v = jnp.where(rel >= thr_ref[k], rb_ref[k, h], v)
        return v * LOG2E

    row = lax.broadcasted_iota(jnp.int32, (QT, QT), 0)
    col = lax.broadcasted_iota(jnp.int32, (QT, QT), 1)

    def tile_t(d, carry):
        rel = (col - row) + d * QT
        tblt_ref[0, d] = jnp.where(rel < 0, NEG, bias_of(rel))
        return carry

    lax.fori_loop(0, FAR_TILE + 1, tile_t, 0)
    tblt_ref[0, MASK_TILE] = jnp.full((QT, QT), NEG, F32)

    def tile_w(d, carry):
        rel = (row - col) + d * QT
        tblw_ref[0, d] = jnp.where((rel < 0) | (rel >= WINDOW), NEG, bias_of(rel))
        return carry

    lax.fori_loop(0, WIN_TILES, tile_w, 0)
    tblw_ref[0, WIN_MASK_TILE] = jnp.full((QT, QT), NEG, F32)
    relc = (lax.broadcasted_iota(jnp.int32, (QT, ncw), 0)
            - CMP_STRIDE * lax.broadcasted_iota(jnp.int32, (QT, ncw), 1) + c_off)
    cb_ref[0] = jnp.where(relc < 0, NEG, bias_of(relc))


def _bucket_thresholds(seq):
    n = jnp.arange(seq)
    nf = jnp.maximum(n, 1).astype(jnp.float32)
    large = MAX_EXACT + (jnp.log(nf / MAX_EXACT) / math.log(MAX_DISTANCE / MAX_EXACT)
                         * (N_BUCKETS - MAX_EXACT)).astype(jnp.int32)
    large = jnp.minimum(large, N_BUCKETS - 1)
    bucket = jnp.where(n < MAX_EXACT, n, large)
    return jnp.sum(bucket[None, :] < jnp.arange(N_BUCKETS)[:, None], axis=1).astype(jnp.int32)


def _bias_tables(rel_bias, seq):
    ncw = seq // CMP_STRIDE
    front = ncw - QT // CMP_STRIDE
    c_off = CMP_STRIDE * front - (CMP_BLOCK - 1)
    smem = pl.BlockSpec(memory_space=pltpu.SMEM)
    return pl.pallas_call(
        functools.partial(_tables_body, ncw=ncw, c_off=c_off),
        grid=(N_HEADS,),
        in_specs=[smem, smem],
        out_specs=[pl.BlockSpec((1, N_TILES, QT, QT), lambda h: (h, 0, 0, 0)),
                   pl.BlockSpec((1, WIN_TILES + 1, QT, QT), lambda h: (h, 0, 0, 0)),
                   pl.BlockSpec((1, QT, ncw), lambda h: (h, 0, 0))],
        out_shape=[jax.ShapeDtypeStruct((N_HEADS, N_TILES, QT, QT), F32),
                   jax.ShapeDtypeStruct((N_HEADS, WIN_TILES + 1, QT, QT), F32),
                   jax.ShapeDtypeStruct((N_HEADS, QT, ncw), F32)],
        compiler_params=_cparams(("parallel",)),
        name="bias_tables",
    )(_bucket_thresholds(seq), rel_bias)


def _nsa_body(q_ref, gt_ref, ks_ref, vst_ref, kw_ref, vw_ref, kc_ref, vc_ref, ov_ref, tblt_ref, tblw_ref, cb_ref,
              o_ref, sa_ref, sb_ref, pa_ref, pb_ref, pre_s_ref, pre_p_ref, oc_ref, ow_ref, lhs_ref,
              m_ref, ala_ref, alb_ref, acct_ref, *, ncw):
    HG = HEADS_PER_GROUP
    R = HG * QT

    def softmax_strips(s_ref, p_ref, width, bias_fn):
        for st in range(R // STRIP):
            rows = slice(st * STRIP, (st + 1) * STRIP)
            s = s_ref[rows, :width] + bias_fn((st * STRIP) // QT, (st * STRIP) % QT)
            p_ref[rows, :width] = jnp.exp2(s - jnp.max(s, axis=-1, keepdims=True)).astype(BF16)

    def front_stage(sub, sc_ref, pc_ref, sw_ref, pw_ref):
        qi = pl.program_id(2) * TILES_PER_STEP + sub
        q2 = q_ref[0, :, sub * QT:(sub + 1) * QT, :].reshape(R, LANES)

        c0 = pl.multiple_of(qi * (QT // CMP_STRIDE), SUBLANES)
        kc = kc_ref[0, 0, pl.ds(c0, ncw), :].astype(BF16)
        vo = jnp.concatenate([vc_ref[0, 0, pl.ds(c0, ncw), :], ov_ref[pl.ds(c0, ncw), :]],
                             axis=1).astype(BF16)
        sc_ref[:, :ncw] = _dot_t(q2, kc)
        softmax_strips(sc_ref, pc_ref, ncw, lambda h, r: cb_ref[h, r:r + STRIP, :])
        r = _dot(pc_ref[:, :ncw], vo)
        t_row = qi * QT + (lax.broadcasted_iota(jnp.int32, (R, 1), 0) & (QT - 1))
        r = jnp.where(t_row >= CMP_BLOCK - 1, r / r[:, FLAG_LANE:FLAG_LANE + 1], 0.0)
        oc_ref[sub] = r[:, :LANES]
        imp = r[0:QT, LANES:]
        for h in range(1, HG):
            imp = imp + r[h * QT:(h + 1) * QT, LANES:]

        kst = jnp.maximum(qi - WINDOW // QT, 0)
        w0 = pl.multiple_of(kst * QT, QT)
        ids = []
        for u in range(N_WSUB):
            d = qi - kst - u
            ids.append(jnp.where(d < 0, WIN_MASK_TILE, d))
        sw_ref[:, :WIN_KEYS] = _dot_t(q2, kw_ref[0, 0, pl.ds(w0, WIN_KEYS), :])
        softmax_strips(sw_ref, pw_ref, WIN_KEYS,
                       lambda h, r: jnp.concatenate([tblw_ref[h, t, r:r + STRIP, :] for t in ids], axis=1))
        acc_w = _dot(pw_ref[:, :WIN_KEYS], vw_ref[0, 0, pl.ds(w0, WIN_KEYS), :])
        ow_ref[sub] = acc_w / acc_w[:, FLAG_LANE:FLAG_LANE + 1]

        blk = lax.broadcasted_iota(jnp.int32, (LANES, QT), 0)
        blk_t = (qi * QT + lax.broadcasted_iota(jnp.int32, (LANES, QT), 1)) // SLC_BLOCK
        forced = (blk == 0) | (blk == blk_t) | (blk == blk_t - 1)
        score = jnp.where(forced, REMOVED, jnp.where(blk <= blk_t, imp.T, NEG))
        sel = jnp.where(forced, 1.0, 0.0)
        for _ in range(N_SELECT - N_FORCED):
            top = jnp.max(score, axis=0, keepdims=True)
            first = jnp.min(jnp.where(score == top, blk, LANES), axis=0, keepdims=True)
            hit = blk == first
            sel = jnp.where(hit, 1.0, sel)
            score = jnp.where(hit, REMOVED, score)
        sel_neg = ((sel - 1.0) * BIG).T.astype(BF16)
        lhs_ref[sub] = jnp.concatenate([q2, jnp.concatenate([sel_neg] * HG, axis=0)], axis=1)

    def selected_stage(sub):
        qi = pl.program_id(2) * TILES_PER_STEP + sub
        m_ref[...] = jnp.full((1, R), REMOVED, F32)
        acct_ref[...] = jnp.zeros((LANES, R), F32)
        n_chunks = qi // N_SUB + 1

        def scores(s_ref, kj):
            k0 = pl.multiple_of(kj * KCHUNK, KCHUNK)
            s_ref[:, :R] = _dot_t(ks_ref[0, 0, pl.ds(k0, KCHUNK), :], lhs_ref[sub])

        def accumulate(s_ref, p_ref, al_ref, kj):
            for h in range(HG):
                cols = slice(h * QT, (h + 1) * QT)
                mx = m_ref[:, cols]
                for u in range(N_SUB):
                    keys = slice(u * QT, (u + 1) * QT)
                    d = qi - kj * N_SUB - u
                    tile = jnp.where(d < 0, MASK_TILE, jnp.minimum(d, FAR_TILE))
                    sb = s_ref[keys, cols] + tblt_ref[h, tile]
                    s_ref[keys, cols] = sb
                    mx = jnp.maximum(mx, jnp.max(sb, axis=0, keepdims=True))
                al_ref[:, cols] = jnp.exp2(m_ref[:, cols] - mx)
                m_ref[:, cols] = mx
                p_ref[:, cols] = jnp.exp2(s_ref[:, cols] - mx).astype(BF16)
            acct_ref[...] = al_ref[...] * acct_ref[...] + _dot(vst_ref[0, 0, kj], p_ref[:, :R])

        scores(sa_ref, 0)

        def chunk_pair(j, carry):
            a = 2 * j
            scores(sb_ref, a + 1)
            accumulate(sa_ref, pa_ref, ala_ref, a)
            scores(sa_ref, jnp.minimum(a + 2, n_chunks - 1))
            accumulate(sb_ref, pb_ref, alb_ref, a + 1)
            return carry

        lax.fori_loop(0, n_chunks // 2, chunk_pair, 0)

        @pl.when(n_chunks % 2 == 1)
        def _():
            accumulate(sa_ref, pa_ref, ala_ref, n_chunks - 1)

        acct = acct_ref[...]
        o_st = acct / acct[FLAG_LANE:FLAG_LANE + 1, :]

        tile_rows = slice(sub * QT, (sub + 1) * QT)
        gt = gt_ref[0, tile_rows, :]
        o_c = oc_ref[sub]
        o_w = ow_ref[sub]
        outs = []
        for h in range(HG):
            rows = slice(h * QT, (h + 1) * QT)
            c = h * N_BRANCH
            outs.append(gt[:, c:c + 1] * o_c[rows] + gt[:, c + 1:c + 2] * o_st[:, rows].T
                        + gt[:, c + 2:c + 3] * o_w[rows])
        low = lax.broadcasted_iota(jnp.int32, (QT, LANES), 1) < HEAD_DIM
        packed = [jnp.where(low, outs[h], pltpu.roll(outs[h + 1], HEAD_DIM, axis=1)) for h in range(0, HG, 2)]
        o_ref[0, tile_rows, :] = jnp.concatenate(packed, axis=1).astype(BF16)

    front_stage(0, sa_ref, pa_ref, sb_ref, pb_ref)
    for sub in range(1, TILES_PER_STEP):
        front_stage(sub, pre_s_ref.at[2 * sub - 2], pre_p_ref.at[2 * sub - 2],
                    pre_s_ref.at[2 * sub - 1], pre_p_ref.at[2 * sub - 1])
    for sub in range(TILES_PER_STEP):
        selected_stage(sub)


def _overlap_table(seq):
    ncw = seq // CMP_STRIDE
    front = ncw - QT // CMP_STRIDE
    nc = (seq - CMP_BLOCK) // CMP_STRIDE + 1
    ns = seq // SLC_BLOCK
    c_start = jnp.arange(nc) * CMP_STRIDE
    s_start = jnp.arange(ns) * SLC_BLOCK
    ov = (jnp.clip(jnp.minimum(c_start[:, None] + CMP_BLOCK, s_start[None, :] + SLC_BLOCK)
                   - jnp.maximum(c_start[:, None], s_start[None, :]), 0) / CMP_STRIDE).astype(F32)
    return jnp.pad(ov, ((front, ncw - nc), (0, LANES - ns)))


def _nsa(q, gates, ks, vst, kw, vw, kc, vc, tblt, tblw, cb):
    batch, _, seq, _ = q.shape
    G, HG = N_KV_GROUPS, HEADS_PER_GROUP
    ncw = seq // CMP_STRIDE
    crow = kc.shape[2]
    ov = _overlap_table(seq)
    rows = HG * QT
    assert rows == KCHUNK
    width = max(ncw, WIN_KEYS, KCHUNK)
    kvspec = lambda lanes: pl.BlockSpec((1, 1, seq, lanes), lambda g, b, i: (b, g, 0, 0))
    cspec = pl.BlockSpec((1, 1, crow, LANES), lambda g, b, i: (b, g, 0, 0))
    per_group = lambda n, w: pl.BlockSpec((HG, n, QT, w), lambda g, b, i: (g, 0, 0, 0),
                                          pipeline_mode=pl.Buffered(1))
    tq = TILES_PER_STEP * QT
    n_pre = 2 * (TILES_PER_STEP - 1)
    return pl.pallas_call(
        functools.partial(_nsa_body, ncw=ncw),
        grid=(G, batch, seq // tq),
        in_specs=[
            pl.BlockSpec((1, HG, tq, LANES), lambda g, b, i: (b, g, i, 0)),
            pl.BlockSpec((1, tq, LANES), lambda g, b, i: (b, i, g)),
            kvspec(2 * LANES),
            pl.BlockSpec((1, 1, seq // KCHUNK, LANES, KCHUNK), lambda g, b, i: (b, g, 0, 0, 0)),
            kvspec(LANES), kvspec(LANES),
            cspec, cspec,
            _const_spec(ov.shape),
            per_group(N_TILES, QT), per_group(WIN_TILES + 1, QT),
            pl.BlockSpec((HG, QT, ncw), lambda g, b, i: (g, 0, 0), pipeline_mode=pl.Buffered(1)),
        ],
        out_specs=pl.BlockSpec((1, tq, HG * HEAD_DIM), lambda g, b, i: (b, i, g)),
        out_shape=jax.ShapeDtypeStruct((batch, seq, N_HEADS * HEAD_DIM), BF16),
        scratch_shapes=[pltpu.VMEM((rows, width), F32), pltpu.VMEM((rows, width), F32),
                        pltpu.VMEM((rows, width), BF16), pltpu.VMEM((rows, width), BF16),
                        pltpu.VMEM((n_pre, rows, width), F32), pltpu.VMEM((n_pre, rows, width), BF16),
                        pltpu.VMEM((TILES_PER_STEP, rows, LANES), F32),
                        pltpu.VMEM((TILES_PER_STEP, rows, LANES), F32),
                        pltpu.VMEM((TILES_PER_STEP, rows, 2 * LANES), BF16),
                        pltpu.VMEM((1, rows), F32), pltpu.VMEM((1, rows), F32), pltpu.VMEM((1, rows), F32),
                        pltpu.VMEM((LANES, rows), F32)],
        compiler_params=_cparams(("arbitrary", "arbitrary", "arbitrary")),
        name="nsa",
    )(q, gates, ks, vst, kw, vw, kc, vc, ov, tblt, tblw, cb)


def kernel(x, ffn1_norm, ffn1_w_in, ffn1_w_out, mix_norm, ffn2_norm, ffn2_w_in, ffn2_w_out, conv_w_in, conv_w, conv_w_out, attn_w_q, attn_q_norm, attn_w_o, kv_norm, w_kv, k_norm, cmp_pe_k, cmp_pe_v, cmp_w1_k, cmp_w2_k, cmp_w1_v, cmp_w2_v, rel_bias):
    batch, seq, d = x.shape
    depth = ffn1_norm.shape[0]
    n_a = conv_w_in.shape[0]
    assert seq % KCHUNK == 0 and seq // SLC_BLOCK <= LANES and seq >= WIN_KEYS
    h = x.reshape(batch * seq, d)
    kv = tables = None
    for i in range(depth):
        if i == n_a:
            kc_raw, vc_raw, ks, vst, kw, vw = _kv_proj(h, kv_norm, w_kv, k_norm, batch=batch)
            kc, vc = _compress(kc_raw, vc_raw, cmp_pe_k, cmp_pe_v, cmp_w1_k, cmp_w2_k,
                               cmp_w1_v, cmp_w2_v, k_norm)
            kv = (ks, vst, kw, vw, kc, vc)
            tables = _bias_tables(rel_bias, seq)
        h = _ffn(h, ffn1_norm[i], ffn1_w_in[i], ffn1_w_out[i])
        mixer_out = None
        if i < n_a:
            h = _conv_mixer(h, mix_norm[i], conv_w_in[i], conv_w[i], conv_w_out[i], batch=batch)
        else:
            j = i - n_a
            q, gates = _q_proj(h, mix_norm[i], attn_w_q[j], attn_q_norm[j], batch=batch)
            attn = _nsa(q, gates, *kv, *tables)
            mixer_out = (attn.reshape(batch * seq, -1), attn_w_o[j])
        h = _ffn(h, ffn2_norm[i], ffn2_w_in[i], ffn2_w_out[i], mixer_out)
    return h.reshape(batch, seq, d)
```

```python
import functools
import math

import jax
import jax.numpy as jnp
from jax import lax
from jax.experimental import pallas as pl
from jax.experimental.pallas import tpu as pltpu

F32 = jnp.float32
BF16 = jnp.bfloat16

N_HEADS = 16
N_KV_GROUPS = 4
HEADS_PER_GROUP = N_HEADS // N_KV_GROUPS
HEAD_DIM = 64
N_BRANCH = 3
CMP_BLOCK = 32
CMP_STRIDE = 16
SLC_BLOCK = 64
N_SELECT = 16
N_FORCED = 3
WINDOW = 512
N_BUCKETS = 32
MAX_EXACT = N_BUCKETS // 2
MAX_DISTANCE = 4096
EPS = 1e-6
NEG = -1e30
BIG = 1e30
REMOVED = -3e38
LOG2E = math.log2(math.e)

LANES = 128
SUBLANES = 8
VMEM_LIMIT = 56 * 1024 * 1024

QT = 128
TILES_PER_STEP = 4
KCHUNK = 512
N_SUB = KCHUNK // QT
STRIP = 32
WIN_KEYS = WINDOW + QT
N_WSUB = WIN_KEYS // QT
FAR_TILE = int(math.ceil((MAX_EXACT * (MAX_DISTANCE / MAX_EXACT) ** ((N_BUCKETS - MAX_EXACT - 1) / (N_BUCKETS - MAX_EXACT))
                          + QT) / QT))
MASK_TILE = FAR_TILE + 1
N_TILES = FAR_TILE + 2
WIN_TILES = WINDOW // QT + 1
WIN_MASK_TILE = WIN_TILES
FLAG_LANE = HEAD_DIM


def _cparams(sem):
    return pltpu.CompilerParams(dimension_semantics=sem, vmem_limit_bytes=VMEM_LIMIT)


def _const_spec(shape):
    nd = len(shape)
    return pl.BlockSpec(shape, lambda *_: (0,) * nd, pipeline_mode=pl.Buffered(1))


def _rms(x, g, n):
    ms = jnp.sum(x * x, axis=-1, keepdims=True) * (1.0 / n)
    return x * lax.rsqrt(ms + EPS) * g


def _dot(a, b):
    return jnp.dot(a, b, preferred_element_type=F32)


def _dot_t(a, b):
    return lax.dot_general(a, b, (((1,), (1,)), ((), ())), preferred_element_type=F32)


def _ffn_body(*refs, n_chunks, d_model, has_mixer_out):
    if has_mixer_out:
        x_ref, a_ref, wa_ref, g_ref, wg_ref, wu_ref, wo_ref, o_ref, acc_ref = refs
        x = x_ref[...] + _dot(a_ref[...], wa_ref[...])
    else:
        x_ref, g_ref, wg_ref, wu_ref, wo_ref, o_ref, acc_ref = refs
        x = x_ref[...]
    xn = _rms(x, g_ref[...], d_model).astype(BF16)
    acc_ref[...] = jnp.zeros_like(acc_ref)

    def chunk(c, carry):
        gate = _dot(xn, wg_ref[c])
        up = _dot(xn, wu_ref[c])
        act = (gate * (1.0 / (1.0 + jnp.exp(-gate))) * up).astype(BF16)
        acc_ref[...] += _dot(act, wo_ref[c])
        return carry

    lax.fori_loop(0, n_chunks, chunk, 0)
    o_ref[...] = x + 0.5 * acc_ref[...]


def _ffn(h, norm_g, w_in, w_out, mixer_out=None, *, tm=1024, ck=256):
    rows, d = h.shape
    dff = w_out.shape[0]
    nch = dff // ck
    wg = w_in[:, :dff].reshape(d, nch, ck).transpose(1, 0, 2).astype(BF16)
    wu = w_in[:, dff:].reshape(d, nch, ck).transpose(1, 0, 2).astype(BF16)
    wo = w_out.reshape(nch, ck, d).astype(BF16)
    row_spec = lambda width: pl.BlockSpec((tm, width), lambda i: (i, 0))
    operands, specs = [h], [row_spec(d)]
    if mixer_out is not None:
        a, wa = mixer_out
        operands += [a, wa.astype(BF16)]
        specs += [row_spec(a.shape[1]), _const_spec(wa.shape)]
    operands += [norm_g.reshape(1, d), wg, wu, wo]
    specs += [_const_spec((1, d)), _const_spec((nch, d, ck)), _const_spec((nch, d, ck)), _const_spec((nch, ck, d))]
    return pl.pallas_call(
        functools.partial(_ffn_body, n_chunks=nch, d_model=d, has_mixer_out=mixer_out is not None),
        grid=(rows // tm,),
        in_specs=specs,
        out_specs=row_spec(d),
        out_shape=jax.ShapeDtypeStruct((rows, d), F32),
        scratch_shapes=[pltpu.VMEM((tm, d), F32)],
        compiler_params=_cparams(("parallel",)),
        name="ffn",
    )(*operands)


def _conv_body(x_ref, g_ref, win_ref, cw_ref, wout_ref, o_ref, ubuf_ref, *, tm, d_model):
    i = pl.program_id(1)
    x = x_ref[0]
    xn = _rms(x, g_ref[...], d_model).astype(BF16)
    b_gate = _dot(xn, win_ref[0])
    c_gate = _dot(xn, win_ref[1])
    v = _dot(xn, win_ref[2])
    u = c_gate * v

    @pl.when(i == 0)
    def _():
        ubuf_ref[0:SUBLANES, :] = jnp.zeros((SUBLANES, d_model), F32)

    ubuf_ref[SUBLANES:SUBLANES + tm, :] = u
    u1 = ubuf_ref[SUBLANES - 1:SUBLANES - 1 + tm, :]
    u2 = ubuf_ref[SUBLANES - 2:SUBLANES - 2 + tm, :]
    y = cw_ref[2:3, :] * u + cw_ref[1:2, :] * u1 + cw_ref[0:1, :] * u2
    ubuf_ref[0:SUBLANES, :] = u[tm - SUBLANES:, :]
    o_ref[0] = x + _dot((b_gate * y).astype(BF16), wout_ref[...])


def _conv_mixer(h, norm_g, w_in, conv_w, w_out, *, batch, tm=512):
    rows, d = h.shape
    seq = rows // batch
    win = w_in.reshape(d, 3, d).transpose(1, 0, 2).astype(BF16)
    out = pl.pallas_call(
        functools.partial(_conv_body, tm=tm, d_model=d),
        grid=(batch, seq // tm),
        in_specs=[
            pl.BlockSpec((1, tm, d), lambda b, i: (b, i, 0)),
            _const_spec((1, d)),
            _const_spec((3, d, d)),
            _const_spec((conv_w.shape[0], d)),
            _const_spec((d, d)),
        ],
        out_specs=pl.BlockSpec((1, tm, d), lambda b, i: (b, i, 0)),
        out_shape=jax.ShapeDtypeStruct((batch, seq, d), F32),
        scratch_shapes=[pltpu.VMEM((tm + SUBLANES, d), F32)],
        compiler_params=_cparams(("arbitrary", "arbitrary")),
        name="conv_mixer",
    )(h.reshape(batch, seq, d), norm_g.reshape(1, d), win, conv_w, w_out.astype(BF16))
    return out.reshape(rows, d)


def _pad_head_cols(w, n_slices):
    d = w.shape[0]
    w = w.reshape(d, n_slices, HEAD_DIM)
    return jnp.pad(w, ((0, 0), (0, 0), (0, LANES - HEAD_DIM))).reshape(d, n_slices * LANES)


def _pad_lanes(v):
    return jnp.pad(v, [(0, 0)] * (v.ndim - 1) + [(0, LANES - HEAD_DIM)])


def _kvproj_body(x_ref, g_ref, w_ref, wvt_ref, kn_ref, kc_ref, vc_ref, ks_ref, vst_ref, kw_ref, vw_ref,
                 fold_ref, *, tm, d_model):
    i = pl.program_id(1)
    xn = _rms(x_ref[0], g_ref[...], d_model).astype(BF16)
    lane = lax.broadcasted_iota(jnp.int32, (tm, LANES), 1)
    ones_col = jnp.where(lane == FLAG_LANE, 1.0, 0.0)
    ones_rows = jnp.where(lax.broadcasted_iota(jnp.int32, (LANES - HEAD_DIM, tm), 0) == 0, 1.0, 0.0)
    blk = (i * tm + lax.broadcasted_iota(jnp.int32, (tm, LANES), 0)) // SLC_BLOCK
    onehot = jnp.where(lane == blk, 1.0, 0.0).astype(BF16)
    G = N_KV_GROUPS
    wide = [_dot(xn, w_ref[p]) for p in range(w_ref.shape[0])]
    pairs = [w2[:, c:c + LANES] for w2 in wide for c in (0, LANES)]
    vst = _dot_t(wvt_ref[...], xn)
    for g in range(G):
        def col(br, kv):
            s = (br * 2 + kv) * G + g
            blk2 = pairs[s // 2] if s % 2 == 0 else pltpu.roll(pairs[s // 2], HEAD_DIM, axis=1)
            return jnp.where(lane < HEAD_DIM, blk2, 0.0)
        for kv, out_ref in ((0, kc_ref), (1, vc_ref)):
            stage = fold_ref.at[2 * g + kv]
            stage[...] = col(0, kv)
            for r in range(CMP_STRIDE):
                out_ref[0, g, :, r * LANES:(r + 1) * LANES] = stage[pl.ds(r, tm // CMP_STRIDE, stride=CMP_STRIDE), :]
        ks = _rms(col(1, 0), kn_ref[1:2, :], HEAD_DIM).astype(BF16)
        ks_ref[0, g] = jnp.concatenate([ks, onehot], axis=1)
        vst_ref[0, g] = jnp.concatenate([vst[g * HEAD_DIM:(g + 1) * HEAD_DIM], ones_rows], axis=0).astype(BF16)
        kw_ref[0, g] = _rms(col(2, 0), kn_ref[2:3, :], HEAD_DIM).astype(BF16)
        vw_ref[0, g] = (col(2, 1) + ones_col).astype(BF16)


def _kv_proj(h, kv_norm, w_kv, k_norm, *, batch, tm=512):
    rows, d = h.shape
    seq = rows // batch
    G = N_KV_GROUPS
    n_sl = N_BRANCH * 2 * G
    n_wide = n_sl * HEAD_DIM // (2 * LANES)
    w = w_kv.reshape(d, n_wide, 2 * LANES).transpose(1, 0, 2).astype(BF16)
    v_sel = (1 * 2 + 1) * G * HEAD_DIM
    wvt = w_kv[:, v_sel:v_sel + G * HEAD_DIM].T.astype(BF16)
    kn = _pad_lanes(k_norm)
    sds = lambda lanes, dt: jax.ShapeDtypeStruct((batch, G, seq, lanes), dt)
    ospec = lambda lanes: pl.BlockSpec((1, G, tm, lanes), lambda b, i: (b, 0, i, 0))
    fsds = jax.ShapeDtypeStruct((batch, G, seq // CMP_STRIDE, CMP_STRIDE * LANES), F32)
    fspec = pl.BlockSpec((1, G, tm // CMP_STRIDE, CMP_STRIDE * LANES), lambda b, i: (b, 0, i, 0))
    kc_raw, vc_raw, ks, vst, kw, vw = pl.pallas_call(
        functools.partial(_kvproj_body, tm=tm, d_model=d),
        grid=(batch, seq // tm),
        in_specs=[
            pl.BlockSpec((1, tm, d), lambda b, i: (b, i, 0)),
            _const_spec((1, d)),
            _const_spec((n_wide, d, 2 * LANES)),
            _const_spec((G * HEAD_DIM, d)),
            _const_spec((N_BRANCH, LANES)),
        ],
        out_specs=[fspec, fspec, ospec(2 * LANES),
                   pl.BlockSpec((1, G, LANES, tm), lambda b, i: (b, 0, 0, i)), ospec(LANES), ospec(LANES)],
        out_shape=[fsds, fsds, sds(2 * LANES, BF16),
                   jax.ShapeDtypeStruct((batch, G, LANES, seq), BF16), sds(LANES, BF16), sds(LANES, BF16)],
        scratch_shapes=[pltpu.VMEM((2 * G, tm, LANES), F32)],
        compiler_params=_cparams(("parallel", "parallel")),
        name="kv_proj",
    )(h.reshape(batch, seq, d), kv_norm.reshape(1, d), w, wvt, kn)
    vst = vst.reshape(batch, G, LANES, seq // KCHUNK, KCHUNK).transpose(0, 1, 3, 2, 4)
    return kc_raw, vc_raw, ks, vst, kw, vw


def _gelu_tanh(x):
    return 0.5 * x * (1.0 + jnp.tanh(math.sqrt(2.0 / math.pi) * (x + 0.044715 * (x * x * x))))


def _compress_body(kr_ref, vr_ref, pek_ref, pev_ref, w1k_ref, w2k_ref, w1v_ref, w2v_ref, kn_ref,
                   kc_ref, vc_ref, *, ncw, front):
    row = lax.broadcasted_iota(jnp.int32, (ncw, LANES), 0)
    row_lane = lax.broadcasted_iota(jnp.int32, (ncw, LANES), 1)
    lane = lax.broadcasted_iota(jnp.int32, (front, LANES), 1)

    def mlp(r_ref, pe_ref, w1_ref, w2_ref):
        r = r_ref[0, 0]
        a = _dot((r + pe_ref[0:1, :]).astype(BF16), w1_ref[0])
        b = _dot((r + pe_ref[1:2, :]).astype(BF16), w1_ref[1])
        hid = a + pltpu.roll(b, ncw - 1, axis=0)
        out = _dot(_gelu_tanh(hid).astype(BF16), w2_ref[...])
        return jnp.where(row < ncw - 1, out, 0.0)

    kc = _rms(mlp(kr_ref, pek_ref, w1k_ref, w2k_ref), kn_ref[0:1, :], HEAD_DIM)
    vc = mlp(vr_ref, pev_ref, w1v_ref, w2v_ref)
    kc_ref[0, 0, 0:front, :] = jnp.where(lane == FLAG_LANE, 1.0, 0.0)
    vc_ref[0, 0, 0:front, :] = jnp.zeros((front, LANES), F32)
    kc_ref[0, 0, front:front + ncw, :] = kc
    vc_ref[0, 0, front:front + ncw, :] = vc + jnp.where(row_lane == FLAG_LANE, 1.0, 0.0)


def _compress(kc_raw, vc_raw, pe_k, pe_v, w1_k, w2_k, w1_v, w2_v, k_norm):
    batch, G, ncw, tok = kc_raw.shape
    front = ncw - QT // CMP_STRIDE
    hid = w1_k.shape[1]

    def prep_w1(w1):
        w = w1.reshape(CMP_BLOCK, HEAD_DIM, hid)
        w = jnp.pad(w, ((0, 0), (0, LANES - HEAD_DIM), (0, 0)))
        return w.reshape(2, tok, hid).astype(BF16)

    def prep_pe(pe):
        return _pad_lanes(pe).reshape(2, tok)

    def prep_w2(w2):
        return _pad_lanes(w2).astype(BF16)

    rspec = pl.BlockSpec((1, 1, ncw, tok), lambda b, g: (b, g, 0, 0))
    ospec = pl.BlockSpec((1, 1, front + ncw, LANES), lambda b, g: (b, g, 0, 0))
    osds = jax.ShapeDtypeStruct((batch, G, front + ncw, LANES), F32)
    return pl.pallas_call(
        functools.partial(_compress_body, ncw=ncw, front=front),
        grid=(batch, G),
        in_specs=[rspec, rspec, _const_spec((2, tok)), _const_spec((2, tok)),
                  _const_spec((2, tok, hid)), _const_spec((hid, LANES)),
                  _const_spec((2, tok, hid)), _const_spec((hid, LANES)),
                  _const_spec((N_BRANCH, LANES))],
        out_specs=[ospec, ospec],
        out_shape=[osds, osds],
        compiler_params=_cparams(("parallel", "parallel")),
        name="compress",
    )(kc_raw, vc_raw,
      prep_pe(pe_k), prep_pe(pe_v), prep_w1(w1_k), prep_w2(w2_k), prep_w1(w1_v), prep_w2(w2_v),
      _pad_lanes(k_norm))


def _qproj_body(x_ref, g_ref, wq_ref, wg_ref, qn_ref, q_ref, gt_ref, *, tm, d_model):
    xn = _rms(x_ref[0], g_ref[...], d_model).astype(BF16)
    lane = lax.broadcasted_iota(jnp.int32, (tm, LANES), 1)
    scale = HEAD_DIM ** -0.5 * LOG2E
    for p in range(N_HEADS // 2):
        pair = _dot(xn, wq_ref[p])
        for half in range(2):
            q = _rms(pair[:, half * LANES:(half + 1) * LANES], qn_ref[...], HEAD_DIM) * scale
            q_ref[0, 2 * p + half] = jnp.where(lane == FLAG_LANE, NEG, q).astype(BF16)
    gates = 1.0 / (1.0 + jnp.exp(-_dot(xn, wg_ref[...])))
    n_gate = HEADS_PER_GROUP * N_BRANCH
    for g in range(N_KV_GROUPS):
        shifted = gates if g == 0 else pltpu.roll(gates, LANES - g * n_gate, axis=1)
        gt_ref[0, :, g * LANES:(g + 1) * LANES] = shifted


def _q_proj(h, norm_g, w_q, q_norm, *, batch, tm=256):
    rows, d = h.shape
    seq = rows // batch
    G, HG = N_KV_GROUPS, HEADS_PER_GROUP
    nq = N_HEADS * HEAD_DIM
    wq = _pad_head_cols(w_q[:, :nq], N_HEADS).reshape(d, N_HEADS // 2, 2 * LANES).transpose(1, 0, 2).astype(BF16)
    n_gates = w_q.shape[1] - nq
    assert n_gates <= LANES
    wg = jnp.pad(w_q[:, nq:], ((0, 0), (0, LANES - n_gates))).astype(BF16)
    return pl.pallas_call(
        functools.partial(_qproj_body, tm=tm, d_model=d),
        grid=(batch, seq // tm),
        in_specs=[
            pl.BlockSpec((1, tm, d), lambda b, i: (b, i, 0)),
            _const_spec((1, d)),
            _const_spec((N_HEADS // 2, d, 2 * LANES)),
            _const_spec((d, LANES)),
            _const_spec((1, LANES)),
        ],
        out_specs=[pl.BlockSpec((1, N_HEADS, tm, LANES), lambda b, i: (b, 0, i, 0)),
                   pl.BlockSpec((1, tm, G * LANES), lambda b, i: (b, i, 0))],
        out_shape=[jax.ShapeDtypeStruct((batch, N_HEADS, seq, LANES), BF16),
                   jax.ShapeDtypeStruct((batch, seq, G * LANES), F32)],
        compiler_params=_cparams(("parallel", "parallel")),
        name="q_proj",
    )(h.reshape(batch, seq, d), norm_g.reshape(1, d), wq, wg, _pad_lanes(q_norm.reshape(1, HEAD_DIM)))


def _tables_body(thr_ref, rb_ref, tblt_ref, tblw_ref, cb_ref, *, ncw, c_off):
    h = pl.program_id(0)

    def bias_of(rel):
        v = jnp.full(rel.shape, rb_ref[0, h], F32)
        for k in range(1, N_BUCKETS):
            v = jnp.where(rel >= thr_ref[k], rb_ref[k, h], v)
        return v * LOG2E

    row = lax.broadcasted_iota(jnp.int32, (QT, QT), 0)
    col = lax.broadcasted_iota(jnp.int32, (QT, QT), 1)

    def tile_t(d, carry):
        rel = (col - row) + d * QT
        tblt_ref[0, d] = jnp.where(rel < 0, NEG, bias_of(rel))
        return carry

    lax.fori_loop(0, FAR_TILE + 1, tile_t, 0)
    tblt_ref[0, MASK_TILE] = jnp.full((QT, QT), NEG, F32)

    def tile_w(d, carry):
        rel = (row - col) + d * QT
        tblw_ref[0, d] = jnp.where((rel < 0) | (rel >= WINDOW), NEG, bias_of(rel))
        return carry

    lax.fori_loop(0, WIN_TILES, tile_w, 0)
    tblw_ref[0, WIN_MASK_TILE] = jnp.full((QT, QT), NEG, F32)
    relc = (lax.broadcasted_iota(jnp.int32, (QT, ncw), 0)
            - CMP_STRIDE * lax.broadcasted_iota(jnp.int32, (QT, ncw), 1) + c_off)
    cb_ref[0] = jnp.where(relc < 0, NEG, bias_of(relc))


def _bucket_thresholds(seq):
    n = jnp.arange(seq)
    nf = jnp.maximum(n, 1).astype(jnp.float32)
    large = MAX_EXACT + (jnp.log(nf / MAX_EXACT) / math.log(MAX_DISTANCE / MAX_EXACT)
                         * (N_BUCKETS - MAX_EXACT)).astype(jnp.int32)
    large = jnp.minimum(large, N_BUCKETS - 1)
    bucket = jnp.where(n < MAX_EXACT, n, large)
    return jnp.sum(bucket[None, :] < jnp.arange(N_BUCKETS)[:, None], axis=1).astype(jnp.int32)


def _bias_tables(rel_bias, seq):
    ncw = seq // CMP_STRIDE
    front = ncw - QT // CMP_STRIDE
    c_off = CMP_STRIDE * front - (CMP_BLOCK - 1)
    smem = pl.BlockSpec(memory_space=pltpu.SMEM)
    return pl.pallas_call(
        functools.partial(_tables_body, ncw=ncw, c_off=c_off),
        grid=(N_HEADS,),
        in_specs=[smem, smem],
        out_specs=[pl.BlockSpec((1, N_TILES, QT, QT), lambda h: (h, 0, 0, 0)),
                   pl.BlockSpec((1, WIN_TILES + 1, QT, QT), lambda h: (h, 0, 0, 0)),
                   pl.BlockSpec((1, QT, ncw), lambda h: (h, 0, 0))],
        out_shape=[jax.ShapeDtypeStruct((N_HEADS, N_TILES, QT, QT), F32),
                   jax.ShapeDtypeStruct((N_HEADS, WIN_TILES + 1, QT, QT), F32),
                   jax.ShapeDtypeStruct((N_HEADS, QT, ncw), F32)],
        compiler_params=_cparams(("parallel",)),
        name="bias_tables",
    )(_bucket_thresholds(seq), rel_bias)


def _nsa_body(q_ref, gt_ref, ks_ref, vst_ref, kw_ref, vw_ref, kc_ref, vc_ref, ov_ref, tblt_ref, tblw_ref, cb_ref,
              o_ref, sa_ref, sb_ref, pa_ref, pb_ref, pre_s_ref, pre_p_ref, oc_ref, ow_ref, lhs_ref,
              m_ref, ala_ref, alb_ref, acct_ref, *, ncw):
    HG = HEADS_PER_GROUP
    R = HG * QT

    def softmax_strips(s_ref, p_ref, width, bias_fn):
        for st in range(R // STRIP):
            rows = slice(st * STRIP, (st + 1) * STRIP)
            s = s_ref[rows, :width] + bias_fn((st * STRIP) // QT, (st * STRIP) % QT)
            p_ref[rows, :width] = jnp.exp2(s - jnp.max(s, axis=-1, keepdims=True)).astype(BF16)

    def front_stage(sub, sc_ref, pc_ref, sw_ref, pw_ref):
        qi = pl.program_id(2) * TILES_PER_STEP + sub
        q2 = q_ref[0, :, sub * QT:(sub + 1) * QT, :].reshape(R, LANES)

        c0 = pl.multiple_of(qi * (QT // CMP_STRIDE), SUBLANES)
        kc = kc_ref[0, 0, pl.ds(c0, ncw), :].astype(BF16)
        vo = jnp.concatenate([vc_ref[0, 0, pl.ds(c0, ncw), :], ov_ref[pl.ds(c0, ncw), :]],
                             axis=1).astype(BF16)
        sc_ref[:, :ncw] = _dot_t(q2, kc)
        softmax_strips(sc_ref, pc_ref, ncw, lambda h, r: cb_ref[h, r:r + STRIP, :])
        r = _dot(pc_ref[:, :ncw], vo)
        t_row = qi * QT + (lax.broadcasted_iota(jnp.int32, (R, 1), 0) & (QT - 1))
        r = jnp.where(t_row >= CMP_BLOCK - 1, r / r[:, FLAG_LANE:FLAG_LANE + 1], 0.0)
        oc_ref[sub] = r[:, :LANES]
        imp = r[0:QT, LANES:]
        for h in range(1, HG):
            imp = imp + r[h * QT:(h + 1) * QT, LANES:]

        kst = jnp.maximum(qi - WINDOW // QT, 0)
        w0 = pl.multiple_of(kst * QT, QT)
        ids = []
        for u in range(N_WSUB):
            d = qi - kst - u
            ids.append(jnp.where(d < 0, WIN_MASK_TILE, d))
        sw_ref[:, :WIN_KEYS] = _dot_t(q2, kw_ref[0, 0, pl.ds(w0, WIN_KEYS), :])
        softmax_strips(sw_ref, pw_ref, WIN_KEYS,
                       lambda h, r: jnp.concatenate([tblw_ref[h, t, r:r + STRIP, :] for t in ids], axis=1))
        acc_w = _dot(pw_ref[:, :WIN_KEYS], vw_ref[0, 0, pl.ds(w0, WIN_KEYS), :])
        ow_ref[sub] = acc_w / acc_w[:, FLAG_LANE:FLAG_LANE + 1]

        blk = lax.broadcasted_iota(jnp.int32, (LANES, QT), 0)
        blk_t = (qi * QT + lax.broadcasted_iota(jnp.int32, (LANES, QT), 1)) // SLC_BLOCK
        forced = (blk == 0) | (blk == blk_t) | (blk == blk_t - 1)
        score = jnp.where(forced, REMOVED, jnp.where(blk <= blk_t, imp.T, NEG))
        sel = jnp.where(forced, 1.0, 0.0)
        for _ in range(N_SELECT - N_FORCED):
            top = jnp.max(score, axis=0, keepdims=True)
            first = jnp.min(jnp.where(score == top, blk, LANES), axis=0, keepdims=True)
            hit = blk == first
            sel = jnp.where(hit, 1.0, sel)
            score = jnp.where(hit, REMOVED, score)
        sel_neg = ((sel - 1.0) * BIG).T.astype(BF16)
        lhs_ref[sub] = jnp.concatenate([q2, jnp.concatenate([sel_neg] * HG, axis=0)], axis=1)

    def selected_stage(sub):
        qi = pl.program_id(2) * TILES_PER_STEP + sub
        m_ref[...] = jnp.full((1, R), REMOVED, F32)
        acct_ref[...] = jnp.zeros((LANES, R), F32)
        n_chunks = qi // N_SUB + 1

        def scores(s_ref, kj):
            k0 = pl.multiple_of(kj * KCHUNK, KCHUNK)
            s_ref[:, :R] = _dot_t(ks_ref[0, 0, pl.ds(k0, KCHUNK), :], lhs_ref[sub])

        def accumulate(s_ref, p_ref, al_ref, kj):
            for h in range(HG):
                cols = slice(h * QT, (h + 1) * QT)
                mx = m_ref[:, cols]
                for u in range(N_SUB):
                    keys = slice(u * QT, (u + 1) * QT)
                    d = qi - kj * N_SUB - u
                    tile = jnp.where(d < 0, MASK_TILE, jnp.minimum(d, FAR_TILE))
                    sb = s_ref[keys, cols] + tblt_ref[h, tile]
                    s_ref[keys, cols] = sb
                    mx = jnp.maximum(mx, jnp.max(sb, axis=0, keepdims=True))
                al_ref[:, cols] = jnp.exp2(m_ref[:, cols] - mx)
                m_ref[:, cols] = mx
                p_ref[:, cols] = jnp.exp2(s_ref[:, cols] - mx).astype(BF16)
            acct_ref[...] = al_ref[...] * acct_ref[...] + _dot(vst_ref[0, 0, kj], p_ref[:, :R])

        scores(sa_ref, 0)

        def chunk_pair(j, carry):
            a = 2 * j
            scores(sb_ref, a + 1)
            accumulate(sa_ref, pa_ref, ala_ref, a)
            scores(sa_ref, jnp.minimum(a + 2, n_chunks - 1))
            accumulate(sb_ref, pb_ref, alb_ref, a + 1)
            return carry

        lax.fori_loop(0, n_chunks // 2, chunk_pair, 0)

        @pl.when(n_chunks % 2 == 1)
        def _():
            accumulate(sa_ref, pa_ref, ala_ref, n_chunks - 1)

        acct = acct_ref[...]
        o_st = acct / acct[FLAG_LANE:FLAG_LANE + 1, :]

        tile_rows = slice(sub * QT, (sub + 1) * QT)
        gt = gt_ref[0, tile_rows, :]
        o_c = oc_ref[sub]
        o_w = ow_ref[sub]
        outs = []
        for h in range(HG):
            rows = slice(h * QT, (h + 1) * QT)
            c = h * N_BRANCH
            outs.append(gt[:, c:c + 1] * o_c[rows] + gt[:, c + 1:c + 2] * o_st[:, rows].T
                        + gt[:, c + 2:c + 3] * o_w[rows])
        low = lax.broadcasted_iota(jnp.int32, (QT, LANES), 1) < HEAD_DIM
        packed = [jnp.where(low, outs[h], pltpu.roll(outs[h + 1], HEAD_DIM, axis=1)) for h in range(0, HG, 2)]
        o_ref[0, tile_rows, :] = jnp.concatenate(packed, axis=1).astype(BF16)

    front_stage(0, sa_ref, pa_ref, sb_ref, pb_ref)
    for sub in range(1, TILES_PER_STEP):
        front_stage(sub, pre_s_ref.at[2 * sub - 2], pre_p_ref.at[2 * sub - 2],
                    pre_s_ref.at[2 * sub - 1], pre_p_ref.at[2 * sub - 1])
    for sub in range(TILES_PER_STEP):
        selected_stage(sub)


def _overlap_table(seq):
    ncw = seq // CMP_STRIDE
    front = ncw - QT // CMP_STRIDE
    nc = (seq - CMP_BLOCK) // CMP_STRIDE + 1
    ns = seq // SLC_BLOCK
    c_start = jnp.arange(nc) * CMP_STRIDE
    s_start = jnp.arange(ns) * SLC_BLOCK
    ov = (jnp.clip(jnp.minimum(c_start[:, None] + CMP_BLOCK, s_start[None, :] + SLC_BLOCK)
                   - jnp.maximum(c_start[:, None], s_start[None, :]), 0) / CMP_STRIDE).astype(F32)
    return jnp.pad(ov, ((front, ncw - nc), (0, LANES - ns)))


def _nsa(q, gates, ks, vst, kw, vw, kc, vc, tblt, tblw, cb):
    batch, _, seq, _ = q.shape
    G, HG = N_KV_GROUPS, HEADS_PER_GROUP
    ncw = seq // CMP_STRIDE
    crow = kc.shape[2]
    ov = _overlap_table(seq)
    rows = HG * QT
    assert rows == KCHUNK
    width = max(ncw, WIN_KEYS, KCHUNK)
    kvspec = lambda lanes: pl.BlockSpec((1, 1, seq, lanes), lambda g, b, i: (b, g, 0, 0))
    cspec = pl.BlockSpec((1, 1, crow, LANES), lambda g, b, i: (b, g, 0, 0))
    per_group = lambda n, w: pl.BlockSpec((HG, n, QT, w), lambda g, b, i: (g, 0, 0, 0),
                                          pipeline_mode=pl.Buffered(1))
    tq = TILES_PER_STEP * QT
    n_pre = 2 * (TILES_PER_STEP - 1)
    return pl.pallas_call(
        functools.partial(_nsa_body, ncw=ncw),
        grid=(G, batch, seq // tq),
        in_specs=[
            pl.BlockSpec((1, HG, tq, LANES), lambda g, b, i: (b, g, i, 0)),
            pl.BlockSpec((1, tq, LANES), lambda g, b, i: (b, i, g)),
            kvspec(2 * LANES),
            pl.BlockSpec((1, 1, seq // KCHUNK, LANES, KCHUNK), lambda g, b, i: (b, g, 0, 0, 0)),
            kvspec(LANES), kvspec(LANES),
            cspec, cspec,
            _const_spec(ov.shape),
            per_group(N_TILES, QT), per_group(WIN_TILES + 1, QT),
            pl.BlockSpec((HG, QT, ncw), lambda g, b, i: (g, 0, 0), pipeline_mode=pl.Buffered(1)),
        ],
        out_specs=pl.BlockSpec((1, tq, HG * HEAD_DIM), lambda g, b, i: (b, i, g)),
        out_shape=jax.ShapeDtypeStruct((batch, seq, N_HEADS * HEAD_DIM), BF16),
        scratch_shapes=[pltpu.VMEM((rows, width), F32), pltpu.VMEM((rows, width), F32),
                        pltpu.VMEM((rows, width), BF16), pltpu.VMEM((rows, width), BF16),
                        pltpu.VMEM((n_pre, rows, width), F32), pltpu.VMEM((n_pre, rows, width), BF16),
                        pltpu.VMEM((TILES_PER_STEP, rows, LANES), F32),
                        pltpu.VMEM((TILES_PER_STEP, rows, LANES), F32),
                        pltpu.VMEM((TILES_PER_STEP, rows, 2 * LANES), BF16),
                        pltpu.VMEM((1, rows), F32), pltpu.VMEM((1, rows), F32), pltpu.VMEM((1, rows), F32),
                        pltpu.VMEM((LANES, rows), F32)],
        compiler_params=_cparams(("arbitrary", "arbitrary", "arbitrary")),
        name="nsa",
    )(q, gates, ks, vst, kw, vw, kc, vc, ov, tblt, tblw, cb)


def kernel(x, ffn1_norm, ffn1_w_in, ffn1_w_out, mix_norm, ffn2_norm, ffn2_w_in, ffn2_w_out, conv_w_in, conv_w, conv_w_out, attn_w_q, attn_q_norm, attn_w_o, kv_norm, w_kv, k_norm, cmp_pe_k, cmp_pe_v, cmp_w1_k, cmp_w2_k, cmp_w1_v, cmp_w2_v, rel_bias):
    batch, seq, d = x.shape
    depth = ffn1_norm.shape[0]
    n_a = conv_w_in.shape[0]
    assert seq % KCHUNK == 0 and seq // SLC_BLOCK <= LANES and seq >= WIN_KEYS
    h = x.reshape(batch * seq, d)
    kv = tables = None
    for i in range(depth):
        if i == n_a:
            kc_raw, vc_raw, ks, vst, kw, vw = _kv_proj(h, kv_norm, w_kv, k_norm, batch=batch)
            kc, vc = _compress(kc_raw, vc_raw, cmp_pe_k, cmp_pe_v, cmp_w1_k, cmp_w2_k,
                               cmp_w1_v, cmp_w2_v, k_norm)
            kv = (ks, vst, kw, vw, kc, vc)
            tables = _bias_tables(rel_bias, seq)
        h = _ffn(h, ffn1_norm[i], ffn1_w_in[i], ffn1_w_out[i])
        mixer_out = None
        if i < n_a:
            h = _conv_mixer(h, mix_norm[i], conv_w_in[i], conv_w[i], conv_w_out[i], batch=batch)
        else:
            j = i - n_a
            q, gates = _q_proj(h, mix_norm[i], attn_w_q[j], attn_q_norm[j], batch=batch)
            attn = _nsa(q, gates, *kv, *tables)
            mixer_out = (attn.reshape(batch * seq, -1), attn_w_o[j])
        h = _ffn(h, ffn2_norm[i], ffn2_w_in[i], ffn2_w_out[i], mixer_out)
    return h.reshape(batch, seq, d)
```

```python
import functools
import math

import jax
import jax.numpy as jnp
from jax import lax
from jax.experimental import pallas as pl
from jax.experimental.pallas import tpu as pltpu

F32 = jnp.float32
BF16 = jnp.bfloat16

N_HEADS = 16
N_KV_GROUPS = 4
HEADS_PER_GROUP = N_HEADS // N_KV_GROUPS
HEAD_DIM = 64
N_BRANCH = 3
CMP_BLOCK = 32
CMP_STRIDE = 16
SLC_BLOCK = 64
N_SELECT = 16
N_FORCED = 3
WINDOW = 512
N_BUCKETS = 32
MAX_EXACT = N_BUCKETS // 2
MAX_DISTANCE = 4096
EPS = 1e-6
NEG = -1e30
BIG = 1e30
REMOVED = -3e38
LOG2E = math.log2(math.e)

LANES = 128
SUBLANES = 8
VMEM_LIMIT = 56 * 1024 * 1024

QT = 128
TILES_PER_STEP = 4
KCHUNK = 512
N_SUB = KCHUNK // QT
STRIP = 32
WIN_KEYS = WINDOW + QT
N_WSUB = WIN_KEYS // QT
FAR_TILE = int(math.ceil((MAX_EXACT * (MAX_DISTANCE / MAX_EXACT) ** ((N_BUCKETS - MAX_EXACT - 1) / (N_BUCKETS - MAX_EXACT))
                          + QT) / QT))
MASK_TILE = FAR_TILE + 1
N_TILES = FAR_TILE + 2
WIN_TILES = WINDOW // QT + 1
WIN_MASK_TILE = WIN_TILES
FLAG_LANE = HEAD_DIM


def _cparams(sem):
    return pltpu.CompilerParams(dimension_semantics=sem, vmem_limit_bytes=VMEM_LIMIT)


def _const_spec(shape):
    nd = len(shape)
    return pl.BlockSpec(shape, lambda *_: (0,) * nd, pipeline_mode=pl.Buffered(1))


def _rms(x, g, n):
    ms = jnp.sum(x * x, axis=-1, keepdims=True) * (1.0 / n)
    return x * lax.rsqrt(ms + EPS) * g


def _dot(a, b):
    return jnp.dot(a, b, preferred_element_type=F32)


def _dot_t(a, b):
    return lax.dot_general(a, b, (((1,), (1,)), ((), ())), preferred_element_type=F32)


def _ffn_body(*refs, n_chunks, d_model, has_mixer_out):
    if has_mixer_out:
        x_ref, a_ref, wa_ref, g_ref, wg_ref, wu_ref, wo_ref, o_ref, acc_ref = refs
        x = x_ref[...] + _dot(a_ref[...], wa_ref[...])
    else:
        x_ref, g_ref, wg_ref, wu_ref, wo_ref, o_ref, acc_ref = refs
        x = x_ref[...]
    xn = _rms(x, g_ref[...], d_model).astype(BF16)
    acc_ref[...] = jnp.zeros_like(acc_ref)

    def chunk(c, carry):
        gate = _dot(xn, wg_ref[c])
        up = _dot(xn, wu_ref[c])
        act = (gate * (1.0 / (1.0 + jnp.exp(-gate))) * up).astype(BF16)
        acc_ref[...] += _dot(act, wo_ref[c])
        return carry

    lax.fori_loop(0, n_chunks, chunk, 0, unroll=True)
    o_ref[...] = x + 0.5 * acc_ref[...]


def _ffn(h, norm_g, w_in, w_out, mixer_out=None, *, tm=1024, ck=256):
    rows, d = h.shape
    dff = w_out.shape[0]
    nch = dff // ck
    wg = w_in[:, :dff].reshape(d, nch, ck).transpose(1, 0, 2).astype(BF16)
    wu = w_in[:, dff:].reshape(d, nch, ck).transpose(1, 0, 2).astype(BF16)
    wo = w_out.reshape(nch, ck, d).astype(BF16)
    row_spec = lambda width: pl.BlockSpec((tm, width), lambda i: (i, 0))
    operands, specs = [h], [row_spec(d)]
    if mixer_out is not None:
        a, wa = mixer_out
        operands += [a, wa.astype(BF16)]
        specs += [row_spec(a.shape[1]), _const_spec(wa.shape)]
    operands += [norm_g.reshape(1, d), wg, wu, wo]
    specs += [_const_spec((1, d)), _const_spec((nch, d, ck)), _const_spec((nch, d, ck)), _const_spec((nch, ck, d))]
    return pl.pallas_call(
        functools.partial(_ffn_body, n_chunks=nch, d_model=d, has_mixer_out=mixer_out is not None),
        grid=(rows // tm,),
        in_specs=specs,
        out_specs=row_spec(d),
        out_shape=jax.ShapeDtypeStruct((rows, d), F32),
        scratch_shapes=[pltpu.VMEM((tm, d), F32)],
        compiler_params=_cparams(("parallel",)),
        name="ffn",
    )(*operands)


def _conv_body(x_ref, g_ref, win_ref, cw_ref, wout_ref, o_ref, ubuf_ref, *, tm, d_model):
    i = pl.program_id(1)
    x = x_ref[0]
    xn = _rms(x, g_ref[...], d_model).astype(BF16)
    b_gate = _dot(xn, win_ref[0])
    c_gate = _dot(xn, win_ref[1])
    v = _dot(xn, win_ref[2])
    u = c_gate * v

    @pl.when(i == 0)
    def _():
        ubuf_ref[0:SUBLANES, :] = jnp.zeros((SUBLANES, d_model), F32)

    ubuf_ref[SUBLANES:SUBLANES + tm, :] = u
    u1 = ubuf_ref[SUBLANES - 1:SUBLANES - 1 + tm, :]
    u2 = ubuf_ref[SUBLANES - 2:SUBLANES - 2 + tm, :]
    y = cw_ref[2:3, :] * u + cw_ref[1:2, :] * u1 + cw_ref[0:1, :] * u2
    ubuf_ref[0:SUBLANES, :] = u[tm - SUBLANES:, :]
    o_ref[0] = x + _dot((b_gate * y).astype(BF16), wout_ref[...])


def _conv_mixer(h, norm_g, w_in, conv_w, w_out, *, batch, tm=512):
    rows, d = h.shape
    seq = rows // batch
    win = w_in.reshape(d, 3, d).transpose(1, 0, 2).astype(BF16)
    out = pl.pallas_call(
        functools.partial(_conv_body, tm=tm, d_model=d),
        grid=(batch, seq // tm),
        in_specs=[
            pl.BlockSpec((1, tm, d), lambda b, i: (b, i, 0)),
            _const_spec((1, d)),
            _const_spec((3, d, d)),
            _const_spec((conv_w.shape[0], d)),
            _const_spec((d, d)),
        ],
        out_specs=pl.BlockSpec((1, tm, d), lambda b, i: (b, i, 0)),
        out_shape=jax.ShapeDtypeStruct((batch, seq, d), F32),
        scratch_shapes=[pltpu.VMEM((tm + SUBLANES, d), F32)],
        compiler_params=_cparams(("arbitrary", "arbitrary")),
        name="conv_mixer",
    )(h.reshape(batch, seq, d), norm_g.reshape(1, d), win, conv_w, w_out.astype(BF16))
    return out.reshape(rows, d)


def _pad_head_cols(w, n_slices):
    d = w.shape[0]
    w = w.reshape(d, n_slices, HEAD_DIM)
    return jnp.pad(w, ((0, 0), (0, 0), (0, LANES - HEAD_DIM))).reshape(d, n_slices * LANES)


def _pad_lanes(v):
    return jnp.pad(v, [(0, 0)] * (v.ndim - 1) + [(0, LANES - HEAD_DIM)])


def _kvproj_body(x_ref, g_ref, w_ref, wvt_ref, kn_ref, kc_ref, vc_ref, ks_ref, vst_ref, kw_ref, vw_ref,
                 fold_ref, *, tm, d_model):
    i = pl.program_id(1)
    xn = _rms(x_ref[0], g_ref[...], d_model).astype(BF16)
    lane = lax.broadcasted_iota(jnp.int32, (tm, LANES), 1)
    ones_col = jnp.where(lane == FLAG_LANE, 1.0, 0.0)
    ones_rows = jnp.where(lax.broadcasted_iota(jnp.int32, (LANES - HEAD_DIM, tm), 0) == 0, 1.0, 0.0)
    blk = (i * tm + lax.broadcasted_iota(jnp.int32, (tm, LANES), 0)) // SLC_BLOCK
    onehot = jnp.where(lane == blk, 1.0, 0.0).astype(BF16)
    G = N_KV_GROUPS
    wide = [_dot(xn, w_ref[p]) for p in range(w_ref.shape[0])]
    pairs = [w2[:, c:c + LANES] for w2 in wide for c in (0, LANES)]
    vst = _dot_t(wvt_ref[...], xn)
    for g in range(G):
        def col(br, kv):
            s = (br * 2 + kv) * G + g
            blk2 = pairs[s // 2] if s % 2 == 0 else pltpu.roll(pairs[s // 2], HEAD_DIM, axis=1)
            return jnp.where(lane < HEAD_DIM, blk2, 0.0)
        for kv, out_ref in ((0, kc_ref), (1, vc_ref)):
            stage = fold_ref.at[2 * g + kv]
            stage[...] = col(0, kv)
            for r in range(CMP_STRIDE):
                out_ref[0, g, :, r * LANES:(r + 1) * LANES] = stage[pl.ds(r, tm // CMP_STRIDE, stride=CMP_STRIDE), :]
        ks = _rms(col(1, 0), kn_ref[1:2, :], HEAD_DIM).astype(BF16)
        ks_ref[0, g] = jnp.concatenate([ks, onehot], axis=1)
        vst_ref[0, g] = jnp.concatenate([vst[g * HEAD_DIM:(g + 1) * HEAD_DIM], ones_rows], axis=0).astype(BF16)
        kw_ref[0, g] = _rms(col(2, 0), kn_ref[2:3, :], HEAD_DIM).astype(BF16)
        vw_ref[0, g] = (col(2, 1) + ones_col).astype(BF16)


def _kv_proj(h, kv_norm, w_kv, k_norm, *, batch, tm=512):
    rows, d = h.shape
    seq = rows // batch
    G = N_KV_GROUPS
    n_sl = N_BRANCH * 2 * G
    n_wide = n_sl * HEAD_DIM // (2 * LANES)
    w = w_kv.reshape(d, n_wide, 2 * LANES).transpose(1, 0, 2).astype(BF16)
    v_sel = (1 * 2 + 1) * G * HEAD_DIM
    wvt = w_kv[:, v_sel:v_sel + G * HEAD_DIM].T.astype(BF16)
    kn = _pad_lanes(k_norm)
    sds = lambda lanes, dt: jax.ShapeDtypeStruct((batch, G, seq, lanes), dt)
    ospec = lambda lanes: pl.BlockSpec((1, G, tm, lanes), lambda b, i: (b, 0, i, 0))
    fsds = jax.ShapeDtypeStruct((batch, G, seq // CMP_STRIDE, CMP_STRIDE * LANES), F32)
    fspec = pl.BlockSpec((1, G, tm // CMP_STRIDE, CMP_STRIDE * LANES), lambda b, i: (b, 0, i, 0))
    kc_raw, vc_raw, ks, vst, kw, vw = pl.pallas_call(
        functools.partial(_kvproj_body, tm=tm, d_model=d),
        grid=(batch, seq // tm),
        in_specs=[
            pl.BlockSpec((1, tm, d), lambda b, i: (b, i, 0)),
            _const_spec((1, d)),
            _const_spec((n_wide, d, 2 * LANES)),
            _const_spec((G * HEAD_DIM, d)),
            _const_spec((N_BRANCH, LANES)),
        ],
        out_specs=[fspec, fspec, ospec(2 * LANES),
                   pl.BlockSpec((1, G, LANES, tm), lambda b, i: (b, 0, 0, i)), ospec(LANES), ospec(LANES)],
        out_shape=[fsds, fsds, sds(2 * LANES, BF16),
                   jax.ShapeDtypeStruct((batch, G, LANES, seq), BF16), sds(LANES, BF16), sds(LANES, BF16)],
        scratch_shapes=[pltpu.VMEM((2 * G, tm, LANES), F32)],
        compiler_params=_cparams(("parallel", "parallel")),
        name="kv_proj",
    )(h.reshape(batch, seq, d), kv_norm.reshape(1, d), w, wvt, kn)
    vst = vst.reshape(batch, G, LANES, seq // KCHUNK, KCHUNK).transpose(0, 1, 3, 2, 4)
    return kc_raw, vc_raw, ks, vst, kw, vw


def _gelu_tanh(x):
    return 0.5 * x * (1.0 + jnp.tanh(math.sqrt(2.0 / math.pi) * (x + 0.044715 * (x * x * x))))


def _compress_body(kr_ref, vr_ref, pek_ref, pev_ref, w1k_ref, w2k_ref, w1v_ref, w2v_ref, kn_ref,
                   kc_ref, vc_ref, *, ncw, front):
    row = lax.broadcasted_iota(jnp.int32, (ncw, LANES), 0)
    row_lane = lax.broadcasted_iota(jnp.int32, (ncw, LANES), 1)
    lane = lax.broadcasted_iota(jnp.int32, (front, LANES), 1)

    def mlp(r_ref, pe_ref, w1_ref, w2_ref):
        r = r_ref[0, 0]
        a = _dot((r + pe_ref[0:1, :]).astype(BF16), w1_ref[0])
        b = _dot((r + pe_ref[1:2, :]).astype(BF16), w1_ref[1])
        hid = a + pltpu.roll(b, ncw - 1, axis=0)
        out = _dot(_gelu_tanh(hid).astype(BF16), w2_ref[...])
        return jnp.where(row < ncw - 1, out, 0.0)

    kc = _rms(mlp(kr_ref, pek_ref, w1k_ref, w2k_ref), kn_ref[0:1, :], HEAD_DIM)
    vc = mlp(vr_ref, pev_ref, w1v_ref, w2v_ref)
    kc_ref[0, 0, 0:front, :] = jnp.where(lane == FLAG_LANE, 1.0, 0.0)
    vc_ref[0, 0, 0:front, :] = jnp.zeros((front, LANES), F32)
    kc_ref[0, 0, front:front + ncw, :] = kc
    vc_ref[0, 0, front:front + ncw, :] = vc + jnp.where(row_lane == FLAG_LANE, 1.0, 0.0)


def _compress(kc_raw, vc_raw, pe_k, pe_v, w1_k, w2_k, w1_v, w2_v, k_norm):
    batch, G, ncw, tok = kc_raw.shape
    front = ncw - QT // CMP_STRIDE
    hid = w1_k.shape[1]

    def prep_w1(w1):
        w = w1.reshape(CMP_BLOCK, HEAD_DIM, hid)
        w = jnp.pad(w, ((0, 0), (0, LANES - HEAD_DIM), (0, 0)))
        return w.reshape(2, tok, hid).astype(BF16)

    def prep_pe(pe):
        return _pad_lanes(pe).reshape(2, tok)

    def prep_w2(w2):
        return _pad_lanes(w2).astype(BF16)

    rspec = pl.BlockSpec((1, 1, ncw, tok), lambda b, g: (b, g, 0, 0))
    ospec = pl.BlockSpec((1, 1, front + ncw, LANES), lambda b, g: (b, g, 0, 0))
    osds = jax.ShapeDtypeStruct((batch, G, front + ncw, LANES), F32)
    return pl.pallas_call(
        functools.partial(_compress_body, ncw=ncw, front=front),
        grid=(batch, G),
        in_specs=[rspec, rspec, _const_spec((2, tok)), _const_spec((2, tok)),
                  _const_spec((2, tok, hid)), _const_spec((hid, LANES)),
                  _const_spec((2, tok, hid)), _const_spec((hid, LANES)),
                  _const_spec((N_BRANCH, LANES))],
        out_specs=[ospec, ospec],
        out_shape=[osds, osds],
        compiler_params=_cparams(("parallel", "parallel")),
        name="compress",
    )(kc_raw, vc_raw,
      prep_pe(pe_k), prep_pe(pe_v), prep_w1(w1_k), prep_w2(w2_k), prep_w1(w1_v), prep_w2(w2_v),
      _pad_lanes(k_norm))


def _qproj_body(x_ref, g_ref, wq_ref, wg_ref, qn_ref, q_ref, gt_ref, *, tm, d_model):
    xn = _rms(x_ref[0], g_ref[...], d_model).astype(BF16)
    lane = lax.broadcasted_iota(jnp.int32, (tm, LANES), 1)
    scale = HEAD_DIM ** -0.5 * LOG2E
    for p in range(N_HEADS // 2):
        pair = _dot(xn, wq_ref[p])
        for half in range(2):
            q = _rms(pair[:, half * LANES:(half + 1) * LANES], qn_ref[...], HEAD_DIM) * scale
            q_ref[0, 2 * p + half] = jnp.where(lane == FLAG_LANE, NEG, q).astype(BF16)
    gates = 1.0 / (1.0 + jnp.exp(-_dot(xn, wg_ref[...])))
    n_gate = HEADS_PER_GROUP * N_BRANCH
    for g in range(N_KV_GROUPS):
        shifted = gates if g == 0 else pltpu.roll(gates, LANES - g * n_gate, axis=1)
        gt_ref[0, :, g * LANES:(g + 1) * LANES] = shifted


def _q_proj(h, norm_g, w_q, q_norm, *, batch, tm=256):
    rows, d = h.shape
    seq = rows // batch
    G, HG = N_KV_GROUPS, HEADS_PER_GROUP
    nq = N_HEADS * HEAD_DIM
    wq = _pad_head_cols(w_q[:, :nq], N_HEADS).reshape(d, N_HEADS // 2, 2 * LANES).transpose(1, 0, 2).astype(BF16)
    n_gates = w_q.shape[1] - nq
    assert n_gates <= LANES
    wg = jnp.pad(w_q[:, nq:], ((0, 0), (0, LANES - n_gates))).astype(BF16)
    return pl.pallas_call(
        functools.partial(_qproj_body, tm=tm, d_model=d),
        grid=(batch, seq // tm),
        in_specs=[
            pl.BlockSpec((1, tm, d), lambda b, i: (b, i, 0)),
            _const_spec((1, d)),
            _const_spec((N_HEADS // 2, d, 2 * LANES)),
            _const_spec((d, LANES)),
            _const_spec((1, LANES)),
        ],
        out_specs=[pl.BlockSpec((1, N_HEADS, tm, LANES), lambda b, i: (b, 0, i, 0)),
                   pl.BlockSpec((1, tm, G * LANES), lambda b, i: (b, i, 0))],
        out_shape=[jax.ShapeDtypeStruct((batch, N_HEADS, seq, LANES), BF16),
                   jax.ShapeDtypeStruct((batch, seq, G * LANES), F32)],
        compiler_params=_cparams(("parallel", "parallel")),
        name="q_proj",
    )(h.reshape(batch, seq, d), norm_g.reshape(1, d), wq, wg, _pad_lanes(q_norm.reshape(1, HEAD_DIM)))


def _tables_body(thr_ref, rb_ref, tblt_ref, tblw_ref, cb_ref, *, ncw, c_off):
    h = pl.program_id(0)

    def bias_of(rel):
        v = jnp.full(rel.shape, rb_ref[0, h], F32)
        for k in range(1, N_BUCKETS):
            v = jnp.where(rel >= thr_ref[k], rb_ref[k, h], v)
        return v * LOG2E

    row = lax.broadcasted_iota(jnp.int32, (QT, QT), 0)
    col = lax.broadcasted_iota(jnp.int32, (QT, QT), 1)

    def tile_t(d, carry):
        rel = (col - row) + d * QT
        tblt_ref[0, d] = jnp.where(rel < 0, NEG, bias_of(rel))
        return carry

    lax.fori_loop(0, FAR_TILE + 1, tile_t, 0)
    tblt_ref[0, MASK_TILE] = jnp.full((QT, QT), NEG, F32)

    def tile_w(d, carry):
        rel = (row - col) + d * QT
        tblw_ref[0, d] = jnp.where((rel < 0) | (rel >= WINDOW), NEG, bias_of(rel))
        return carry

    lax.fori_loop(0, WIN_TILES, tile_w, 0)
    tblw_ref[0, WIN_MASK_TILE] = jnp.full((QT, QT), NEG, F32)
    relc = (lax.broadcasted_iota(jnp.int32, (QT, ncw), 0)
            - CMP_STRIDE * lax.broadcasted_iota(jnp.int32, (QT, ncw), 1) + c_off)
    cb_ref[0] = jnp.where(relc < 0, NEG, bias_of(relc))


def _bucket_thresholds(seq):
    n = jnp.arange(seq)
    nf = jnp.maximum(n, 1).astype(jnp.float32)
    large = MAX_EXACT + (jnp.log(nf / MAX_EXACT) / math.log(MAX_DISTANCE / MAX_EXACT)
                         * (N_BUCKETS - MAX_EXACT)).astype(jnp.int32)
    large = jnp.minimum(large, N_BUCKETS - 1)
    bucket = jnp.where(n < MAX_EXACT, n, large)
    return jnp.sum(bucket[None, :] < jnp.arange(N_BUCKETS)[:, None], axis=1).astype(jnp.int32)


def _bias_tables(rel_bias, seq):
    ncw = seq // CMP_STRIDE
    front = ncw - QT // CMP_STRIDE
    c_off = CMP_STRIDE * front - (CMP_BLOCK - 1)
    smem = pl.BlockSpec(memory_space=pltpu.SMEM)
    return pl.pallas_call(
        functools.partial(_tables_body, ncw=ncw, c_off=c_off),
        grid=(N_HEADS,),
        in_specs=[smem, smem],
        out_specs=[pl.BlockSpec((1, N_TILES, QT, QT), lambda h: (h, 0, 0, 0)),
                   pl.BlockSpec((1, WIN_TILES + 1, QT, QT), lambda h: (h, 0, 0, 0)),
                   pl.BlockSpec((1, QT, ncw), lambda h: (h, 0, 0))],
        out_shape=[jax.ShapeDtypeStruct((N_HEADS, N_TILES, QT, QT), F32),
                   jax.ShapeDtypeStruct((N_HEADS, WIN_TILES + 1, QT, QT), F32),
                   jax.ShapeDtypeStruct((N_HEADS, QT, ncw), F32)],
        compiler_params=_cparams(("parallel",)),
        name="bias_tables",
    )(_bucket_thresholds(seq), rel_bias)


def _nsa_body(q_ref, gt_ref, ks_ref, vst_ref, kw_ref, vw_ref, kc_ref, vc_ref, ov_ref, tblt_ref, tblw_ref, cb_ref,
              o_ref, sa_ref, sb_ref, pa_ref, pb_ref, pre_s_ref, pre_p_ref, oc_ref, ow_ref, lhs_ref,
              m_ref, ala_ref, alb_ref, acct_ref, *, ncw):
    HG = HEADS_PER_GROUP
    R = HG * QT

    def softmax_strips(s_ref, p_ref, width, bias_fn):
        for st in range(R // STRIP):
            rows = slice(st * STRIP, (st + 1) * STRIP)
            s = s_ref[rows, :width] + bias_fn((st * STRIP) // QT, (st * STRIP) % QT)
            p_ref[rows, :width] = jnp.exp2(s - jnp.max(s, axis=-1, keepdims=True)).astype(BF16)

    def front_stage(sub, sc_ref, pc_ref, sw_ref, pw_ref):
        qi = pl.program_id(2) * TILES_PER_STEP + sub
        q2 = q_ref[0, :, sub * QT:(sub + 1) * QT, :].reshape(R, LANES)

        c0 = pl.multiple_of(qi * (QT // CMP_STRIDE), SUBLANES)
        kc = kc_ref[0, 0, pl.ds(c0, ncw), :].astype(BF16)
        vo = jnp.concatenate([vc_ref[0, 0, pl.ds(c0, ncw), :], ov_ref[pl.ds(c0, ncw), :]],
                             axis=1).astype(BF16)
        sc_ref[:, :ncw] = _dot_t(q2, kc)
        softmax_strips(sc_ref, pc_ref, ncw, lambda h, r: cb_ref[h, r:r + STRIP, :])
        r = _dot(pc_ref[:, :ncw], vo)
        t_row = qi * QT + (lax.broadcasted_iota(jnp.int32, (R, 1), 0) & (QT - 1))
        r = jnp.where(t_row >= CMP_BLOCK - 1, r / r[:, FLAG_LANE:FLAG_LANE + 1], 0.0)
        oc_ref[sub] = r[:, :LANES]
        imp = r[0:QT, LANES:]
        for h in range(1, HG):
            imp = imp + r[h * QT:(h + 1) * QT, LANES:]

        kst = jnp.maximum(qi - WINDOW // QT, 0)
        w0 = pl.multiple_of(kst * QT, QT)
        ids = []
        for u in range(N_WSUB):
            d = qi - kst - u
            ids.append(jnp.where(d < 0, WIN_MASK_TILE, d))
        sw_ref[:, :WIN_KEYS] = _dot_t(q2, kw_ref[0, 0, pl.ds(w0, WIN_KEYS), :])
        softmax_strips(sw_ref, pw_ref, WIN_KEYS,
                       lambda h, r: jnp.concatenate([tblw_ref[h, t, r:r + STRIP, :] for t in ids], axis=1))
        acc_w = _dot(pw_ref[:, :WIN_KEYS], vw_ref[0, 0, pl.ds(w0, WIN_KEYS), :])
        ow_ref[sub] = acc_w / acc_w[:, FLAG_LANE:FLAG_LANE + 1]

        blk = lax.broadcasted_iota(jnp.int32, (LANES, QT), 0)
        blk_t = (qi * QT + lax.broadcasted_iota(jnp.int32, (LANES, QT), 1)) // SLC_BLOCK
        forced = (blk == 0) | (blk == blk_t) | (blk == blk_t - 1)
        score = jnp.where(forced, REMOVED, jnp.where(blk <= blk_t, imp.T, NEG))
        sel = jnp.where(forced, 1.0, 0.0)
        for _ in range(N_SELECT - N_FORCED):
            top = jnp.max(score, axis=0, keepdims=True)
            first = jnp.min(jnp.where(score == top, blk, LANES), axis=0, keepdims=True)
            hit = blk == first
            sel = jnp.where(hit, 1.0, sel)
            score = jnp.where(hit, REMOVED, score)
        sel_neg = ((sel - 1.0) * BIG).T.astype(BF16)
        lhs_ref[sub] = jnp.concatenate([q2, jnp.concatenate([sel_neg] * HG, axis=0)], axis=1)

    def selected_stage(sub):
        qi = pl.program_id(2) * TILES_PER_STEP + sub
        m_ref[...] = jnp.full((1, R), REMOVED, F32)
        acct_ref[...] = jnp.zeros((LANES, R), F32)
        n_chunks = qi // N_SUB + 1

        def scores(s_ref, kj):
            k0 = pl.multiple_of(kj * KCHUNK, KCHUNK)
            s_ref[:, :R] = _dot_t(ks_ref[0, 0, pl.ds(k0, KCHUNK), :], lhs_ref[sub])

        def accumulate(s_ref, p_ref, al_ref, kj):
            for h in range(HG):
                cols = slice(h * QT, (h + 1) * QT)
                mx = m_ref[:, cols]
                for u in range(N_SUB):
                    keys = slice(u * QT, (u + 1) * QT)
                    d = qi - kj * N_SUB - u
                    tile = jnp.where(d < 0, MASK_TILE, jnp.minimum(d, FAR_TILE))
                    sb = s_ref[keys, cols] + tblt_ref[h, tile]
                    s_ref[keys, cols] = sb
                    mx = jnp.maximum(mx, jnp.max(sb, axis=0, keepdims=True))
                al_ref[:, cols] = jnp.exp2(m_ref[:, cols] - mx)
                m_ref[:, cols] = mx
                p_ref[:, cols] = jnp.exp2(s_ref[:, cols] - mx).astype(BF16)
            acct_ref[...] = al_ref[...] * acct_ref[...] + _dot(vst_ref[0, 0, kj], p_ref[:, :R])

        scores(sa_ref, 0)

        def chunk_pair(j, carry):
            a = 2 * j
            scores(sb_ref, a + 1)
            accumulate(sa_ref, pa_ref, ala_ref, a)
            scores(sa_ref, jnp.minimum(a + 2, n_chunks - 1))
            accumulate(sb_ref, pb_ref, alb_ref, a + 1)
            return carry

        lax.fori_loop(0, n_chunks // 2, chunk_pair, 0)

        @pl.when(n_chunks % 2 == 1)
        def _():
            accumulate(sa_ref, pa_ref, ala_ref, n_chunks - 1)

        acct = acct_ref[...]
        o_st = acct / acct[FLAG_LANE:FLAG_LANE + 1, :]

        tile_rows = slice(sub * QT, (sub + 1) * QT)
        gt = gt_ref[0, tile_rows, :]
        o_c = oc_ref[sub]
        o_w = ow_ref[sub]
        outs = []
        for h in range(HG):
            rows = slice(h * QT, (h + 1) * QT)
            c = h * N_BRANCH
            outs.append(gt[:, c:c + 1] * o_c[rows] + gt[:, c + 1:c + 2] * o_st[:, rows].T
                        + gt[:, c + 2:c + 3] * o_w[rows])
        low = lax.broadcasted_iota(jnp.int32, (QT, LANES), 1) < HEAD_DIM
        packed = [jnp.where(low, outs[h], pltpu.roll(outs[h + 1], HEAD_DIM, axis=1)) for h in range(0, HG, 2)]
        o_ref[0, tile_rows, :] = jnp.concatenate(packed, axis=1).astype(BF16)

    front_stage(0, sa_ref, pa_ref, sb_ref, pb_ref)
    for sub in range(1, TILES_PER_STEP):
        front_stage(sub, pre_s_ref.at[2 * sub - 2], pre_p_ref.at[2 * sub - 2],
                    pre_s_ref.at[2 * sub - 1], pre_p_ref.at[2 * sub - 1])
    for sub in range(TILES_PER_STEP):
        selected_stage(sub)


def _overlap_table(seq):
    ncw = seq // CMP_STRIDE
    front = ncw - QT // CMP_STRIDE
    nc = (seq - CMP_BLOCK) // CMP_STRIDE + 1
    ns = seq // SLC_BLOCK
    c_start = jnp.arange(nc) * CMP_STRIDE
    s_start = jnp.arange(ns) * SLC_BLOCK
    ov = (jnp.clip(jnp.minimum(c_start[:, None] + CMP_BLOCK, s_start[None, :] + SLC_BLOCK)
                   - jnp.maximum(c_start[:, None], s_start[None, :]), 0) / CMP_STRIDE).astype(F32)
    return jnp.pad(ov, ((front, ncw - nc), (0, LANES - ns)))


def _nsa(q, gates, ks, vst, kw, vw, kc, vc, tblt, tblw, cb):
    batch, _, seq, _ = q.shape
    G, HG = N_KV_GROUPS, HEADS_PER_GROUP
    ncw = seq // CMP_STRIDE
    crow = kc.shape[2]
    ov = _overlap_table(seq)
    rows = HG * QT
    assert rows == KCHUNK
    width = max(ncw, WIN_KEYS, KCHUNK)
    kvspec = lambda lanes: pl.BlockSpec((1, 1, seq, lanes), lambda g, b, i: (b, g, 0, 0))
    cspec = pl.BlockSpec((1, 1, crow, LANES), lambda g, b, i: (b, g, 0, 0))
    per_group = lambda n, w: pl.BlockSpec((HG, n, QT, w), lambda g, b, i: (g, 0, 0, 0),
                                          pipeline_mode=pl.Buffered(1))
    tq = TILES_PER_STEP * QT
    n_pre = 2 * (TILES_PER_STEP - 1)
    return pl.pallas_call(
        functools.partial(_nsa_body, ncw=ncw),
        grid=(G, batch, seq // tq),
        in_specs=[
            pl.BlockSpec((1, HG, tq, LANES), lambda g, b, i: (b, g, i, 0)),
            pl.BlockSpec((1, tq, LANES), lambda g, b, i: (b, i, g)),
            kvspec(2 * LANES),
            pl.BlockSpec((1, 1, seq // KCHUNK, LANES, KCHUNK), lambda g, b, i: (b, g, 0, 0, 0)),
            kvspec(LANES), kvspec(LANES),
            cspec, cspec,
            _const_spec(ov.shape),
            per_group(N_TILES, QT), per_group(WIN_TILES + 1, QT),
            pl.BlockSpec((HG, QT, ncw), lambda g, b, i: (g, 0, 0), pipeline_mode=pl.Buffered(1)),
        ],
        out_specs=pl.BlockSpec((1, tq, HG * HEAD_DIM), lambda g, b, i: (b, i, g)),
        out_shape=jax.ShapeDtypeStruct((batch, seq, N_HEADS * HEAD_DIM), BF16),
        scratch_shapes=[pltpu.VMEM((rows, width), F32), pltpu.VMEM((rows, width), F32),
                        pltpu.VMEM((rows, width), BF16), pltpu.VMEM((rows, width), BF16),
                        pltpu.VMEM((n_pre, rows, width), F32), pltpu.VMEM((n_pre, rows, width), BF16),
                        pltpu.VMEM((TILES_PER_STEP, rows, LANES), F32),
                        pltpu.VMEM((TILES_PER_STEP, rows, LANES), F32),
                        pltpu.VMEM((TILES_PER_STEP, rows, 2 * LANES), BF16),
                        pltpu.VMEM((1, rows), F32), pltpu.VMEM((1, rows), F32), pltpu.VMEM((1, rows), F32),
                        pltpu.VMEM((LANES, rows), F32)],
        compiler_params=_cparams(("arbitrary", "arbitrary", "arbitrary")),
        name="nsa",
    )(q, gates, ks, vst, kw, vw, kc, vc, ov, tblt, tblw, cb)


def kernel(x, ffn1_norm, ffn1_w_in, ffn1_w_out, mix_norm, ffn2_norm, ffn2_w_in, ffn2_w_out, conv_w_in, conv_w, conv_w_out, attn_w_q, attn_q_norm, attn_w_o, kv_norm, w_kv, k_norm, cmp_pe_k, cmp_pe_v, cmp_w1_k, cmp_w2_k, cmp_w1_v, cmp_w2_v, rel_bias):
    batch, seq, d = x.shape
    depth = ffn1_norm.shape[0]
    n_a = conv_w_in.shape[0]
    assert seq % KCHUNK == 0 and seq // SLC_BLOCK <= LANES and seq >= WIN_KEYS
    h = x.reshape(batch * seq, d)
    kv = tables = None
    for i in range(depth):
        if i == n_a:
            kc_raw, vc_raw, ks, vst, kw, vw = _kv_proj(h, kv_norm, w_kv, k_norm, batch=batch)
            kc, vc = _compress(kc_raw, vc_raw, cmp_pe_k, cmp_pe_v, cmp_w1_k, cmp_w2_k,
                               cmp_w1_v, cmp_w2_v, k_norm)
            kv = (ks, vst, kw, vw, kc, vc)
            tables = _bias_tables(rel_bias, seq)
        h = _ffn(h, ffn1_norm[i], ffn1_w_in[i], ffn1_w_out[i])
        mixer_out = None
        if i < n_a:
            h = _conv_mixer(h, mix_norm[i], conv_w_in[i], conv_w[i], conv_w_out[i], batch=batch)
        else:
            j = i - n_a
            q, gates = _q_proj(h, mix_norm[i], attn_w_q[j], attn_q_norm[j], batch=batch)
            attn = _nsa(q, gates, *kv, *tables)
            mixer_out = (attn.reshape(batch * seq, -1), attn_w_o[j])
        h = _ffn(h, ffn2_norm[i], ffn2_w_in[i], ffn2_w_out[i], mixer_out)
    return h.reshape(batch, seq, d)
```
